```python
import jax, jax.numpy as jnp
from jax import lax
import numpy as np

D_MODEL = 1024
BATCH = 2
SEQ = 8192
DEPTH = 2

N_MIXERS = 2
N_HGRN_LAYERS = (DEPTH + 1) // 2
N_ATTN_LAYERS = DEPTH // 2

HGRN_EXPAND = 128
HGRN_HEADS = D_MODEL // HGRN_EXPAND
HGRN_KEY_DIM = HGRN_EXPAND
HGRN_VAL_DIM = D_MODEL // HGRN_HEADS
HGRN_QK_WIDTH = HGRN_HEADS * HGRN_KEY_DIM
HGRN_V_WIDTH = HGRN_HEADS * HGRN_VAL_DIM
HGRN_IN_WIDTH = 2 * HGRN_QK_WIDTH + 2 * HGRN_V_WIDTH
HGRN_CHUNK = 64

ATTN_HEAD_DIM = 128
DILATED_GROUPS = ((128, 1), (512, 4), (2048, 16))
HEADS_PER_GROUP = 4
N_GROUPS = len(DILATED_GROUPS)
ATTN_HEADS = HEADS_PER_GROUP * N_GROUPS
ATTN_WIDTH = ATTN_HEADS * ATTN_HEAD_DIM
ROPE_THETA = 10000.0

D_FF = ((8 * D_MODEL // 3 + 255) // 256) * 256

NORM_EPS = 1e-6

kernel_name = "hgrn2_dilated_swa_interleaved_trunk"


def rmsnorm(x, gain):
    xf = x.astype(jnp.float32)
    y = xf * lax.rsqrt(jnp.mean(xf * xf, axis=-1, keepdims=True) + NORM_EPS)
    return (y * gain.astype(jnp.float32)).astype(x.dtype)


def rope_tables(seq_len):
    inv_freq = 1.0 / (ROPE_THETA ** (jnp.arange(0, ATTN_HEAD_DIM, 2, dtype=jnp.float32) / ATTN_HEAD_DIM))
    pos = jnp.arange(seq_len, dtype=jnp.float32)
    ang = pos[:, None] * inv_freq[None, :]
    return jnp.cos(ang)[:, None, :], jnp.sin(ang)[:, None, :]


def apply_rope(x, cos, sin):
    xf = x.astype(jnp.float32)
    x1, x2 = jnp.split(xf, 2, axis=-1)
    out = jnp.concatenate([x1 * cos - x2 * sin, x2 * cos + x1 * sin], axis=-1)
    return out.astype(x.dtype)


def hgrn2_mixer(u, w_in, lower_bound, out_gain, w_out):
    B, S, _ = u.shape
    H, K, V, C = HGRN_HEADS, HGRN_KEY_DIM, HGRN_VAL_DIM, HGRN_CHUNK
    proj = u @ w_in
    q, f, i, g = jnp.split(proj, [HGRN_QK_WIDTH, 2 * HGRN_QK_WIDTH, 2 * HGRN_QK_WIDTH + HGRN_V_WIDTH], axis=-1)
    lb = lower_bound.astype(jnp.float32)
    forget = lb + (1.0 - lb) * jax.nn.sigmoid(f.astype(jnp.float32))
    key = 1.0 - forget
    log_f = jnp.log(forget)
    query = jax.nn.silu(q.astype(jnp.float32))
    value = i.astype(jnp.float32)

    def to_chunks(t, d):
        return t.reshape(B, S // C, C, H, d).transpose(1, 0, 3, 2, 4)

    xs = (to_chunks(query, K), to_chunks(key, K), to_chunks(value, V), to_chunks(log_f, K))
    causal = jnp.tril(jnp.ones((C, C), dtype=bool))[:, :, None]

    def chunk_step(state, inp):
        qc, kc, vc, gc = inp
        b = jnp.cumsum(gc, axis=2)
        o_inter = jnp.einsum('bhck,bhkv->bhcv', qc * jnp.exp(b), state)
        diff = b[:, :, :, None, :] - b[:, :, None, :, :]
        decay = jnp.exp(jnp.where(causal, diff, -jnp.inf))
        scores = jnp.einsum('bhtk,bhsk,bhtsk->bhts', qc, kc, decay)
        o_intra = jnp.einsum('bhts,bhsv->bhtv', scores, vc)
        b_last = b[:, :, -1, :]
        k_to_end = kc * jnp.exp(b_last[:, :, None, :] - b)
        new_state = jnp.exp(b_last)[..., None] * state + jnp.einsum('bhck,bhcv->bhkv', k_to_end, vc)
        return new_state, o_inter + o_intra

    state0 = jnp.zeros((B, H, K, V), dtype=jnp.float32)
    _, o = lax.scan(chunk_step, state0, xs)
    o = o.transpose(1, 0, 3, 2, 4).reshape(B, S, H, V)
    o = rmsnorm(o, out_gain)
    o = o * jax.nn.silu(g.astype(jnp.float32)).reshape(B, S, H, V)
    return o.reshape(B, S, H * V).astype(u.dtype) @ w_out


def dilated_window_group(q, k, v, window, dilation):
    B, S, Hg, D = q.shape
    span = window // dilation
    L = S // dilation
    nb = -(-L // span)
    Lp = nb * span

    def to_blocks(t):
        t = t.reshape(B, L, dilation, Hg, D).transpose(0, 2, 3, 1, 4)
        t = jnp.pad(t, ((0, 0), (0, 0), (0, 0), (0, Lp - L), (0, 0)))
        return t.reshape(B, dilation, Hg, nb, span, D)

    qb, kb, vb = to_blocks(q), to_blocks(k), to_blocks(v)

    def with_prev(t):
        prev = jnp.pad(t, ((0, 0), (0, 0), (0, 0), (1, 0), (0, 0), (0, 0)))[:, :, :, :-1]
        return jnp.concatenate([prev, t], axis=4)

    kw, vw = with_prev(kb), with_prev(vb)
    scores = jnp.einsum('bdhnqe,bdhnke->bdhnqk', qb, kw, preferred_element_type=jnp.float32) * (D ** -0.5)
    blk = jnp.arange(nb)[:, None, None] * span
    qpos = blk + jnp.arange(span)[None, :, None]
    kpos = blk - span + jnp.arange(2 * span)[None, None, :]
    mask = (kpos <= qpos) & (kpos >= qpos - span) & (kpos >= 0)
    scores = jnp.where(mask, scores, -jnp.inf)
    m = jnp.max(scores, axis=-1, keepdims=True)
    p = jnp.exp(scores - m)
    l = jnp.sum(p, axis=-1, keepdims=True)
    out = jnp.einsum('bdhnqk,bdhnke->bdhnqe', p, vw.astype(jnp.float32)) / l
    lse = (m + jnp.log(l))[..., 0]
    out = out.reshape(B, dilation, Hg, Lp, D)[:, :, :, :L].transpose(0, 3, 1, 2, 4).reshape(B, S, Hg, D)
    lse = lse.reshape(B, dilation, Hg, Lp)[:, :, :, :L].transpose(0, 3, 1, 2).reshape(B, S, Hg)
    return out, lse


def dilated_attention_mixer(u, w_qkv, w_out, cos, sin):
    B, S, _ = u.shape
    qkv = (u @ w_qkv).reshape(B, S, 3, ATTN_HEADS, ATTN_HEAD_DIM)
    q = apply_rope(qkv[:, :, 0], cos, sin)
    k = apply_rope(qkv[:, :, 1], cos, sin)
    v = qkv[:, :, 2]
    outs, lses = [], []
    for gi, (window, dilation) in enumerate(DILATED_GROUPS):
        hs = slice(gi * HEADS_PER_GROUP, (gi + 1) * HEADS_PER_GROUP)
        o_g, lse_g = dilated_window_group(q[:, :, hs], k[:, :, hs], v[:, :, hs], window, dilation)
        outs.append(o_g)
        lses.append(lse_g)
    o = jnp.stack(outs, axis=2)
    lse = jnp.stack(lses, axis=2)
    alpha = jax.nn.softmax(lse, axis=2)
    o = (o * alpha[..., None]).reshape(B, S, ATTN_WIDTH).astype(u.dtype)
    return o @ w_out


def swiglu_ffn(u, w_in, w_down):
    gate, up = jnp.split(u @ w_in, 2, axis=-1)
    return (jax.nn.silu(gate) * up) @ w_down


def setup_inputs(seed: int = 0) -> dict:
    key = jax.random.key(seed)
    ks = jax.random.split(key, 13)
    f32 = jnp.float32

    def dense(k, shape, fan_in):
        return jax.random.normal(k, shape, f32) * (fan_in ** -0.5)

    def gain(k, shape):
        return 1.0 + 0.02 * jax.random.normal(k, shape, f32)

    return {
        "x": jax.random.normal(ks[0], (BATCH, SEQ, D_MODEL), f32),
        "norm_mix": gain(ks[1], (DEPTH, D_MODEL)),
        "norm_ffn": gain(ks[2], (DEPTH, D_MODEL)),
        "hgrn_w_in": dense(ks[3], (N_HGRN_LAYERS, D_MODEL, HGRN_IN_WIDTH), D_MODEL),
        "hgrn_lb_logits": 0.5 * jax.random.normal(ks[4], (DEPTH + 1, HGRN_QK_WIDTH), f32),
        "hgrn_out_norm": gain(ks[5], (N_HGRN_LAYERS, HGRN_VAL_DIM)),
        "hgrn_w_out": dense(ks[6], (N_HGRN_LAYERS, HGRN_V_WIDTH, D_MODEL), HGRN_V_WIDTH),
        "attn_w_qkv": dense(ks[7], (N_ATTN_LAYERS, D_MODEL, 3 * ATTN_WIDTH), D_MODEL),
        "attn_w_out": dense(ks[8], (N_ATTN_LAYERS, ATTN_WIDTH, D_MODEL), ATTN_WIDTH),
        "ffn_w_in": dense(ks[9], (DEPTH, D_MODEL, 2 * D_FF), D_MODEL),
        "ffn_w_down": dense(ks[10], (DEPTH, D_FF, D_MODEL), D_FF),
        "final_norm": gain(ks[11], (D_MODEL,)),
    }


def reference(x, norm_mix, norm_ffn, hgrn_w_in, hgrn_lb_logits, hgrn_out_norm, hgrn_w_out,
              attn_w_qkv, attn_w_out, ffn_w_in, ffn_w_down, final_norm):
    lb_table = jnp.cumsum(jax.nn.softmax(hgrn_lb_logits.astype(jnp.float32), axis=0), axis=0)
    cos, sin = rope_tables(x.shape[1])
    h = x
    for layer in range(DEPTH):
        u = rmsnorm(h, norm_mix[layer])
        if layer % N_MIXERS == 0:
            a = layer // N_MIXERS
            mix = hgrn2_mixer(u, hgrn_w_in[a], lb_table[layer], hgrn_out_norm[a], hgrn_w_out[a])
        else:
            a = layer // N_MIXERS
            mix = dilated_attention_mixer(u, attn_w_qkv[a], attn_w_out[a], cos, sin)
        h = h + mix
        h = h + swiglu_ffn(rmsnorm(h, norm_ffn[layer]), ffn_w_in[layer], ffn_w_down[layer])
    return rmsnorm(h, final_norm)
```

```python
import functools

import numpy as np
import jax
import jax.numpy as jnp
from jax import lax
from jax.experimental import pallas as pl
from jax.experimental.pallas import tpu as pltpu

D_MODEL = 1024
HEAD_DIM = 128
HGRN_HEADS = 8
HGRN_CHUNK = 64
HGRN_STEP_TOKENS = 256
ATTN_HEADS = 12
ATTN_WIDTH = ATTN_HEADS * HEAD_DIM
DILATIONS = (1, 4, 16)
SPAN = 128
HEADS_PER_GROUP = 4
ATTN_BLOCK = SPAN * DILATIONS[-1]
ROPE_THETA = 10000.0
NORM_EPS = 1e-6
NEG_BIG = -1e30
VMEM_LIMIT_BYTES = 56 * 1024 * 1024

_F32 = jnp.float32
_BF16 = jnp.bfloat16


def _dot(a, b):
    return jnp.dot(a, b, preferred_element_type=_F32)


def _dot_nt(a, b):
    return lax.dot_general(a, b, (((1,), (1,)), ((), ())), preferred_element_type=_F32)


def _dot_tn(a, b):
    return lax.dot_general(a, b, (((0,), (0,)), ((), ())), preferred_element_type=_F32)


def _sigmoid(x):
    return 1.0 / (1.0 + jnp.exp(-x))


def _rms_scale(x):
    return lax.rsqrt(jnp.mean(x * x, axis=-1, keepdims=True) + NORM_EPS)


def _params(*sem):
    return pltpu.CompilerParams(dimension_semantics=sem, vmem_limit_bytes=VMEM_LIMIT_BYTES)


def _norm_matmul_kernel(h_ref, gain_ref, w_ref, cos_ref, sin_ref, o_ref, u_ref, *,
                        rope_tiles, scaled_tiles, scale):
    j = pl.program_id(1)

    @pl.when(j == 0)
    def _():
        x = h_ref[...]
        u_ref[...] = (x * _rms_scale(x) * gain_ref[...]).astype(_BF16)

    res = _dot(u_ref[...], w_ref[...])

    if rope_tiles == 0:
        o_ref[...] = res
    else:
        @pl.when(j >= rope_tiles)
        def _():
            o_ref[...] = res

        @pl.when(j < rope_tiles)
        def _():
            sc = jnp.where(j < scaled_tiles, scale, 1.0).astype(_F32)
            cos = cos_ref[...] * sc
            sin = sin_ref[...] * sc
            for hh in range(res.shape[1] // HEAD_DIM):
                xh = res[:, hh * HEAD_DIM:(hh + 1) * HEAD_DIM]
                o_ref[:, hh * HEAD_DIM:(hh + 1) * HEAD_DIM] = (
                    xh * cos + pltpu.roll(xh, HEAD_DIM // 2, 1) * sin)


def _norm_matmul(h, gain, w, cos2, sin2, *, tm=512, tn=512, rope_tiles=0, scaled_tiles=0,
                 scale=1.0):
    t, d = h.shape
    n = w.shape[1]
    s = cos2.shape[0]
    kern = functools.partial(_norm_matmul_kernel, rope_tiles=rope_tiles,
                             scaled_tiles=scaled_tiles, scale=scale)
    return pl.pallas_call(
        kern,
        grid=(t // tm, n // tn),
        in_specs=[
            pl.BlockSpec((tm, d), lambda i, j: (i, 0)),
            pl.BlockSpec((1, d), lambda i, j: (0, 0)),
            pl.BlockSpec((d, tn), lambda i, j: (0, j)),
            pl.BlockSpec((tm, HEAD_DIM), lambda i, j: (i % (s // tm), 0)),
            pl.BlockSpec((tm, HEAD_DIM), lambda i, j: (i % (s // tm), 0)),
        ],
        out_specs=pl.BlockSpec((tm, tn), lambda i, j: (i, j)),
        out_shape=jax.ShapeDtypeStruct((t, n), _F32),
        scratch_shapes=[pltpu.VMEM((tm, d), _BF16)],
        compiler_params=_params("parallel", "arbitrary"),
        name="norm_matmul",
    )(h, gain.reshape(1, d), w, cos2, sin2)


def _hgrn_tables():
    c = HGRN_CHUNK
    t = np.arange(c)
    col = t[None, :]
    row = t[:, None]
    sums = np.zeros((8, c, c), np.float32)
    sums[0] = col <= row
    sums[1] = col > row
    masks = np.zeros((7, c, c), np.float32)
    half = c // 2
    level = 0
    while half >= 1:
        block = t // (2 * half)
        mid = block * 2 * half + half
        is_query = t >= mid
        q_rows = (col >= mid[:, None]) & (col <= row) & is_query[:, None]
        k_rows = (col > row) & (col < mid[:, None]) & (~is_query)[:, None]
        sums[2 + level] = q_rows | k_rows
        masks[level] = ((block[:, None] == block[None, :]) & is_query[:, None]
                        & (~is_query)[None, :])
        half //= 2
        level += 1
    masks[6] = np.eye(c)
    assert level == 6 and np.array_equal(masks.sum(0), np.tril(np.ones((c, c))))
    return sums.reshape(8 * c, c), masks


def _hgrn_kernel(q_ref, f_ref, i_ref, g_ref, lbl_ref, gain_ref, sums_ref, masks_ref, o_ref,
                 state_ref, a_ref, b_ref, v_ref, *, layer):
    c = HGRN_CHUNK

    @pl.when(pl.program_id(1) == 0)
    def _():
        state_ref[...] = jnp.zeros_like(state_ref)

    logits = lbl_ref[...]
    e = jnp.exp(logits - jnp.max(logits, axis=0, keepdims=True))
    lb = jnp.sum(e[:layer + 1], axis=0, keepdims=True) / jnp.sum(e, axis=0, keepdims=True)
    out_gain = gain_ref[...]

    def chunk(ci, carry):
        r0 = pl.multiple_of(ci * c, c)
        rows = pl.ds(r0, c)
        q = q_ref[rows, :]
        forget = lb + (1.0 - lb) * _sigmoid(f_ref[rows, :])
        glog = jnp.log(forget)
        kk = 1.0 - forget
        qq = q * _sigmoid(q)
        g_hi = glog.astype(_BF16)
        g_lo = (glog - g_hi.astype(_F32)).astype(_BF16)

        def exponent(level):
            m = sums_ref[level * c:(level + 1) * c, :]
            return _dot(m, g_hi) + _dot(m, g_lo)

        from_start = jnp.exp(exponent(0))
        a_ref[0] = (qq * from_start).astype(_BF16)
        chunk_decay = from_start[c - 1:c, :]
        b_ref[0] = (kk * jnp.exp(exponent(1))).astype(_BF16)
        for level in range(2, 8):
            fac = jnp.exp(exponent(level))
            a_ref[level - 1] = (qq * fac).astype(_BF16)
            b_ref[level - 1] = (kk * fac).astype(_BF16)
        a_ref[7] = qq.astype(_BF16)
        b_ref[7] = kk.astype(_BF16)
        v_ref[...] = i_ref[rows, :].astype(_BF16)

        for h in range(HGRN_HEADS):
            sl = slice(h * HEAD_DIM, (h + 1) * HEAD_DIM)
            scores = jnp.zeros((c, c), _F32)
            for level in range(1, 8):
                scores += masks_ref[level - 1] * _dot_nt(a_ref[level, :, sl], b_ref[level, :, sl])
            state = state_ref[h]
            vh = v_ref[:, sl]
            o = _dot(scores.astype(_BF16), vh) + _dot_nt(a_ref[0, :, sl], state.astype(_BF16))
            state_ref[h] = state * chunk_decay[:, sl] + _dot_tn(vh, b_ref[0, :, sl])
            o = o * _rms_scale(o) * out_gain
            gate = g_ref[rows, sl]
            o_ref[rows, sl] = (o * (gate * _sigmoid(gate))).astype(o_ref.dtype)
        return carry

    lax.fori_loop(0, q_ref.shape[0] // c, chunk, 0)


def _hgrn_scan(proj, lb_logits, out_gain, *, batch, layer):
    t = proj.shape[0]
    width = HGRN_HEADS * HEAD_DIM
    tc = HGRN_STEP_TOKENS
    steps = t // batch // tc
    sums, masks = _hgrn_tables()
    c = HGRN_CHUNK

    def col_spec(k):
        return pl.BlockSpec((tc, width), lambda b, s: (b * steps + s, k))

    return pl.pallas_call(
        functools.partial(_hgrn_kernel, layer=layer),
        grid=(batch, steps),
        in_specs=[
            col_spec(0), col_spec(1), col_spec(2), col_spec(3),
            pl.BlockSpec(lb_logits.shape, lambda b, s: (0, 0)),
            pl.BlockSpec((1, HEAD_DIM), lambda b, s: (0, 0)),
            pl.BlockSpec(sums.shape, lambda b, s: (0, 0)),
            pl.BlockSpec(masks.shape, lambda b, s: (0, 0, 0)),
        ],
        out_specs=pl.BlockSpec((tc, width), lambda b, s: (b * steps + s, 0)),
        out_shape=jax.ShapeDtypeStruct((t, width), _BF16),
        scratch_shapes=[
            pltpu.VMEM((HGRN_HEADS, HEAD_DIM, HEAD_DIM), _F32),
            pltpu.VMEM((8, c, width), _BF16),
            pltpu.VMEM((8, c, width), _BF16),
            pltpu.VMEM((c, width), _BF16),
        ],
        compiler_params=_params("parallel", "arbitrary"),
        name="hgrn_scan",
    )(proj, proj, proj, proj, lb_logits, out_gain.reshape(1, HEAD_DIM),
      jnp.asarray(sums, _BF16), jnp.asarray(masks, _F32))


def _matmul_residual_kernel(a_ref, w_ref, h_ref, o_ref):
    n_slabs, _, kw = a_ref.shape
    acc = h_ref[...]
    for s in range(n_slabs):
        acc += _dot(a_ref[s], w_ref[s * kw:(s + 1) * kw, :])
    o_ref[...] = acc


def _matmul_residual(a, w, h, *, tm=512):
    n_slabs, t, kw = a.shape
    d = w.shape[1]
    return pl.pallas_call(
        _matmul_residual_kernel,
        grid=(t // tm,),
        in_specs=[
            pl.BlockSpec((n_slabs, tm, kw), lambda i: (0, i, 0)),
            pl.BlockSpec((n_slabs * kw, d), lambda i: (0, 0)),
            pl.BlockSpec((tm, d), lambda i: (i, 0)),
        ],
        out_specs=pl.BlockSpec((tm, d), lambda i: (i, 0)),
        out_shape=jax.ShapeDtypeStruct((t, d), _F32),
        compiler_params=_params("parallel"),
        name="matmul_residual",
    )(a, w, h)


def _ffn_kernel(h_ref, gain_ref, wg_ref, wu_ref, wd_ref, fgain_ref, o_ref, u_ref, acc_ref, *,
                final_norm):
    j = pl.program_id(1)

    @pl.when(j == 0)
    def _():
        x = h_ref[...]
        u_ref[...] = (x * _rms_scale(x) * gain_ref[...]).astype(_BF16)
        acc_ref[...] = x

    u = u_ref[...]
    gate = _dot(u, wg_ref[...])
    up = _dot(u, wu_ref[...])
    act = (gate * _sigmoid(gate) * up).astype(_BF16)
    acc_ref[...] += _dot(act, wd_ref[...])

    @pl.when(j == pl.num_programs(1) - 1)
    def _():
        y = acc_ref[...]
        if final_norm:
            y = y * _rms_scale(y) * fgain_ref[...]
        o_ref[...] = y


def _ffn(h, gain, w_in, w_down, final_gain, *, final_norm, tm=1024, tf=256):
    t, d = h.shape
    d_ff = w_down.shape[0]
    nf = d_ff // tf
    return pl.pallas_call(
        functools.partial(_ffn_kernel, final_norm=final_norm),
        grid=(t // tm, nf),
        in_specs=[
            pl.BlockSpec((tm, d), lambda i, j: (i, 0)),
            pl.BlockSpec((1, d), lambda i, j: (0, 0)),
            pl.BlockSpec((d, tf), lambda i, j: (0, j)),
            pl.BlockSpec((d, tf), lambda i, j: (0, nf + j)),
            pl.BlockSpec((tf, d), lambda i, j: (j, 0)),
            pl.BlockSpec((1, d), lambda i, j: (0, 0)),
        ],
        out_specs=pl.BlockSpec((tm, d), lambda i, j: (i, 0)),
        out_shape=jax.ShapeDtypeStruct((t, d), _F32),
        scratch_shapes=[pltpu.VMEM((tm, d), _BF16), pltpu.VMEM((tm, d), _F32)],
        compiler_params=_params("parallel", "arbitrary"),
        name="swiglu_ffn",
    )(h, gain.reshape(1, d), w_in, w_in, w_down, final_gain.reshape(1, d))


def _attn_kernel(*refs):
    ins = refs[:15]
    out_ref, o_scr, l_scr = refs[15:]
    first_block = pl.program_id(1) == 0
    row = lax.broadcasted_iota(jnp.int32, (SPAN, SPAN), 0)
    col = lax.broadcasted_iota(jnp.int32, (SPAN, SPAN), 1)
    bias_cur = jnp.where(col <= row, 0.0, NEG_BIG).astype(_F32)
    bias_prev = jnp.where(col >= row, 0.0, NEG_BIG).astype(_F32)
    bias_halo = bias_prev + jnp.where(first_block, NEG_BIG, 0.0).astype(_F32)

    for g, dil in enumerate(DILATIONS):
        q_ref, k_ref, v_ref, kh_ref, vh_ref = ins[5 * g:5 * g + 5]

        def rows(start, dil=dil):
            if dil == 1:
                return pl.ds(start, SPAN)
            return pl.ds(start, SPAN, stride=dil)

        def unit(q_start, kp_ref, vp_ref, prev_start, prev_bias,
                 g=g, q_ref=q_ref, k_ref=k_ref, v_ref=v_ref, rows=rows):
            qb = q_ref[rows(q_start), :].astype(_BF16)
            kc = k_ref[rows(q_start), :].astype(_BF16)
            vc = v_ref[rows(q_start), :].astype(_BF16)
            kp = kp_ref[rows(prev_start), :].astype(_BF16)
            vp = vp_ref[rows(prev_start), :].astype(_BF16)
            s_prev = _dot_nt(qb, kp) + prev_bias
            s_cur = _dot_nt(qb, kc) + bias_cur
            m = jnp.maximum(jnp.max(s_prev, axis=-1, keepdims=True),
                            jnp.max(s_cur, axis=-1, keepdims=True))
            p_prev = jnp.exp(s_prev - m)
            p_cur = jnp.exp(s_cur - m)
            denom = (jnp.sum(p_prev, axis=-1, keepdims=True)
                     + jnp.sum(p_cur, axis=-1, keepdims=True))
            o = (_dot(p_prev.astype(_BF16), vp) + _dot(p_cur.astype(_BF16), vc)) / denom
            lse = m + jnp.log(denom)
            o_scr[g, rows(q_start), :] = o
            l_scr[g, rows(q_start), :] = jnp.broadcast_to(lse, (SPAN, HEAD_DIM))

        blocks = DILATIONS[-1] // dil
        block_rows = SPAN * dil

        def residue(r, carry, unit=unit, blocks=blocks, block_rows=block_rows,
                    k_ref=k_ref, v_ref=v_ref, kh_ref=kh_ref, vh_ref=vh_ref, dil=dil):
            unit(r, kh_ref, vh_ref, r, bias_halo)
            if dil == 1:
                def later(b, c2):
                    start = pl.multiple_of(b * block_rows, SPAN)
                    unit(start, k_ref, v_ref, start - block_rows, bias_prev)
                    return c2
                lax.fori_loop(1, blocks, later, 0)
            else:
                for b in range(1, blocks):
                    unit(r + b * block_rows, k_ref, v_ref, r + (b - 1) * block_rows, bias_prev)
            return carry

        if dil == 1:
            residue(0, 0)
        else:
            lax.fori_loop(0, dil, residue, 0)

    merge_rows = 256

    def merge(ci, carry):
        rr = pl.ds(pl.multiple_of(ci * merge_rows, merge_rows), merge_rows)
        l0, l1, l2 = l_scr[0, rr, :], l_scr[1, rr, :], l_scr[2, rr, :]
        m = jnp.maximum(jnp.maximum(l0, l1), l2)
        e0, e1, e2 = jnp.exp(l0 - m), jnp.exp(l1 - m), jnp.exp(l2 - m)
        inv = 1.0 / (e0 + e1 + e2)
        out_ref[0, rr, :] = (o_scr[0, rr, :] * (e0 * inv)).astype(out_ref.dtype)
        out_ref[1, rr, :] = (o_scr[1, rr, :] * (e1 * inv)).astype(out_ref.dtype)
        out_ref[2, rr, :] = (o_scr[2, rr, :] * (e2 * inv)).astype(out_ref.dtype)
        return carry

    lax.fori_loop(0, ATTN_BLOCK // merge_rows, merge, 0)


def _attention(qkv, *, batch):
    t = qkv.shape[0]
    tb = ATTN_BLOCK
    steps = t // batch // tb
    in_specs, operands = [], []
    for g, dil in enumerate(DILATIONS):
        halo = SPAN * dil
        ratio = tb // halo

        def cur(which, g=g):
            return pl.BlockSpec(
                (tb, HEAD_DIM),
                lambda b, i, j: (b * steps + i, which * ATTN_HEADS + g * HEADS_PER_GROUP + j))

        def prev(which, g=g, ratio=ratio, halo=halo):
            return pl.BlockSpec(
                (halo, HEAD_DIM),
                lambda b, i, j: (jnp.maximum((b * steps + i) * ratio - 1, 0),
                                 which * ATTN_HEADS + g * HEADS_PER_GROUP + j))

        in_specs += [cur(0), cur(1), cur(2), prev(1), prev(2)]
        operands += [qkv] * 5
    n_groups = len(DILATIONS)
    return pl.pallas_call(
        _attn_kernel,
        grid=(batch, steps, HEADS_PER_GROUP),
        in_specs=in_specs,
        out_specs=pl.BlockSpec((n_groups, tb, HEAD_DIM), lambda b, i, j: (0, b * steps + i, j)),
        out_shape=jax.ShapeDtypeStruct((n_groups, t, HEADS_PER_GROUP * HEAD_DIM), _BF16),
        scratch_shapes=[pltpu.VMEM((3, tb, HEAD_DIM), _F32), pltpu.VMEM((3, tb, HEAD_DIM), _F32)],
        compiler_params=_params("parallel", "arbitrary", "arbitrary"),
        name="dilated_attention",
    )(*operands)


def _rope_tables(seq_len):
    inv_freq = 1.0 / (ROPE_THETA ** (jnp.arange(0, HEAD_DIM, 2, dtype=_F32) / HEAD_DIM))
    ang = jnp.arange(seq_len, dtype=_F32)[:, None] * inv_freq[None, :]
    cos, sin = jnp.cos(ang), jnp.sin(ang)
    return jnp.concatenate([cos, cos], axis=-1), jnp.concatenate([-sin, sin], axis=-1)


def kernel(x, norm_mix, norm_ffn, hgrn_w_in, hgrn_lb_logits, hgrn_out_norm, hgrn_w_out,
           attn_w_qkv, attn_w_out, ffn_w_in, ffn_w_down, final_norm):
    batch, seq, d = x.shape
    cos2, sin2 = _rope_tables(seq)
    h = x.reshape(batch * seq, d)
    bf = lambda w: w.astype(_BF16)

    proj = _norm_matmul(h, norm_mix[0], bf(hgrn_w_in[0]), cos2, sin2)
    gated = _hgrn_scan(proj, hgrn_lb_logits, hgrn_out_norm[0], batch=batch, layer=0)
    h = _matmul_residual(gated[None], bf(hgrn_w_out[0]), h)
    h = _ffn(h, norm_ffn[0], bf(ffn_w_in[0]), bf(ffn_w_down[0]), final_norm, final_norm=False)

    qk_tiles = 2 * ATTN_WIDTH // 512
    qkv = _norm_matmul(h, norm_mix[1], bf(attn_w_qkv[0]), cos2, sin2, rope_tiles=qk_tiles,
                       scaled_tiles=qk_tiles // 2, scale=HEAD_DIM ** -0.5)
    attn = _attention(qkv, batch=batch)
    h = _matmul_residual(attn, bf(attn_w_out[0]), h)
    h = _ffn(h, norm_ffn[1], bf(ffn_w_in[0 + 1]), bf(ffn_w_down[1]), final_norm, final_norm=True)
    return h.reshape(batch, seq, d)
```

```python
import functools

import numpy as np
import jax
import jax.numpy as jnp
from jax import lax
from jax.experimental import pallas as pl
from jax.experimental.pallas import tpu as pltpu

D_MODEL = 1024
HEAD_DIM = 128
HGRN_HEADS = 8
HGRN_CHUNK = 64
HGRN_STEP_TOKENS = 256
ATTN_HEADS = 12
ATTN_WIDTH = ATTN_HEADS * HEAD_DIM
DILATIONS = (1, 4, 16)
SPAN = 128
HEADS_PER_GROUP = 4
ATTN_BLOCK = SPAN * DILATIONS[-1]
ROPE_THETA = 10000.0
NORM_EPS = 1e-6
NEG_BIG = -1e30
LOG2_E = float(np.log2(np.e))
VMEM_LIMIT_BYTES = 56 * 1024 * 1024

_F32 = jnp.float32
_BF16 = jnp.bfloat16


def _dot(a, b):
    return jnp.dot(a, b, preferred_element_type=_F32)


def _dot_nt(a, b):
    return lax.dot_general(a, b, (((1,), (1,)), ((), ())), preferred_element_type=_F32)


def _dot_tn(a, b):
    return lax.dot_general(a, b, (((0,), (0,)), ((), ())), preferred_element_type=_F32)


def _sigmoid(x):
    return 1.0 / (1.0 + jnp.exp(-x))


def _rms_scale(x):
    return lax.rsqrt(jnp.mean(x * x, axis=-1, keepdims=True) + NORM_EPS)


def _params(*sem):
    return pltpu.CompilerParams(dimension_semantics=sem, vmem_limit_bytes=VMEM_LIMIT_BYTES)


def _norm_matmul_kernel(h_ref, gain_ref, w_ref, cos_ref, sin_ref, o_ref, u_ref, *,
                        rope_tiles, scaled_tiles, scale):
    j = pl.program_id(1)

    @pl.when(j == 0)
    def _():
        x = h_ref[...]
        u_ref[...] = (x * _rms_scale(x) * gain_ref[...]).astype(_BF16)

    res = _dot(u_ref[...], w_ref[...])

    if rope_tiles == 0:
        o_ref[...] = res
    else:
        @pl.when(j >= rope_tiles)
        def _():
            o_ref[...] = res

        @pl.when(j < rope_tiles)
        def _():
            sc = jnp.where(j < scaled_tiles, scale, 1.0).astype(_F32)
            cos = cos_ref[...] * sc
            sin = sin_ref[...] * sc
            for hh in range(res.shape[1] // HEAD_DIM):
                xh = res[:, hh * HEAD_DIM:(hh + 1) * HEAD_DIM]
                o_ref[:, hh * HEAD_DIM:(hh + 1) * HEAD_DIM] = (
                    xh * cos + pltpu.roll(xh, HEAD_DIM // 2, 1) * sin)


def _norm_matmul(h, gain, w, cos2, sin2, *, tm=512, tn=512, rope_tiles=0, scaled_tiles=0,
                 scale=1.0):
    t, d = h.shape
    n = w.shape[1]
    s = cos2.shape[0]
    kern = functools.partial(_norm_matmul_kernel, rope_tiles=rope_tiles,
                             scaled_tiles=scaled_tiles, scale=scale)
    return pl.pallas_call(
        kern,
        grid=(t // tm, n // tn),
        in_specs=[
            pl.BlockSpec((tm, d), lambda i, j: (i, 0)),
            pl.BlockSpec((1, d), lambda i, j: (0, 0)),
            pl.BlockSpec((d, tn), lambda i, j: (0, j)),
            pl.BlockSpec((tm, HEAD_DIM), lambda i, j: (i % (s // tm), 0)),
            pl.BlockSpec((tm, HEAD_DIM), lambda i, j: (i % (s // tm), 0)),
        ],
        out_specs=pl.BlockSpec((tm, tn), lambda i, j: (i, j)),
        out_shape=jax.ShapeDtypeStruct((t, n), _F32),
        scratch_shapes=[pltpu.VMEM((tm, d), _BF16)],
        compiler_params=_params("parallel", "arbitrary"),
        name="norm_matmul",
    )(h, gain.reshape(1, d), w, cos2, sin2)


def _hgrn_tables():
    c = HGRN_CHUNK
    t = np.arange(c)
    col = t[None, :]
    row = t[:, None]
    sums = np.zeros((8, c, c), np.float32)
    sums[0] = col <= row
    sums[1] = col > row
    masks = np.zeros((7, c, c), np.float32)
    half = c // 2
    level = 0
    while half >= 1:
        block = t // (2 * half)
        mid = block * 2 * half + half
        is_query = t >= mid
        q_rows = (col >= mid[:, None]) & (col <= row) & is_query[:, None]
        k_rows = (col > row) & (col < mid[:, None]) & (~is_query)[:, None]
        sums[2 + level] = q_rows | k_rows
        masks[level] = ((block[:, None] == block[None, :]) & is_query[:, None]
                        & (~is_query)[None, :])
        half //= 2
        level += 1
    masks[6] = np.eye(c)
    assert level == 6 and np.array_equal(masks.sum(0), np.tril(np.ones((c, c))))
    sums = sums.reshape(8 * c, c)
    return np.concatenate([sums, sums], axis=1), masks


def _hgrn_kernel(q_ref, f_ref, i_ref, g_ref, lbl_ref, gain_ref, sums_ref, masks_ref, o_ref,
                 state_ref, a_ref, b_ref, v_ref, *, layer):
    c = HGRN_CHUNK

    @pl.when(pl.program_id(1) == 0)
    def _():
        state_ref[...] = jnp.zeros_like(state_ref)

    logits = lbl_ref[...]
    e = jnp.exp(logits - jnp.max(logits, axis=0, keepdims=True))
    lb = jnp.sum(e[:layer + 1], axis=0, keepdims=True) / jnp.sum(e, axis=0, keepdims=True)
    out_gain = gain_ref[...]

    def chunk(ci, carry):
        r0 = pl.multiple_of(ci * c, c)
        rows = pl.ds(r0, c)
        q = q_ref[rows, :]
        forget = lb + (1.0 - lb) * _sigmoid(f_ref[rows, :])
        glog = jnp.log(forget) * LOG2_E
        kk = 1.0 - forget
        qq = q * _sigmoid(q)
        g_hi = glog.astype(_BF16)
        g_lo = (glog - g_hi.astype(_F32)).astype(_BF16)
        factors = jnp.exp2(_dot(sums_ref[...], jnp.concatenate([g_hi, g_lo], axis=0)))

        from_start = factors[0:c]
        a_ref[0] = (qq * from_start).astype(_BF16)
        chunk_decay = from_start[c - 1:c, :]
        b_ref[0] = (kk * factors[c:2 * c]).astype(_BF16)
        for level in range(2, 8):
            fac = factors[level * c:(level + 1) * c]
            a_ref[level - 1] = (qq * fac).astype(_BF16)
            b_ref[level - 1] = (kk * fac).astype(_BF16)
        a_ref[7] = qq.astype(_BF16)
        b_ref[7] = kk.astype(_BF16)
        v_ref[...] = i_ref[rows, :].astype(_BF16)

        for h in range(HGRN_HEADS):
            sl = slice(h * HEAD_DIM, (h + 1) * HEAD_DIM)
            scores = jnp.zeros((c, c), _F32)
            for level in range(1, 8):
                scores += masks_ref[level - 1] * _dot_nt(a_ref[level, :, sl], b_ref[level, :, sl])
            state = state_ref[h]
            vh = v_ref[:, sl]
            o = _dot(scores.astype(_BF16), vh) + _dot_nt(a_ref[0, :, sl], state.astype(_BF16))
            state_ref[h] = state * chunk_decay[:, sl] + _dot_tn(vh, b_ref[0, :, sl])
            o = o * _rms_scale(o) * out_gain
            gate = g_ref[rows, sl]
            o_ref[rows, sl] = (o * (gate * _sigmoid(gate))).astype(o_ref.dtype)
        return carry

    lax.fori_loop(0, q_ref.shape[0] // c, chunk, 0)


def _hgrn_scan(proj, lb_logits, out_gain, *, batch, layer):
    t = proj.shape[0]
    width = HGRN_HEADS * HEAD_DIM
    tc = HGRN_STEP_TOKENS
    steps = t // batch // tc
    sums, masks = _hgrn_tables()
    c = HGRN_CHUNK

    def col_spec(k):
        return pl.BlockSpec((tc, width), lambda b, s: (b * steps + s, k))

    return pl.pallas_call(
        functools.partial(_hgrn_kernel, layer=layer),
        grid=(batch, steps),
        in_specs=[
            col_spec(0), col_spec(1), col_spec(2), col_spec(3),
            pl.BlockSpec(lb_logits.shape, lambda b, s: (0, 0)),
            pl.BlockSpec((1, HEAD_DIM), lambda b, s: (0, 0)),
            pl.BlockSpec(sums.shape, lambda b, s: (0, 0)),
            pl.BlockSpec(masks.shape, lambda b, s: (0, 0, 0)),
        ],
        out_specs=pl.BlockSpec((tc, width), lambda b, s: (b * steps + s, 0)),
        out_shape=jax.ShapeDtypeStruct((t, width), _BF16),
        scratch_shapes=[
            pltpu.VMEM((HGRN_HEADS, HEAD_DIM, HEAD_DIM), _F32),
            pltpu.VMEM((8, c, width), _BF16),
            pltpu.VMEM((8, c, width), _BF16),
            pltpu.VMEM((c, width), _BF16),
        ],
        compiler_params=_params("parallel", "arbitrary"),
        name="hgrn_scan",
    )(proj, proj, proj, proj, lb_logits, out_gain.reshape(1, HEAD_DIM),
      jnp.asarray(sums, _BF16), jnp.asarray(masks, _F32))


def _matmul_residual_kernel(a_ref, w_ref, h_ref, o_ref):
    n_slabs, _, kw = a_ref.shape
    acc = h_ref[...]
    for s in range(n_slabs):
        acc += _dot(a_ref[s], w_ref[s * kw:(s + 1) * kw, :])
    o_ref[...] = acc


def _matmul_residual(a, w, h, *, tm=512):
    n_slabs, t, kw = a.shape
    d = w.shape[1]
    return pl.pallas_call(
        _matmul_residual_kernel,
        grid=(t // tm,),
        in_specs=[
            pl.BlockSpec((n_slabs, tm, kw), lambda i: (0, i, 0)),
            pl.BlockSpec((n_slabs * kw, d), lambda i: (0, 0)),
            pl.BlockSpec((tm, d), lambda i: (i, 0)),
        ],
        out_specs=pl.BlockSpec((tm, d), lambda i: (i, 0)),
        out_shape=jax.ShapeDtypeStruct((t, d), _F32),
        compiler_params=_params("parallel"),
        name="matmul_residual",
    )(a, w, h)


def _ffn_kernel(h_ref, gain_ref, wg_ref, wu_ref, wd_ref, fgain_ref, o_ref, u_ref, acc_ref, *,
                final_norm):
    j = pl.program_id(1)

    @pl.when(j == 0)
    def _():
        x = h_ref[...]
        u_ref[...] = (x * _rms_scale(x) * gain_ref[...]).astype(_BF16)
        acc_ref[...] = x

    u = u_ref[...]
    gate = _dot(u, wg_ref[...])
    up = _dot(u, wu_ref[...])
    act = (gate * _sigmoid(gate) * up).astype(_BF16)
    acc_ref[...] += _dot(act, wd_ref[...])

    @pl.when(j == pl.num_programs(1) - 1)
    def _():
        y = acc_ref[...]
        if final_norm:
            y = y * _rms_scale(y) * fgain_ref[...]
        o_ref[...] = y


def _ffn(h, gain, w_in, w_down, final_gain, *, final_norm, tm=1024, tf=256):
    t, d = h.shape
    d_ff = w_down.shape[0]
    nf = d_ff // tf
    return pl.pallas_call(
        functools.partial(_ffn_kernel, final_norm=final_norm),
        grid=(t // tm, nf),
        in_specs=[
            pl.BlockSpec((tm, d), lambda i, j: (i, 0)),
            pl.BlockSpec((1, d), lambda i, j: (0, 0)),
            pl.BlockSpec((d, tf), lambda i, j: (0, j)),
            pl.BlockSpec((d, tf), lambda i, j: (0, nf + j)),
            pl.BlockSpec((tf, d), lambda i, j: (j, 0)),
            pl.BlockSpec((1, d), lambda i, j: (0, 0)),
        ],
        out_specs=pl.BlockSpec((tm, d), lambda i, j: (i, 0)),
        out_shape=jax.ShapeDtypeStruct((t, d), _F32),
        scratch_shapes=[pltpu.VMEM((tm, d), _BF16), pltpu.VMEM((tm, d), _F32)],
        compiler_params=_params("parallel", "arbitrary"),
        name="swiglu_ffn",
    )(h, gain.reshape(1, d), w_in, w_in, w_down, final_gain.reshape(1, d))


def _attn_kernel(*refs):
    ins = refs[:15]
    out_ref, o_scr, l_scr = refs[15:]
    first_block = pl.program_id(1) == 0
    row = lax.broadcasted_iota(jnp.int32, (SPAN, SPAN), 0)
    col = lax.broadcasted_iota(jnp.int32, (SPAN, SPAN), 1)
    bias_cur = jnp.where(col <= row, 0.0, NEG_BIG).astype(_F32)
    bias_prev = jnp.where(col >= row, 0.0, NEG_BIG).astype(_F32)
    bias_halo = bias_prev + jnp.where(first_block, NEG_BIG, 0.0).astype(_F32)
    ones = jnp.ones((SPAN, HEAD_DIM), _BF16)
    chain_len = 4

    for g, dil in enumerate(DILATIONS):
        q_ref, k_ref, v_ref, kh_ref, vh_ref = ins[5 * g:5 * g + 5]

        def load(ref, start, dil=dil):
            idx = pl.ds(start, SPAN) if dil == 1 else pl.ds(start, SPAN, stride=dil)
            return ref[idx, :].astype(_BF16)

        def load_kv(kref, vref, start, load=load):
            return load(kref, start), jnp.concatenate([load(vref, start), ones], axis=1)

        def chain(starts, prev, prev_bias, g=g, dil=dil, q_ref=q_ref, k_ref=k_ref, v_ref=v_ref,
                  load=load, load_kv=load_kv):
            for start in starts:
                kp, vp = prev
                kc, vc = cur = load_kv(k_ref, v_ref, start)
                s = _dot_nt(load(q_ref, start), jnp.concatenate([kp, kc], axis=0))
                s = s + jnp.concatenate([prev_bias, bias_cur], axis=1)
                m = jnp.max(jnp.maximum(s[:, :SPAN], s[:, SPAN:]), axis=-1, keepdims=True)
                p = jnp.exp2(s - m).astype(_BF16)
                r = _dot(p, jnp.concatenate([vp, vc], axis=0))
                denom = r[:, HEAD_DIM:]
                idx = pl.ds(start, SPAN) if dil == 1 else pl.ds(start, SPAN, stride=dil)
                o_scr[g, idx, :] = r[:, :HEAD_DIM] / denom
                l_scr[g, idx, :] = m + jnp.log2(denom)
                prev, prev_bias = cur, bias_prev

        block_rows = SPAN * dil
        if dil == DILATIONS[-1]:
            def body(i, carry, chain=chain, load_kv=load_kv, kh_ref=kh_ref, vh_ref=vh_ref):
                for u in range(chain_len):
                    r = i * chain_len + u
                    chain([r], load_kv(kh_ref, vh_ref, r), bias_halo)
                return carry
            lax.fori_loop(0, dil // chain_len, body, 0)
        elif dil > 1:
            assert DILATIONS[-1] // dil == chain_len

            def body(r, carry, chain=chain, load_kv=load_kv, kh_ref=kh_ref, vh_ref=vh_ref,
                     block_rows=block_rows):
                chain([r + b * block_rows for b in range(chain_len)],
                      load_kv(kh_ref, vh_ref, r), bias_halo)
                return carry
            lax.fori_loop(0, dil, body, 0)
        else:
            chain([b * block_rows for b in range(chain_len)], load_kv(kh_ref, vh_ref, 0), bias_halo)

            def body(i, carry, chain=chain, load_kv=load_kv, k_ref=k_ref, v_ref=v_ref,
                     block_rows=block_rows):
                base = pl.multiple_of(i * chain_len * block_rows, SPAN)
                chain([base + b * block_rows for b in range(chain_len)],
                      load_kv(k_ref, v_ref, base - block_rows), bias_prev)
                return carry
            lax.fori_loop(1, DILATIONS[-1] // chain_len, body, 0)

    merge_rows = 256

    def merge(ci, carry):
        rr = pl.ds(pl.multiple_of(ci * merge_rows, merge_rows), merge_rows)
        l0, l1, l2 = l_scr[0, rr, :], l_scr[1, rr, :], l_scr[2, rr, :]
        m = jnp.maximum(jnp.maximum(l0, l1), l2)
        e0, e1, e2 = jnp.exp2(l0 - m), jnp.exp2(l1 - m), jnp.exp2(l2 - m)
        inv = 1.0 / (e0 + e1 + e2)
        out_ref[0, rr, :] = (o_scr[0, rr, :] * (e0 * inv)).astype(out_ref.dtype)
        out_ref[1, rr, :] = (o_scr[1, rr, :] * (e1 * inv)).astype(out_ref.dtype)
        out_ref[2, rr, :] = (o_scr[2, rr, :] * (e2 * inv)).astype(out_ref.dtype)
        return carry

    lax.fori_loop(0, ATTN_BLOCK // merge_rows, merge, 0)


def _attention(qkv, *, batch):
    t = qkv.shape[0]
    tb = ATTN_BLOCK
    steps = t // batch // tb
    in_specs, operands = [], []
    for g, dil in enumerate(DILATIONS):
        halo = SPAN * dil
        ratio = tb // halo

        def cur(which, g=g):
            return pl.BlockSpec(
                (tb, HEAD_DIM),
                lambda b, i, j: (b * steps + i, which * ATTN_HEADS + g * HEADS_PER_GROUP + j))

        def prev(which, g=g, ratio=ratio, halo=halo):
            return pl.BlockSpec(
                (halo, HEAD_DIM),
                lambda b, i, j: (jnp.maximum((b * steps + i) * ratio - 1, 0),
                                 which * ATTN_HEADS + g * HEADS_PER_GROUP + j))

        in_specs += [cur(0), cur(1), cur(2), prev(1), prev(2)]
        operands += [qkv] * 5
    n_groups = len(DILATIONS)
    return pl.pallas_call(
        _attn_kernel,
        grid=(batch, steps, HEADS_PER_GROUP),
        in_specs=in_specs,
        out_specs=pl.BlockSpec((n_groups, tb, HEAD_DIM), lambda b, i, j: (0, b * steps + i, j)),
        out_shape=jax.ShapeDtypeStruct((n_groups, t, HEADS_PER_GROUP * HEAD_DIM), _BF16),
        scratch_shapes=[pltpu.VMEM((3, tb, HEAD_DIM), _F32), pltpu.VMEM((3, tb, HEAD_DIM), _F32)],
        compiler_params=_params("parallel", "arbitrary", "arbitrary"),
        name="dilated_attention",
    )(*operands)


def _rope_tables(seq_len):
    inv_freq = 1.0 / (ROPE_THETA ** (jnp.arange(0, HEAD_DIM, 2, dtype=_F32) / HEAD_DIM))
    ang = jnp.arange(seq_len, dtype=_F32)[:, None] * inv_freq[None, :]
    cos, sin = jnp.cos(ang), jnp.sin(ang)
    return jnp.concatenate([cos, cos], axis=-1), jnp.concatenate([-sin, sin], axis=-1)


def kernel(x, norm_mix, norm_ffn, hgrn_w_in, hgrn_lb_logits, hgrn_out_norm, hgrn_w_out,
           attn_w_qkv, attn_w_out, ffn_w_in, ffn_w_down, final_norm):
    batch, seq, d = x.shape
    cos2, sin2 = _rope_tables(seq)
    h = x.reshape(batch * seq, d)
    bf = lambda w: w.astype(_BF16)

    proj = _norm_matmul(h, norm_mix[0], bf(hgrn_w_in[0]), cos2, sin2, tm=1024, tn=1024)
    gated = _hgrn_scan(proj, hgrn_lb_logits, hgrn_out_norm[0], batch=batch, layer=0)
    h = _matmul_residual(gated[None], bf(hgrn_w_out[0]), h)
    h = _ffn(h, norm_ffn[0], bf(ffn_w_in[0]), bf(ffn_w_down[0]), final_norm, final_norm=False)

    qkv_tile = ATTN_WIDTH // 2
    qkv = _norm_matmul(h, norm_mix[1], bf(attn_w_qkv[0]), cos2, sin2, tm=1024, tn=qkv_tile,
                       rope_tiles=4, scaled_tiles=2, scale=HEAD_DIM ** -0.5 * LOG2_E)
    attn = _attention(qkv, batch=batch)
    h = _matmul_residual(attn, bf(attn_w_out[0]), h)
    h = _ffn(h, norm_ffn[1], bf(ffn_w_in[0 + 1]), bf(ffn_w_down[1]), final_norm, final_norm=True)
    return h.reshape(batch, seq, d)
```

```python
import functools

import numpy as np
import jax
import jax.numpy as jnp
from jax import lax
from jax.experimental import pallas as pl
from jax.experimental.pallas import tpu as pltpu

D_MODEL = 1024
HEAD_DIM = 128
HGRN_HEADS = 8
HGRN_CHUNK = 64
HGRN_STEP_TOKENS = 512
ATTN_HEADS = 12
ATTN_WIDTH = ATTN_HEADS * HEAD_DIM
DILATIONS = (1, 4, 16)
SPAN = 128
HEADS_PER_GROUP = 4
ATTN_BLOCK = SPAN * DILATIONS[-1]
ROW_TILE = 512
COL_TILE = 512
ROPE_THETA = 10000.0
NORM_EPS = 1e-6
NEG_BIG = -1e30
LOG2_E = float(np.log2(np.e))
VMEM_LIMIT_BYTES = 56 * 1024 * 1024

_F32 = jnp.float32
_BF16 = jnp.bfloat16


def _dot(a, b):
    return jnp.dot(a, b, preferred_element_type=_F32)


def _dot_nt(a, b):
    return lax.dot_general(a, b, (((1,), (1,)), ((), ())), preferred_element_type=_F32)


def _dot_tn(a, b):
    return lax.dot_general(a, b, (((0,), (0,)), ((), ())), preferred_element_type=_F32)


def _sigmoid(x):
    return 1.0 / (1.0 + jnp.exp(-x))


def _rms_scale(x):
    return lax.rsqrt(jnp.mean(x * x, axis=-1, keepdims=True) + NORM_EPS)


def _params(*sem):
    return pltpu.CompilerParams(dimension_semantics=sem, vmem_limit_bytes=VMEM_LIMIT_BYTES)


def _resident(shape):
    zeros = (0,) * len(shape)
    return pl.BlockSpec(shape, lambda *_: zeros, pipeline_mode=pl.Buffered(1))


def _qkv_kernel(h_ref, gain_ref, w_ref, cos_ref, sin_ref, o_ref, *, scale):
    x = h_ref[...]
    u = (x * _rms_scale(x) * gain_ref[...]).astype(_BF16)
    cos, sin = cos_ref[...], sin_ref[...]
    cos_q, sin_q = cos * scale, sin * scale
    for j in range(w_ref.shape[1] // COL_TILE):
        res = _dot(u, w_ref[:, j * COL_TILE:(j + 1) * COL_TILE])
        for hh in range(COL_TILE // HEAD_DIM):
            lo = j * COL_TILE + hh * HEAD_DIM
            xh = res[:, hh * HEAD_DIM:(hh + 1) * HEAD_DIM]
            if lo < ATTN_WIDTH:
                xh = xh * cos_q + pltpu.roll(xh, HEAD_DIM // 2, 1) * sin_q
            elif lo < 2 * ATTN_WIDTH:
                xh = xh * cos + pltpu.roll(xh, HEAD_DIM // 2, 1) * sin
            o_ref[:, lo:lo + HEAD_DIM] = xh


def _qkv_projection(h, gain, w, cos2, sin2, *, scale):
    t, d = h.shape
    n = w.shape[1]
    tm = ROW_TILE
    seq_tiles = cos2.shape[0] // tm
    return pl.pallas_call(
        functools.partial(_qkv_kernel, scale=scale),
        grid=(t // tm,),
        in_specs=[
            pl.BlockSpec((tm, d), lambda i: (i, 0)),
            _resident((1, d)),
            _resident((d, n)),
            pl.BlockSpec((tm, HEAD_DIM), lambda i: (i % seq_tiles, 0)),
            pl.BlockSpec((tm, HEAD_DIM), lambda i: (i % seq_tiles, 0)),
        ],
        out_specs=pl.BlockSpec((tm, n), lambda i: (i, 0)),
        out_shape=jax.ShapeDtypeStruct((t, n), _F32),
        compiler_params=_params("parallel"),
        name="qkv_projection",
    )(h, gain.reshape(1, d), w, cos2, sin2)


def _hgrn_tables():
    c = HGRN_CHUNK
    t = np.arange(c)
    col = t[None, :]
    row = t[:, None]
    sums = np.zeros((8, c, c), np.float32)
    sums[0] = col <= row
    sums[1] = col > row
    masks = np.zeros((7, c, c), np.float32)
    half = c // 2
    level = 0
    while half >= 1:
        block = t // (2 * half)
        mid = block * 2 * half + half
        is_query = t >= mid
        q_rows = (col >= mid[:, None]) & (col <= row) & is_query[:, None]
        k_rows = (col > row) & (col < mid[:, None]) & (~is_query)[:, None]
        sums[2 + level] = q_rows | k_rows
        masks[level] = ((block[:, None] == block[None, :]) & is_query[:, None]
                        & (~is_query)[None, :])
        half //= 2
        level += 1
    masks[6] = np.eye(c)
    assert level == 6 and np.array_equal(masks.sum(0), np.tril(np.ones((c, c))))
    sums = sums.reshape(8 * c, c)
    return np.concatenate([sums, sums], axis=1), masks


def _hgrn_kernel(h_ref, ngain_ref, w_ref, lbl_ref, gain_ref, sums_ref, masks_ref, o_ref,
                 state_ref, proj_ref, a_ref, b_ref, v_ref, *, layer):
    c = HGRN_CHUNK
    width = HGRN_HEADS * HEAD_DIM

    @pl.when(pl.program_id(1) == 0)
    def _():
        state_ref[...] = jnp.zeros_like(state_ref)

    x = h_ref[...]
    u = (x * _rms_scale(x) * ngain_ref[...]).astype(_BF16)
    for j in range(w_ref.shape[1] // COL_TILE):
        cols = slice(j * COL_TILE, (j + 1) * COL_TILE)
        proj_ref[:, cols] = _dot(u, w_ref[:, cols])

    logits = lbl_ref[...]
    e = jnp.exp(logits - jnp.max(logits, axis=0, keepdims=True))
    lb = jnp.sum(e[:layer + 1], axis=0, keepdims=True) / jnp.sum(e, axis=0, keepdims=True)
    out_gain = gain_ref[...]

    def chunk(ci, carry):
        r0 = pl.multiple_of(ci * c, c)
        rows = pl.ds(r0, c)
        q = proj_ref[rows, 0:width]
        forget = lb + (1.0 - lb) * _sigmoid(proj_ref[rows, width:2 * width])
        glog = jnp.log(forget) * LOG2_E
        kk = 1.0 - forget
        qq = q * _sigmoid(q)
        g_hi = glog.astype(_BF16)
        g_lo = (glog - g_hi.astype(_F32)).astype(_BF16)
        factors = jnp.exp2(_dot(sums_ref[...], jnp.concatenate([g_hi, g_lo], axis=0)))

        from_start = factors[0:c]
        a_ref[0] = (qq * from_start).astype(_BF16)
        chunk_decay = from_start[c - 1:c, :]
        b_ref[0] = (kk * factors[c:2 * c]).astype(_BF16)
        for level in range(2, 8):
            fac = factors[level * c:(level + 1) * c]
            a_ref[level - 1] = (qq * fac).astype(_BF16)
            b_ref[level - 1] = (kk * fac).astype(_BF16)
        a_ref[7] = qq.astype(_BF16)
        b_ref[7] = kk.astype(_BF16)
        v_ref[...] = proj_ref[rows, 2 * width:3 * width].astype(_BF16)

        for h in range(HGRN_HEADS):
            sl = slice(h * HEAD_DIM, (h + 1) * HEAD_DIM)
            scores = jnp.zeros((c, c), _F32)
            for level in range(1, 8):
                scores += masks_ref[level - 1] * _dot_nt(a_ref[level, :, sl], b_ref[level, :, sl])
            state = state_ref[h]
            vh = v_ref[:, sl]
            o = _dot(scores.astype(_BF16), vh) + _dot_nt(a_ref[0, :, sl], state.astype(_BF16))
            state_ref[h] = state * chunk_decay[:, sl] + _dot_tn(vh, b_ref[0, :, sl])
            o = o * _rms_scale(o) * out_gain
            gate = proj_ref[rows, 3 * width + h * HEAD_DIM:3 * width + (h + 1) * HEAD_DIM]
            o_ref[rows, sl] = (o * (gate * _sigmoid(gate))).astype(o_ref.dtype)
        return carry

    lax.fori_loop(0, h_ref.shape[0] // c, chunk, 0)


def _hgrn_mixer(h, norm_gain, w_in, lb_logits, out_gain, *, batch, layer):
    t, d = h.shape
    width = HGRN_HEADS * HEAD_DIM
    tc = HGRN_STEP_TOKENS
    steps = t // batch // tc
    sums, masks = _hgrn_tables()
    c = HGRN_CHUNK
    return pl.pallas_call(
        functools.partial(_hgrn_kernel, layer=layer),
        grid=(batch, steps),
        in_specs=[
            pl.BlockSpec((tc, d), lambda b, s: (b * steps + s, 0)),
            _resident((1, d)),
            _resident(w_in.shape),
            _resident(lb_logits.shape),
            _resident((1, HEAD_DIM)),
            _resident(sums.shape),
            _resident(masks.shape),
        ],
        out_specs=pl.BlockSpec((tc, width), lambda b, s: (b * steps + s, 0)),
        out_shape=jax.ShapeDtypeStruct((t, width), _BF16),
        scratch_shapes=[
            pltpu.VMEM((HGRN_HEADS, HEAD_DIM, HEAD_DIM), _F32),
            pltpu.VMEM((tc, 4 * width), _F32),
            pltpu.VMEM((8, c, width), _BF16),
            pltpu.VMEM((8, c, width), _BF16),
            pltpu.VMEM((c, width), _BF16),
        ],
        compiler_params=_params("parallel", "arbitrary"),
        name="hgrn_mixer",
    )(h, norm_gain.reshape(1, d), w_in, lb_logits, out_gain.reshape(1, HEAD_DIM),
      jnp.asarray(sums, _BF16), jnp.asarray(masks, _F32))


def _tail_kernel(a_ref, wo_ref, h_ref, gain_ref, wi_ref, wd_ref, fgain_ref, o_ref, u_ref, *,
                 final_norm, ff_tile):
    n_slabs, _, kw = a_ref.shape
    mixed = h_ref[...]
    for s in range(n_slabs):
        mixed += _dot(a_ref[s], wo_ref[s * kw:(s + 1) * kw, :])
    o_ref[...] = mixed
    u_ref[...] = (mixed * _rms_scale(mixed) * gain_ref[...]).astype(_BF16)
    d_ff = wd_ref.shape[0]
    for j in range(d_ff // ff_tile):
        u = u_ref[...]
        gate = _dot(u, wi_ref[:, j * ff_tile:(j + 1) * ff_tile])
        up = _dot(u, wi_ref[:, d_ff + j * ff_tile:d_ff + (j + 1) * ff_tile])
        act = (gate * _sigmoid(gate) * up).astype(_BF16)
        o_ref[...] += _dot(act, wd_ref[j * ff_tile:(j + 1) * ff_tile, :])
    if final_norm:
        y = o_ref[...]
        o_ref[...] = y * _rms_scale(y) * fgain_ref[...]


def _block_tail(a, w_out, h, gain, w_in, w_down, final_gain, *, final_norm, ff_tile=256):
    n_slabs, t, kw = a.shape
    d = h.shape[1]
    tm = ROW_TILE
    return pl.pallas_call(
        functools.partial(_tail_kernel, final_norm=final_norm, ff_tile=ff_tile),
        grid=(t // tm,),
        in_specs=[
            pl.BlockSpec((n_slabs, tm, kw), lambda i: (0, i, 0)),
            _resident(w_out.shape),
            pl.BlockSpec((tm, d), lambda i: (i, 0)),
            _resident((1, d)),
            _resident(w_in.shape),
            _resident(w_down.shape),
            _resident((1, d)),
        ],
        out_specs=pl.BlockSpec((tm, d), lambda i: (i, 0)),
        out_shape=jax.ShapeDtypeStruct((t, d), _F32),
        scratch_shapes=[pltpu.VMEM((tm, d), _BF16)],
        compiler_params=_params("parallel"),
        name="block_tail",
    )(a, w_out, h, gain.reshape(1, d), w_in, w_down, final_gain.reshape(1, d))


def _attn_kernel(*refs):
    ins = refs[:15]
    out_ref, o_scr, l_scr = refs[15:]
    first_block = pl.program_id(1) == 0
    row = lax.broadcasted_iota(jnp.int32, (SPAN, SPAN), 0)
    col = lax.broadcasted_iota(jnp.int32, (SPAN, SPAN), 1)
    bias_cur = jnp.where(col <= row, 0.0, NEG_BIG).astype(_F32)
    bias_prev = jnp.where(col >= row, 0.0, NEG_BIG).astype(_F32)
    bias_halo = bias_prev + jnp.where(first_block, NEG_BIG, 0.0).astype(_F32)
    ones = jnp.ones((SPAN, HEAD_DIM), _BF16)
    chain_len = 4

    for g, dil in enumerate(DILATIONS):
        q_ref, k_ref, v_ref, kh_ref, vh_ref = ins[5 * g:5 * g + 5]

        def load(ref, start, dil=dil):
            idx = pl.ds(start, SPAN) if dil == 1 else pl.ds(start, SPAN, stride=dil)
            return ref[idx, :].astype(_BF16)

        def load_kv(kref, vref, start, load=load):
            return load(kref, start), jnp.concatenate([load(vref, start), ones], axis=1)

        def chain(starts, prev, prev_bias, g=g, dil=dil, q_ref=q_ref, k_ref=k_ref, v_ref=v_ref,
                  load=load, load_kv=load_kv):
            for start in starts:
                kp, vp = prev
                kc, vc = cur = load_kv(k_ref, v_ref, start)
                s = _dot_nt(load(q_ref, start), jnp.concatenate([kp, kc], axis=0))
                s = s + jnp.concatenate([prev_bias, bias_cur], axis=1)
                m = jnp.max(jnp.maximum(s[:, :SPAN], s[:, SPAN:]), axis=-1, keepdims=True)
                p = jnp.exp2(s - m).astype(_BF16)
                r = _dot(p, jnp.concatenate([vp, vc], axis=0))
                denom = r[:, HEAD_DIM:]
                idx = pl.ds(start, SPAN) if dil == 1 else pl.ds(start, SPAN, stride=dil)
                o_scr[g, idx, :] = r[:, :HEAD_DIM] / denom
                l_scr[g, idx, :] = m + jnp.log2(denom)
                prev, prev_bias = cur, bias_prev

        block_rows = SPAN * dil
        if dil == DILATIONS[-1]:
            def body(i, carry, chain=chain, load_kv=load_kv, kh_ref=kh_ref, vh_ref=vh_ref):
                for u in range(chain_len):
                    r = i * chain_len + u
                    chain([r], load_kv(kh_ref, vh_ref, r), bias_halo)
                return carry
            lax.fori_loop(0, dil // chain_len, body, 0)
        elif dil > 1:
            assert DILATIONS[-1] // dil == chain_len

            def body(r, carry, chain=chain, load_kv=load_kv, kh_ref=kh_ref, vh_ref=vh_ref,
                     block_rows=block_rows):
                chain([r + b * block_rows for b in range(chain_len)],
                      load_kv(kh_ref, vh_ref, r), bias_halo)
                return carry
            lax.fori_loop(0, dil, body, 0)
        else:
            chain([b * block_rows for b in range(chain_len)], load_kv(kh_ref, vh_ref, 0), bias_halo)

            def body(i, carry, chain=chain, load_kv=load_kv, k_ref=k_ref, v_ref=v_ref,
                     block_rows=block_rows):
                base = pl.multiple_of(i * chain_len * block_rows, SPAN)
                chain([base + b * block_rows for b in range(chain_len)],
                      load_kv(k_ref, v_ref, base - block_rows), bias_prev)
                return carry
            lax.fori_loop(1, DILATIONS[-1] // chain_len, body, 0)

    merge_rows = 256

    def merge(ci, carry):
        rr = pl.ds(pl.multiple_of(ci * merge_rows, merge_rows), merge_rows)
        l0, l1, l2 = l_scr[0, rr, :], l_scr[1, rr, :], l_scr[2, rr, :]
        m = jnp.maximum(jnp.maximum(l0, l1), l2)
        e0, e1, e2 = jnp.exp2(l0 - m), jnp.exp2(l1 - m), jnp.exp2(l2 - m)
        inv = 1.0 / (e0 + e1 + e2)
        out_ref[0, rr, :] = (o_scr[0, rr, :] * (e0 * inv)).astype(out_ref.dtype)
        out_ref[1, rr, :] = (o_scr[1, rr, :] * (e1 * inv)).astype(out_ref.dtype)
        out_ref[2, rr, :] = (o_scr[2, rr, :] * (e2 * inv)).astype(out_ref.dtype)
        return carry

    lax.fori_loop(0, ATTN_BLOCK // merge_rows, merge, 0)


def _attention(qkv, *, batch):
    t = qkv.shape[0]
    tb = ATTN_BLOCK
    steps = t // batch // tb
    in_specs, operands = [], []
    for g, dil in enumerate(DILATIONS):
        halo = SPAN * dil
        ratio = tb // halo

        def cur(which, g=g):
            return pl.BlockSpec(
                (tb, HEAD_DIM),
                lambda b, i, j: (b * steps + i, which * ATTN_HEADS + g * HEADS_PER_GROUP + j))

        def prev(which, g=g, ratio=ratio, halo=halo):
            return pl.BlockSpec(
                (halo, HEAD_DIM),
                lambda b, i, j: (jnp.maximum((b * steps + i) * ratio - 1, 0),
                                 which * ATTN_HEADS + g * HEADS_PER_GROUP + j))

        in_specs += [cur(0), cur(1), cur(2), prev(1), prev(2)]
        operands += [qkv] * 5
    n_groups = len(DILATIONS)
    return pl.pallas_call(
        _attn_kernel,
        grid=(batch, steps, HEADS_PER_GROUP),
        in_specs=in_specs,
        out_specs=pl.BlockSpec((n_groups, tb, HEAD_DIM), lambda b, i, j: (0, b * steps + i, j)),
        out_shape=jax.ShapeDtypeStruct((n_groups, t, HEADS_PER_GROUP * HEAD_DIM), _BF16),
        scratch_shapes=[pltpu.VMEM((3, tb, HEAD_DIM), _F32), pltpu.VMEM((3, tb, HEAD_DIM), _F32)],
        compiler_params=_params("parallel", "arbitrary", "arbitrary"),
        name="dilated_attention",
    )(*operands)


def _rope_tables(seq_len):
    inv_freq = 1.0 / (ROPE_THETA ** (jnp.arange(0, HEAD_DIM, 2, dtype=_F32) / HEAD_DIM))
    ang = jnp.arange(seq_len, dtype=_F32)[:, None] * inv_freq[None, :]
    cos, sin = jnp.cos(ang), jnp.sin(ang)
    return jnp.concatenate([cos, cos], axis=-1), jnp.concatenate([-sin, sin], axis=-1)


def kernel(x, norm_mix, norm_ffn, hgrn_w_in, hgrn_lb_logits, hgrn_out_norm, hgrn_w_out,
           attn_w_qkv, attn_w_out, ffn_w_in, ffn_w_down, final_norm):
    batch, seq, d = x.shape
    cos2, sin2 = _rope_tables(seq)
    h = x.reshape(batch * seq, d)
    bf = lambda w: w.astype(_BF16)

    gated = _hgrn_mixer(h, norm_mix[0], bf(hgrn_w_in[0]), hgrn_lb_logits, hgrn_out_norm[0],
                        batch=batch, layer=0)
    h = _block_tail(gated[None], bf(hgrn_w_out[0]), h, norm_ffn[0], bf(ffn_w_in[0]),
                    bf(ffn_w_down[0]), final_norm, final_norm=False)

    qkv = _qkv_projection(h, norm_mix[1], bf(attn_w_qkv[0]), cos2, sin2,
                          scale=HEAD_DIM ** -0.5 * LOG2_E)
    attn = _attention(qkv, batch=batch)
    h = _block_tail(attn, bf(attn_w_out[0]), h, norm_ffn[1], bf(ffn_w_in[1]),
                    bf(ffn_w_down[1]), final_norm, final_norm=True)
    return h.reshape(batch, seq, d)
```

```python
import functools

import numpy as np
import jax
import jax.numpy as jnp
from jax import lax
from jax.experimental import pallas as pl
from jax.experimental.pallas import tpu as pltpu

D_MODEL = 1024
HEAD_DIM = 128
HGRN_HEADS = 8
HGRN_CHUNK = 64
HGRN_STEP_TOKENS = 512
ATTN_HEADS = 12
ATTN_WIDTH = ATTN_HEADS * HEAD_DIM
DILATIONS = (1, 4, 16)
SPAN = 128
HEADS_PER_GROUP = 4
ATTN_BLOCK = SPAN * DILATIONS[-1]
ROW_TILE = 512
COL_TILE = 512
ROPE_THETA = 10000.0
NORM_EPS = 1e-6
NEG_BIG = -1e30
LOG2_E = float(np.log2(np.e))
VMEM_LIMIT_BYTES = 56 * 1024 * 1024

_F32 = jnp.float32
_BF16 = jnp.bfloat16


def _dot(a, b):
    return jnp.dot(a, b, preferred_element_type=_F32)


def _dot_nt(a, b):
    return lax.dot_general(a, b, (((1,), (1,)), ((), ())), preferred_element_type=_F32)


def _dot_tn(a, b):
    return lax.dot_general(a, b, (((0,), (0,)), ((), ())), preferred_element_type=_F32)


def _sigmoid(x):
    return 1.0 / (1.0 + jnp.exp(-x))


def _rms_scale(x):
    return lax.rsqrt(jnp.mean(x * x, axis=-1, keepdims=True) + NORM_EPS)


def _params(*sem):
    return pltpu.CompilerParams(dimension_semantics=sem, vmem_limit_bytes=VMEM_LIMIT_BYTES)


def _resident(shape, layer=None):
    if layer is None:
        index = (0,) * len(shape)
    else:
        index = (layer,) + (0,) * (len(shape) - 1)
        shape = (None,) + tuple(shape[1:])
    return pl.BlockSpec(shape, lambda *_: index, pipeline_mode=pl.Buffered(1))


def _qkv_kernel(h_ref, gain_ref, w_ref, cos_ref, sin_ref, o_ref, *, scale):
    x = h_ref[...]
    u = (x * _rms_scale(x) * gain_ref[...]).astype(_BF16)
    cos, sin = cos_ref[...], sin_ref[...]
    cos_q, sin_q = cos * scale, sin * scale
    for j in range(w_ref.shape[1] // COL_TILE):
        res = _dot(u, w_ref[:, j * COL_TILE:(j + 1) * COL_TILE])
        for hh in range(COL_TILE // HEAD_DIM):
            lo = j * COL_TILE + hh * HEAD_DIM
            xh = res[:, hh * HEAD_DIM:(hh + 1) * HEAD_DIM]
            if lo < ATTN_WIDTH:
                xh = xh * cos_q + pltpu.roll(xh, HEAD_DIM // 2, 1) * sin_q
            elif lo < 2 * ATTN_WIDTH:
                xh = xh * cos + pltpu.roll(xh, HEAD_DIM // 2, 1) * sin
            o_ref[:, lo:lo + HEAD_DIM] = xh


def _qkv_projection(h, gain, w, cos2, sin2, *, scale):
    t, d = h.shape
    n = w.shape[1]
    tm = ROW_TILE
    seq_tiles = cos2.shape[0] // tm
    return pl.pallas_call(
        functools.partial(_qkv_kernel, scale=scale),
        grid=(t // tm,),
        in_specs=[
            pl.BlockSpec((tm, d), lambda i: (i, 0)),
            _resident((1, d)),
            _resident((d, n)),
            pl.BlockSpec((tm, HEAD_DIM), lambda i: (i % seq_tiles, 0)),
            pl.BlockSpec((tm, HEAD_DIM), lambda i: (i % seq_tiles, 0)),
        ],
        out_specs=pl.BlockSpec((tm, n), lambda i: (i, 0)),
        out_shape=jax.ShapeDtypeStruct((t, n), _F32),
        compiler_params=_params("parallel"),
        name="qkv_projection",
    )(h, gain.reshape(1, d), w, cos2, sin2)


def _hgrn_tables():
    c = HGRN_CHUNK
    t = np.arange(c)
    col = t[None, :]
    row = t[:, None]
    sums = np.zeros((8, c, c), np.float32)
    sums[0] = col <= row
    sums[1] = col > row
    masks = np.zeros((7, c, c), np.float32)
    half = c // 2
    level = 0
    while half >= 1:
        block = t // (2 * half)
        mid = block * 2 * half + half
        is_query = t >= mid
        q_rows = (col >= mid[:, None]) & (col <= row) & is_query[:, None]
        k_rows = (col > row) & (col < mid[:, None]) & (~is_query)[:, None]
        sums[2 + level] = q_rows | k_rows
        masks[level] = ((block[:, None] == block[None, :]) & is_query[:, None]
                        & (~is_query)[None, :])
        half //= 2
        level += 1
    masks[6] = np.eye(c)
    assert level == 6 and np.array_equal(masks.sum(0), np.tril(np.ones((c, c))))
    sums = sums.reshape(8 * c, c)
    return np.concatenate([sums, sums], axis=1), masks


def _hgrn_kernel(h_ref, ngain_ref, w_ref, lbl_ref, gain_ref, sums_ref, masks_ref, o_ref,
                 state_ref, proj_ref, a_ref, b_ref, v_ref, *, layer):
    c = HGRN_CHUNK
    width = HGRN_HEADS * HEAD_DIM

    @pl.when(pl.program_id(1) == 0)
    def _():
        state_ref[...] = jnp.zeros_like(state_ref)

    x = h_ref[...]
    u = (x * _rms_scale(x) * ngain_ref[...]).astype(_BF16)
    for j in range(w_ref.shape[1] // COL_TILE):
        cols = slice(j * COL_TILE, (j + 1) * COL_TILE)
        proj_ref[:, cols] = _dot(u, w_ref[:, cols])

    logits = lbl_ref[...]
    e = jnp.exp(logits - jnp.max(logits, axis=0, keepdims=True))
    lb = jnp.sum(e[:layer + 1], axis=0, keepdims=True) / jnp.sum(e, axis=0, keepdims=True)
    out_gain = gain_ref[...]

    def chunk(ci, carry):
        r0 = pl.multiple_of(ci * c, c)
        rows = pl.ds(r0, c)
        q = proj_ref[rows, 0:width]
        forget = lb + (1.0 - lb) * _sigmoid(proj_ref[rows, width:2 * width])
        glog = jnp.log(forget) * LOG2_E
        kk = 1.0 - forget
        qq = q * _sigmoid(q)
        g_hi = glog.astype(_BF16)
        g_lo = (glog - g_hi.astype(_F32)).astype(_BF16)
        factors = jnp.exp2(_dot(sums_ref[...], jnp.concatenate([g_hi, g_lo], axis=0)))

        from_start = factors[0:c]
        a_ref[0] = (qq * from_start).astype(_BF16)
        chunk_decay = from_start[c - 1:c, :]
        b_ref[0] = (kk * factors[c:2 * c]).astype(_BF16)
        for level in range(2, 8):
            fac = factors[level * c:(level + 1) * c]
            a_ref[level - 1] = (qq * fac).astype(_BF16)
            b_ref[level - 1] = (kk * fac).astype(_BF16)
        a_ref[7] = qq.astype(_BF16)
        b_ref[7] = kk.astype(_BF16)
        v_ref[...] = proj_ref[rows, 2 * width:3 * width].astype(_BF16)

        for h in range(HGRN_HEADS):
            sl = slice(h * HEAD_DIM, (h + 1) * HEAD_DIM)
            scores = jnp.zeros((c, c), _F32)
            for level in range(1, 8):
                scores += masks_ref[level - 1] * _dot_nt(a_ref[level, :, sl], b_ref[level, :, sl])
            state = state_ref[h]
            vh = v_ref[:, sl]
            o = _dot(scores.astype(_BF16), vh) + _dot_nt(a_ref[0, :, sl], state.astype(_BF16))
            state_ref[h] = state * chunk_decay[:, sl] + _dot_tn(vh, b_ref[0, :, sl])
            o = o * _rms_scale(o) * out_gain
            gate = proj_ref[rows, 3 * width + h * HEAD_DIM:3 * width + (h + 1) * HEAD_DIM]
            o_ref[rows, sl] = (o * (gate * _sigmoid(gate))).astype(o_ref.dtype)
        return carry

    lax.fori_loop(0, h_ref.shape[0] // c, chunk, 0)


def _hgrn_mixer(h, norm_gain, w_in, lb_logits, out_gain, *, batch, layer):
    t, d = h.shape
    width = HGRN_HEADS * HEAD_DIM
    tc = HGRN_STEP_TOKENS
    steps = t // batch // tc
    sums, masks = _hgrn_tables()
    c = HGRN_CHUNK
    return pl.pallas_call(
        functools.partial(_hgrn_kernel, layer=layer),
        grid=(batch, steps),
        in_specs=[
            pl.BlockSpec((tc, d), lambda b, s: (b * steps + s, 0)),
            _resident((1, d)),
            _resident(w_in.shape),
            _resident(lb_logits.shape),
            _resident((1, HEAD_DIM)),
            _resident(sums.shape),
            _resident(masks.shape),
        ],
        out_specs=pl.BlockSpec((tc, width), lambda b, s: (b * steps + s, 0)),
        out_shape=jax.ShapeDtypeStruct((t, width), _BF16),
        scratch_shapes=[
            pltpu.VMEM((HGRN_HEADS, HEAD_DIM, HEAD_DIM), _F32),
            pltpu.VMEM((tc, 4 * width), _F32),
            pltpu.VMEM((8, c, width), _BF16),
            pltpu.VMEM((8, c, width), _BF16),
            pltpu.VMEM((c, width), _BF16),
        ],
        compiler_params=_params("parallel", "arbitrary"),
        name="hgrn_mixer",
    )(h, norm_gain.reshape(1, d), w_in, lb_logits, out_gain.reshape(1, HEAD_DIM),
      jnp.asarray(sums, _BF16), jnp.asarray(masks, _F32))


def _tail_kernel(a_ref, wo_ref, h_ref, gain_ref, wi_ref, wd_ref, fgain_ref, o_ref, u_ref, *,
                 final_norm, ff_tile):
    n_slabs, _, kw = a_ref.shape
    mixed = h_ref[...]
    for s in range(n_slabs):
        mixed += _dot(a_ref[s], wo_ref[s * kw:(s + 1) * kw, :])
    o_ref[...] = mixed
    u_ref[...] = (mixed * _rms_scale(mixed) * gain_ref[...]).astype(_BF16)
    d_ff = wd_ref.shape[0]
    for j in range(d_ff // ff_tile):
        u = u_ref[...]
        gate = _dot(u, wi_ref[:, j * ff_tile:(j + 1) * ff_tile])
        up = _dot(u, wi_ref[:, d_ff + j * ff_tile:d_ff + (j + 1) * ff_tile])
        act = (gate * _sigmoid(gate) * up).astype(_BF16)
        o_ref[...] += _dot(act, wd_ref[j * ff_tile:(j + 1) * ff_tile, :])
    if final_norm:
        y = o_ref[...]
        o_ref[...] = y * _rms_scale(y) * fgain_ref[...]


def _block_tail(a, w_out, h, gain, w_in, w_down, final_gain, *, layer, final_norm, ff_tile=256):
    n_slabs, t, kw = a.shape
    d = h.shape[1]
    tm = ROW_TILE
    return pl.pallas_call(
        functools.partial(_tail_kernel, final_norm=final_norm, ff_tile=ff_tile),
        grid=(t // tm,),
        in_specs=[
            pl.BlockSpec((n_slabs, tm, kw), lambda i: (0, i, 0)),
            _resident(w_out.shape),
            pl.BlockSpec((tm, d), lambda i: (i, 0)),
            _resident((1, d)),
            _resident(w_in.shape, layer=layer),
            _resident(w_down.shape, layer=layer),
            _resident((1, d)),
        ],
        out_specs=pl.BlockSpec((tm, d), lambda i: (i, 0)),
        out_shape=jax.ShapeDtypeStruct((t, d), _F32),
        scratch_shapes=[pltpu.VMEM((tm, d), _BF16)],
        compiler_params=_params("parallel"),
        name="block_tail",
    )(a, w_out, h, gain.reshape(1, d), w_in, w_down, final_gain.reshape(1, d))


def _attn_kernel(*refs):
    ins = refs[:15]
    out_ref, o_scr, l_scr = refs[15:]
    first_block = pl.program_id(1) == 0
    row = lax.broadcasted_iota(jnp.int32, (SPAN, SPAN), 0)
    col = lax.broadcasted_iota(jnp.int32, (SPAN, SPAN), 1)
    bias_cur = jnp.where(col <= row, 0.0, NEG_BIG).astype(_F32)
    bias_prev = jnp.where(col >= row, 0.0, NEG_BIG).astype(_F32)
    bias_halo = bias_prev + jnp.where(first_block, NEG_BIG, 0.0).astype(_F32)
    ones = jnp.ones((SPAN, HEAD_DIM), _BF16)
    chain_len = 4

    for g, dil in enumerate(DILATIONS):
        q_ref, k_ref, v_ref, kh_ref, vh_ref = ins[5 * g:5 * g + 5]

        def load(ref, start, dil=dil):
            idx = pl.ds(start, SPAN) if dil == 1 else pl.ds(start, SPAN, stride=dil)
            return ref[idx, :].astype(_BF16)

        def load_kv(kref, vref, start, load=load):
            return load(kref, start), jnp.concatenate([load(vref, start), ones], axis=1)

        def chain(starts, prev, prev_bias, g=g, dil=dil, q_ref=q_ref, k_ref=k_ref, v_ref=v_ref,
                  load=load, load_kv=load_kv):
            for start in starts:
                kp, vp = prev
                kc, vc = cur = load_kv(k_ref, v_ref, start)
                s = _dot_nt(load(q_ref, start), jnp.concatenate([kp, kc], axis=0))
                s = s + jnp.concatenate([prev_bias, bias_cur], axis=1)
                m = jnp.max(jnp.maximum(s[:, :SPAN], s[:, SPAN:]), axis=-1, keepdims=True)
                p = jnp.exp2(s - m).astype(_BF16)
                r = _dot(p, jnp.concatenate([vp, vc], axis=0))
                denom = r[:, HEAD_DIM:]
                idx = pl.ds(start, SPAN) if dil == 1 else pl.ds(start, SPAN, stride=dil)
                o_scr[g, idx, :] = r[:, :HEAD_DIM] / denom
                l_scr[g, idx, :] = m + jnp.log2(denom)
                prev, prev_bias = cur, bias_prev

        block_rows = SPAN * dil
        if dil == DILATIONS[-1]:
            def body(i, carry, chain=chain, load_kv=load_kv, kh_ref=kh_ref, vh_ref=vh_ref):
                for u in range(chain_len):
                    r = i * chain_len + u
                    chain([r], load_kv(kh_ref, vh_ref, r), bias_halo)
                return carry
            lax.fori_loop(0, dil // chain_len, body, 0)
        elif dil > 1:
            assert DILATIONS[-1] // dil == chain_len

            def body(r, carry, chain=chain, load_kv=load_kv, kh_ref=kh_ref, vh_ref=vh_ref,
                     block_rows=block_rows):
                chain([r + b * block_rows for b in range(chain_len)],
                      load_kv(kh_ref, vh_ref, r), bias_halo)
                return carry
            lax.fori_loop(0, dil, body, 0)
        else:
            chain([b * block_rows for b in range(chain_len)], load_kv(kh_ref, vh_ref, 0), bias_halo)

            def body(i, carry, chain=chain, load_kv=load_kv, k_ref=k_ref, v_ref=v_ref,
                     block_rows=block_rows):
                base = pl.multiple_of(i * chain_len * block_rows, SPAN)
                chain([base + b * block_rows for b in range(chain_len)],
                      load_kv(k_ref, v_ref, base - block_rows), bias_prev)
                return carry
            lax.fori_loop(1, DILATIONS[-1] // chain_len, body, 0)

    merge_rows = 256

    def merge(ci, carry):
        rr = pl.ds(pl.multiple_of(ci * merge_rows, merge_rows), merge_rows)
        l0, l1, l2 = l_scr[0, rr, :], l_scr[1, rr, :], l_scr[2, rr, :]
        m = jnp.maximum(jnp.maximum(l0, l1), l2)
        e0, e1, e2 = jnp.exp2(l0 - m), jnp.exp2(l1 - m), jnp.exp2(l2 - m)
        inv = 1.0 / (e0 + e1 + e2)
        out_ref[0, rr, :] = (o_scr[0, rr, :] * (e0 * inv)).astype(out_ref.dtype)
        out_ref[1, rr, :] = (o_scr[1, rr, :] * (e1 * inv)).astype(out_ref.dtype)
        out_ref[2, rr, :] = (o_scr[2, rr, :] * (e2 * inv)).astype(out_ref.dtype)
        return carry

    lax.fori_loop(0, ATTN_BLOCK // merge_rows, merge, 0)


def _attention(qkv, *, batch):
    t = qkv.shape[0]
    tb = ATTN_BLOCK
    steps = t // batch // tb
    in_specs, operands = [], []
    for g, dil in enumerate(DILATIONS):
        halo = SPAN * dil
        ratio = tb // halo

        def cur(which, g=g):
            return pl.BlockSpec(
                (tb, HEAD_DIM),
                lambda b, i, j: (b * steps + i, which * ATTN_HEADS + g * HEADS_PER_GROUP + j))

        def prev(which, g=g, ratio=ratio, halo=halo):
            return pl.BlockSpec(
                (halo, HEAD_DIM),
                lambda b, i, j: (jnp.maximum((b * steps + i) * ratio - 1, 0),
                                 which * ATTN_HEADS + g * HEADS_PER_GROUP + j))

        in_specs += [cur(0), cur(1), cur(2), prev(1), prev(2)]
        operands += [qkv] * 5
    n_groups = len(DILATIONS)
    return pl.pallas_call(
        _attn_kernel,
        grid=(batch, steps, HEADS_PER_GROUP),
        in_specs=in_specs,
        out_specs=pl.BlockSpec((n_groups, tb, HEAD_DIM), lambda b, i, j: (0, b * steps + i, j)),
        out_shape=jax.ShapeDtypeStruct((n_groups, t, HEADS_PER_GROUP * HEAD_DIM), _BF16),
        scratch_shapes=[pltpu.VMEM((3, tb, HEAD_DIM), _F32), pltpu.VMEM((3, tb, HEAD_DIM), _F32)],
        compiler_params=_params("parallel", "arbitrary", "arbitrary"),
        name="dilated_attention",
    )(*operands)


def _rope_tables(seq_len):
    inv_freq = 1.0 / (ROPE_THETA ** (np.arange(0, HEAD_DIM, 2, dtype=np.float64) / HEAD_DIM))
    ang = np.arange(seq_len, dtype=np.float64)[:, None] * inv_freq[None, :]
    cos, sin = np.cos(ang), np.sin(ang)
    return (jnp.asarray(np.concatenate([cos, cos], axis=-1), _F32),
            jnp.asarray(np.concatenate([-sin, sin], axis=-1), _F32))


def kernel(x, norm_mix, norm_ffn, hgrn_w_in, hgrn_lb_logits, hgrn_out_norm, hgrn_w_out,
           attn_w_qkv, attn_w_out, ffn_w_in, ffn_w_down, final_norm):
    batch, seq, d = x.shape
    cos2, sin2 = _rope_tables(seq)
    h = x.reshape(batch * seq, d)
    bf = lambda w: w.astype(_BF16)

    gated = _hgrn_mixer(h, norm_mix[0], bf(hgrn_w_in[0]), hgrn_lb_logits, hgrn_out_norm[0],
                        batch=batch, layer=0)
    ffn_in, ffn_down = bf(ffn_w_in), bf(ffn_w_down)
    h = _block_tail(gated[None], bf(hgrn_w_out[0]), h, norm_ffn[0], ffn_in, ffn_down, final_norm,
                    layer=0, final_norm=False)

    qkv = _qkv_projection(h, norm_mix[1], bf(attn_w_qkv[0]), cos2, sin2,
                          scale=HEAD_DIM ** -0.5 * LOG2_E)
    attn = _attention(qkv, batch=batch)
    h = _block_tail(attn, bf(attn_w_out[0]), h, norm_ffn[1], ffn_in, ffn_down, final_norm,
                    layer=1, final_norm=True)
    return h.reshape(batch, seq, d)
```

```python
import functools

import numpy as np
import jax
import jax.numpy as jnp
from jax import lax
from jax.experimental import pallas as pl
from jax.experimental.pallas import tpu as pltpu

D_MODEL = 1024
HEAD_DIM = 128
HGRN_HEADS = 8
HGRN_CHUNK = 64
HGRN_STEP_TOKENS = 512
ATTN_HEADS = 12
ATTN_WIDTH = ATTN_HEADS * HEAD_DIM
DILATIONS = (1, 4, 16)
SPAN = 128
HEADS_PER_GROUP = 4
ATTN_BLOCK = SPAN * DILATIONS[-1]
ROW_TILE = 512
COL_TILE = 512
ROPE_THETA = 10000.0
NORM_EPS = 1e-6
NEG_BIG = -1e30
LOG2_E = float(np.log2(np.e))
VMEM_LIMIT_BYTES = 56 * 1024 * 1024

_F32 = jnp.float32
_BF16 = jnp.bfloat16


def _dot(a, b):
    return jnp.dot(a, b, preferred_element_type=_F32)


def _dot_nt(a, b):
    return lax.dot_general(a, b, (((1,), (1,)), ((), ())), preferred_element_type=_F32)


def _dot_tn(a, b):
    return lax.dot_general(a, b, (((0,), (0,)), ((), ())), preferred_element_type=_F32)


def _sigmoid(x):
    return 1.0 / (1.0 + jnp.exp(-x))


def _rms_scale(x):
    return lax.rsqrt(jnp.mean(x * x, axis=-1, keepdims=True) + NORM_EPS)


def _params(*sem):
    return pltpu.CompilerParams(dimension_semantics=sem, vmem_limit_bytes=VMEM_LIMIT_BYTES)


def _resident(shape, layer=None):
    if layer is None:
        index = (0,) * len(shape)
    else:
        index = (layer,) + (0,) * (len(shape) - 1)
        shape = (None,) + tuple(shape[1:])
    return pl.BlockSpec(shape, lambda *_: index, pipeline_mode=pl.Buffered(1))


def _qkv_kernel(h_ref, gain_ref, w_ref, cos_ref, sin_ref, o_ref, *, scale):
    x = h_ref[...]
    u = (x * _rms_scale(x) * gain_ref[...]).astype(_BF16)
    cos, sin = cos_ref[...], sin_ref[...]
    cos_q, sin_q = cos * scale, sin * scale
    for j in range(w_ref.shape[1] // COL_TILE):
        res = _dot(u, w_ref[:, j * COL_TILE:(j + 1) * COL_TILE])
        for hh in range(COL_TILE // HEAD_DIM):
            lo = j * COL_TILE + hh * HEAD_DIM
            xh = res[:, hh * HEAD_DIM:(hh + 1) * HEAD_DIM]
            if lo < ATTN_WIDTH:
                xh = xh * cos_q + pltpu.roll(xh, HEAD_DIM // 2, 1) * sin_q
            elif lo < 2 * ATTN_WIDTH:
                xh = xh * cos + pltpu.roll(xh, HEAD_DIM // 2, 1) * sin
            o_ref[:, lo:lo + HEAD_DIM] = xh


def _qkv_projection(h, gain, w, cos2, sin2, *, scale):
    t, d = h.shape
    n = w.shape[1]
    tm = ROW_TILE
    seq_tiles = cos2.shape[0] // tm
    return pl.pallas_call(
        functools.partial(_qkv_kernel, scale=scale),
        grid=(t // tm,),
        in_specs=[
            pl.BlockSpec((tm, d), lambda i: (i, 0)),
            _resident((1, d)),
            _resident((d, n)),
            pl.BlockSpec((tm, HEAD_DIM), lambda i: (i % seq_tiles, 0)),
            pl.BlockSpec((tm, HEAD_DIM), lambda i: (i % seq_tiles, 0)),
        ],
        out_specs=pl.BlockSpec((tm, n), lambda i: (i, 0)),
        out_shape=jax.ShapeDtypeStruct((t, n), _F32),
        compiler_params=_params("parallel"),
        name="qkv_projection",
    )(h, gain.reshape(1, d), w, cos2, sin2)


def _hgrn_tables():
    c = HGRN_CHUNK
    t = np.arange(c)
    col = t[None, :]
    row = t[:, None]
    sums = np.zeros((8, c, c), np.float32)
    sums[0] = col <= row
    sums[1] = col > row
    masks = np.zeros((7, c, c), np.float32)
    half = c // 2
    level = 0
    while half >= 1:
        block = t // (2 * half)
        mid = block * 2 * half + half
        is_query = t >= mid
        q_rows = (col >= mid[:, None]) & (col <= row) & is_query[:, None]
        k_rows = (col > row) & (col < mid[:, None]) & (~is_query)[:, None]
        sums[2 + level] = q_rows | k_rows
        masks[level] = ((block[:, None] == block[None, :]) & is_query[:, None]
                        & (~is_query)[None, :])
        half //= 2
        level += 1
    masks[6] = np.eye(c)
    assert level == 6 and np.array_equal(masks.sum(0), np.tril(np.ones((c, c))))
    sums = sums.reshape(8 * c, c)
    return np.concatenate([sums, sums], axis=1), masks


def _hgrn_kernel(h_ref, ngain_ref, w_ref, lbl_ref, gain_ref, sums_ref, masks_ref, o_ref,
                 state_ref, proj_ref, a_ref, b_ref, v_ref, s_ref, *, layer):
    c = HGRN_CHUNK
    width = HGRN_HEADS * HEAD_DIM

    @pl.when(pl.program_id(1) == 0)
    def _():
        state_ref[...] = jnp.zeros_like(state_ref)

    x = h_ref[...]
    u = (x * _rms_scale(x) * ngain_ref[...]).astype(_BF16)
    for j in range(w_ref.shape[1] // COL_TILE):
        cols = slice(j * COL_TILE, (j + 1) * COL_TILE)
        proj_ref[:, cols] = _dot(u, w_ref[:, cols])

    logits = lbl_ref[...]
    e = jnp.exp(logits - jnp.max(logits, axis=0, keepdims=True))
    lb = jnp.sum(e[:layer + 1], axis=0, keepdims=True) / jnp.sum(e, axis=0, keepdims=True)
    out_gain = gain_ref[...]

    def chunk(ci, carry):
        r0 = pl.multiple_of(ci * c, c)
        rows = pl.ds(r0, c)
        q = proj_ref[rows, 0:width]
        forget = lb + (1.0 - lb) * _sigmoid(proj_ref[rows, width:2 * width])
        glog = jnp.log(forget) * LOG2_E
        kk = 1.0 - forget
        qq = q * _sigmoid(q)
        g_hi = glog.astype(_BF16)
        g_lo = (glog - g_hi.astype(_F32)).astype(_BF16)
        factors = jnp.exp2(_dot(sums_ref[...], jnp.concatenate([g_hi, g_lo], axis=0)))

        from_start = factors[0:c]
        a_ref[0] = (qq * from_start).astype(_BF16)
        chunk_decay = from_start[c - 1:c, :]
        b_ref[0] = (kk * factors[c:2 * c]).astype(_BF16)
        for level in range(2, 8):
            fac = factors[level * c:(level + 1) * c]
            a_ref[level - 1] = (qq * fac).astype(_BF16)
            b_ref[level - 1] = (kk * fac).astype(_BF16)
        a_ref[7] = qq.astype(_BF16)
        b_ref[7] = kk.astype(_BF16)
        v_ref[...] = proj_ref[rows, 2 * width:3 * width].astype(_BF16)

        def lanes(h):
            return slice(h * HEAD_DIM, (h + 1) * HEAD_DIM)

        pairs = [(h, h + 1) for h in range(0, HGRN_HEADS, 2)]
        for pair in pairs:
            scores = [jnp.zeros((c, c), _F32) for _ in pair]
            for level in range(1, 8):
                for i, h in enumerate(pair):
                    scores[i] += masks_ref[level - 1] * _dot_nt(a_ref[level, :, lanes(h)],
                                                                b_ref[level, :, lanes(h)])
            for i, h in enumerate(pair):
                s_ref[h] = scores[i].astype(_BF16)
        for pair in pairs:
            states = [state_ref[h] for h in pair]
            intra = [_dot(s_ref[h], v_ref[:, lanes(h)]) for h in pair]
            inter = [_dot_nt(a_ref[0, :, lanes(h)], states[i].astype(_BF16))
                     for i, h in enumerate(pair)]
            update = [_dot_tn(v_ref[:, lanes(h)], b_ref[0, :, lanes(h)]) for h in pair]
            for i, h in enumerate(pair):
                state_ref[h] = states[i] * chunk_decay[:, lanes(h)] + update[i]
                o = intra[i] + inter[i]
                o = o * _rms_scale(o) * out_gain
                gate = proj_ref[rows, 3 * width + h * HEAD_DIM:3 * width + (h + 1) * HEAD_DIM]
                o_ref[rows, lanes(h)] = (o * (gate * _sigmoid(gate))).astype(o_ref.dtype)
        return carry

    lax.fori_loop(0, h_ref.shape[0] // c, chunk, 0)


def _hgrn_mixer(h, norm_gain, w_in, lb_logits, out_gain, *, batch, layer):
    t, d = h.shape
    width = HGRN_HEADS * HEAD_DIM
    tc = HGRN_STEP_TOKENS
    steps = t // batch // tc
    sums, masks = _hgrn_tables()
    c = HGRN_CHUNK
    return pl.pallas_call(
        functools.partial(_hgrn_kernel, layer=layer),
        grid=(batch, steps),
        in_specs=[
            pl.BlockSpec((tc, d), lambda b, s: (b * steps + s, 0)),
            _resident((1, d)),
            _resident(w_in.shape),
            _resident(lb_logits.shape),
            _resident((1, HEAD_DIM)),
            _resident(sums.shape),
            _resident(masks.shape),
        ],
        out_specs=pl.BlockSpec((tc, width), lambda b, s: (b * steps + s, 0)),
        out_shape=jax.ShapeDtypeStruct((t, width), _BF16),
        scratch_shapes=[
            pltpu.VMEM((HGRN_HEADS, HEAD_DIM, HEAD_DIM), _F32),
            pltpu.VMEM((tc, 4 * width), _F32),
            pltpu.VMEM((8, c, width), _BF16),
            pltpu.VMEM((8, c, width), _BF16),
            pltpu.VMEM((c, width), _BF16),
            pltpu.VMEM((HGRN_HEADS, c, c), _BF16),
        ],
        compiler_params=_params("parallel", "arbitrary"),
        name="hgrn_mixer",
    )(h, norm_gain.reshape(1, d), w_in, lb_logits, out_gain.reshape(1, HEAD_DIM),
      jnp.asarray(sums, _BF16), jnp.asarray(masks, _F32))


def _tail_kernel(a_ref, wo_ref, h_ref, gain_ref, wi_ref, wd_ref, fgain_ref, o_ref, u_ref, *,
                 final_norm, ff_tile):
    n_slabs, _, kw = a_ref.shape
    mixed = h_ref[...]
    for s in range(n_slabs):
        mixed += _dot(a_ref[s], wo_ref[s * kw:(s + 1) * kw, :])
    o_ref[...] = mixed
    u_ref[...] = (mixed * _rms_scale(mixed) * gain_ref[...]).astype(_BF16)
    d_ff = wd_ref.shape[0]
    for j in range(d_ff // ff_tile):
        u = u_ref[...]
        gate = _dot(u, wi_ref[:, j * ff_tile:(j + 1) * ff_tile])
        up = _dot(u, wi_ref[:, d_ff + j * ff_tile:d_ff + (j + 1) * ff_tile])
        act = (gate * _sigmoid(gate) * up).astype(_BF16)
        o_ref[...] += _dot(act, wd_ref[j * ff_tile:(j + 1) * ff_tile, :])
    if final_norm:
        y = o_ref[...]
        o_ref[...] = y * _rms_scale(y) * fgain_ref[...]


def _block_tail(a, w_out, h, gain, w_in, w_down, final_gain, *, layer, final_norm, ff_tile=256):
    n_slabs, t, kw = a.shape
    d = h.shape[1]
    tm = ROW_TILE
    return pl.pallas_call(
        functools.partial(_tail_kernel, final_norm=final_norm, ff_tile=ff_tile),
        grid=(t // tm,),
        in_specs=[
            pl.BlockSpec((n_slabs, tm, kw), lambda i: (0, i, 0)),
            _resident(w_out.shape),
            pl.BlockSpec((tm, d), lambda i: (i, 0)),
            _resident((1, d)),
            _resident(w_in.shape, layer=layer),
            _resident(w_down.shape, layer=layer),
            _resident((1, d)),
        ],
        out_specs=pl.BlockSpec((tm, d), lambda i: (i, 0)),
        out_shape=jax.ShapeDtypeStruct((t, d), _F32),
        scratch_shapes=[pltpu.VMEM((tm, d), _BF16)],
        compiler_params=_params("parallel"),
        name="block_tail",
    )(a, w_out, h, gain.reshape(1, d), w_in, w_down, final_gain.reshape(1, d))


def _attn_kernel(*refs):
    ins = refs[:15]
    out_ref, o_scr, l_scr = refs[15:]
    first_block = pl.program_id(1) == 0
    row = lax.broadcasted_iota(jnp.int32, (SPAN, SPAN), 0)
    col = lax.broadcasted_iota(jnp.int32, (SPAN, SPAN), 1)
    bias_cur = jnp.where(col <= row, 0.0, NEG_BIG).astype(_F32)
    bias_prev = jnp.where(col >= row, 0.0, NEG_BIG).astype(_F32)
    bias_halo = bias_prev + jnp.where(first_block, NEG_BIG, 0.0).astype(_F32)
    ones = jnp.ones((SPAN, HEAD_DIM), _BF16)
    chain_len = 4

    for g, dil in enumerate(DILATIONS):
        q_ref, k_ref, v_ref, kh_ref, vh_ref = ins[5 * g:5 * g + 5]

        def load(ref, start, dil=dil):
            idx = pl.ds(start, SPAN) if dil == 1 else pl.ds(start, SPAN, stride=dil)
            return ref[idx, :].astype(_BF16)

        def load_kv(kref, vref, start, load=load):
            return load(kref, start), jnp.concatenate([load(vref, start), ones], axis=1)

        def chain(starts, prev, prev_bias, g=g, dil=dil, q_ref=q_ref, k_ref=k_ref, v_ref=v_ref,
                  load=load, load_kv=load_kv):
            for start in starts:
                kp, vp = prev
                kc, vc = cur = load_kv(k_ref, v_ref, start)
                s = _dot_nt(load(q_ref, start), jnp.concatenate([kp, kc], axis=0))
                s = s + jnp.concatenate([prev_bias, bias_cur], axis=1)
                m = jnp.max(jnp.maximum(s[:, :SPAN], s[:, SPAN:]), axis=-1, keepdims=True)
                p = jnp.exp2(s - m).astype(_BF16)
                r = _dot(p, jnp.concatenate([vp, vc], axis=0))
                denom = r[:, HEAD_DIM:]
                idx = pl.ds(start, SPAN) if dil == 1 else pl.ds(start, SPAN, stride=dil)
                o_scr[g, idx, :] = r[:, :HEAD_DIM] / denom
                l_scr[g, idx, :] = m + jnp.log2(denom)
                prev, prev_bias = cur, bias_prev

        block_rows = SPAN * dil
        if dil == DILATIONS[-1]:
            def body(i, carry, chain=chain, load_kv=load_kv, kh_ref=kh_ref, vh_ref=vh_ref):
                for u in range(chain_len):
                    r = i * chain_len + u
                    chain([r], load_kv(kh_ref, vh_ref, r), bias_halo)
                return carry
            lax.fori_loop(0, dil // chain_len, body, 0)
        elif dil > 1:
            assert DILATIONS[-1] // dil == chain_len

            def body(r, carry, chain=chain, load_kv=load_kv, kh_ref=kh_ref, vh_ref=vh_ref,
                     block_rows=block_rows):
                chain([r + b * block_rows for b in range(chain_len)],
                      load_kv(kh_ref, vh_ref, r), bias_halo)
                return carry
            lax.fori_loop(0, dil, body, 0)
        else:
            chain([b * block_rows for b in range(chain_len)], load_kv(kh_ref, vh_ref, 0), bias_halo)

            def body(i, carry, chain=chain, load_kv=load_kv, k_ref=k_ref, v_ref=v_ref,
                     block_rows=block_rows):
                base = pl.multiple_of(i * chain_len * block_rows, SPAN)
                chain([base + b * block_rows for b in range(chain_len)],
                      load_kv(k_ref, v_ref, base - block_rows), bias_prev)
                return carry
            lax.fori_loop(1, DILATIONS[-1] // chain_len, body, 0)

    merge_rows = 256

    def merge(ci, carry):
        rr = pl.ds(pl.multiple_of(ci * merge_rows, merge_rows), merge_rows)
        l0, l1, l2 = l_scr[0, rr, :], l_scr[1, rr, :], l_scr[2, rr, :]
        m = jnp.maximum(jnp.maximum(l0, l1), l2)
        e0, e1, e2 = jnp.exp2(l0 - m), jnp.exp2(l1 - m), jnp.exp2(l2 - m)
        inv = 1.0 / (e0 + e1 + e2)
        out_ref[0, rr, :] = (o_scr[0, rr, :] * (e0 * inv)).astype(out_ref.dtype)
        out_ref[1, rr, :] = (o_scr[1, rr, :] * (e1 * inv)).astype(out_ref.dtype)
        out_ref[2, rr, :] = (o_scr[2, rr, :] * (e2 * inv)).astype(out_ref.dtype)
        return carry

    lax.fori_loop(0, ATTN_BLOCK // merge_rows, merge, 0)


def _attention(qkv, *, batch):
    t = qkv.shape[0]
    tb = ATTN_BLOCK
    steps = t // batch // tb
    in_specs, operands = [], []
    for g, dil in enumerate(DILATIONS):
        halo = SPAN * dil
        ratio = tb // halo

        def cur(which, g=g):
            return pl.BlockSpec(
                (tb, HEAD_DIM),
                lambda b, i, j: (b * steps + i, which * ATTN_HEADS + g * HEADS_PER_GROUP + j))

        def prev(which, g=g, ratio=ratio, halo=halo):
            return pl.BlockSpec(
                (halo, HEAD_DIM),
                lambda b, i, j: (jnp.maximum((b * steps + i) * ratio - 1, 0),
                                 which * ATTN_HEADS + g * HEADS_PER_GROUP + j))

        in_specs += [cur(0), cur(1), cur(2), prev(1), prev(2)]
        operands += [qkv] * 5
    n_groups = len(DILATIONS)
    return pl.pallas_call(
        _attn_kernel,
        grid=(batch, steps, HEADS_PER_GROUP),
        in_specs=in_specs,
        out_specs=pl.BlockSpec((n_groups, tb, HEAD_DIM), lambda b, i, j: (0, b * steps + i, j)),
        out_shape=jax.ShapeDtypeStruct((n_groups, t, HEADS_PER_GROUP * HEAD_DIM), _BF16),
        scratch_shapes=[pltpu.VMEM((3, tb, HEAD_DIM), _F32), pltpu.VMEM((3, tb, HEAD_DIM), _F32)],
        compiler_params=_params("parallel", "arbitrary", "arbitrary"),
        name="dilated_attention",
    )(*operands)


def _rope_tables(seq_len):
    inv_freq = 1.0 / (ROPE_THETA ** (np.arange(0, HEAD_DIM, 2, dtype=np.float64) / HEAD_DIM))
    ang = np.arange(seq_len, dtype=np.float64)[:, None] * inv_freq[None, :]
    cos, sin = np.cos(ang), np.sin(ang)
    return (jnp.asarray(np.concatenate([cos, cos], axis=-1), _F32),
            jnp.asarray(np.concatenate([-sin, sin], axis=-1), _F32))


def kernel(x, norm_mix, norm_ffn, hgrn_w_in, hgrn_lb_logits, hgrn_out_norm, hgrn_w_out,
           attn_w_qkv, attn_w_out, ffn_w_in, ffn_w_down, final_norm):
    batch, seq, d = x.shape
    cos2, sin2 = _rope_tables(seq)
    h = x.reshape(batch * seq, d)
    bf = lambda w: w.astype(_BF16)

    gated = _hgrn_mixer(h, norm_mix[0], bf(hgrn_w_in[0]), hgrn_lb_logits, hgrn_out_norm[0],
                        batch=batch, layer=0)
    ffn_in, ffn_down = bf(ffn_w_in), bf(ffn_w_down)
    h = _block_tail(gated[None], bf(hgrn_w_out[0]), h, norm_ffn[0], ffn_in, ffn_down, final_norm,
                    layer=0, final_norm=False)

    qkv = _qkv_projection(h, norm_mix[1], bf(attn_w_qkv[0]), cos2, sin2,
                          scale=HEAD_DIM ** -0.5 * LOG2_E)
    attn = _attention(qkv, batch=batch)
    h = _block_tail(attn, bf(attn_w_out[0]), h, norm_ffn[1], ffn_in, ffn_down, final_norm,
                    layer=1, final_norm=True)
    return h.reshape(batch, seq, d)
```

```python
import functools

import numpy as np
import jax
import jax.numpy as jnp
from jax import lax
from jax.experimental import pallas as pl
from jax.experimental.pallas import tpu as pltpu

D_MODEL = 1024
HEAD_DIM = 128
HGRN_HEADS = 8
HGRN_CHUNK = 64
HGRN_STEP_TOKENS = 512
ATTN_HEADS = 12
ATTN_WIDTH = ATTN_HEADS * HEAD_DIM
DILATIONS = (1, 4, 16)
SPAN = 128
HEADS_PER_GROUP = 4
ATTN_BLOCK = SPAN * DILATIONS[-1]
ROW_TILE = 512
COL_TILE = 512
ROPE_THETA = 10000.0
NORM_EPS = 1e-6
NEG_BIG = -1e30
LOG2_E = float(np.log2(np.e))
VMEM_LIMIT_BYTES = 56 * 1024 * 1024

_F32 = jnp.float32
_BF16 = jnp.bfloat16


def _dot(a, b):
    return jnp.dot(a, b, preferred_element_type=_F32)


def _dot_nt(a, b):
    return lax.dot_general(a, b, (((1,), (1,)), ((), ())), preferred_element_type=_F32)


def _dot_tn(a, b):
    return lax.dot_general(a, b, (((0,), (0,)), ((), ())), preferred_element_type=_F32)


def _sigmoid(x):
    return 1.0 / (1.0 + jnp.exp(-x))


def _rms_scale(x):
    return lax.rsqrt(jnp.mean(x * x, axis=-1, keepdims=True) + NORM_EPS)


def _params(*sem):
    return pltpu.CompilerParams(dimension_semantics=sem, vmem_limit_bytes=VMEM_LIMIT_BYTES)


def _resident(shape, layer=None):
    if layer is None:
        index = (0,) * len(shape)
    else:
        index = (layer,) + (0,) * (len(shape) - 1)
        shape = (None,) + tuple(shape[1:])
    return pl.BlockSpec(shape, lambda *_: index, pipeline_mode=pl.Buffered(1))


def _qkv_kernel(h_ref, gain_ref, w_ref, cos_ref, sin_ref, o_ref, *, scale):
    x = h_ref[...]
    u = (x * _rms_scale(x) * gain_ref[...]).astype(_BF16)
    cos, sin = cos_ref[...], sin_ref[...]
    cos_q, sin_q = cos * scale, sin * scale
    for j in range(w_ref.shape[1] // COL_TILE):
        res = _dot(u, w_ref[:, j * COL_TILE:(j + 1) * COL_TILE])
        for hh in range(COL_TILE // HEAD_DIM):
            lo = j * COL_TILE + hh * HEAD_DIM
            xh = res[:, hh * HEAD_DIM:(hh + 1) * HEAD_DIM]
            if lo < ATTN_WIDTH:
                xh = xh * cos_q + pltpu.roll(xh, HEAD_DIM // 2, 1) * sin_q
            elif lo < 2 * ATTN_WIDTH:
                xh = xh * cos + pltpu.roll(xh, HEAD_DIM // 2, 1) * sin
            o_ref[:, lo:lo + HEAD_DIM] = xh


def _qkv_projection(h, gain, w, cos2, sin2, *, scale):
    t, d = h.shape
    n = w.shape[1]
    tm = ROW_TILE
    seq_tiles = cos2.shape[0] // tm
    return pl.pallas_call(
        functools.partial(_qkv_kernel, scale=scale),
        grid=(t // tm,),
        in_specs=[
            pl.BlockSpec((tm, d), lambda i: (i, 0)),
            _resident((1, d)),
            _resident((d, n)),
            pl.BlockSpec((tm, HEAD_DIM), lambda i: (i % seq_tiles, 0)),
            pl.BlockSpec((tm, HEAD_DIM), lambda i: (i % seq_tiles, 0)),
        ],
        out_specs=pl.BlockSpec((tm, n), lambda i: (i, 0)),
        out_shape=jax.ShapeDtypeStruct((t, n), _F32),
        compiler_params=_params("parallel"),
        name="qkv_projection",
    )(h, gain.reshape(1, d), w, cos2, sin2)


def _hgrn_tables():
    c = HGRN_CHUNK
    t = np.arange(c)
    col = t[None, :]
    row = t[:, None]
    sums = np.zeros((8, c, c), np.float32)
    sums[0] = col <= row
    sums[1] = col > row
    masks = np.zeros((7, c, c), np.float32)
    half = c // 2
    level = 0
    while half >= 1:
        block = t // (2 * half)
        mid = block * 2 * half + half
        is_query = t >= mid
        q_rows = (col >= mid[:, None]) & (col <= row) & is_query[:, None]
        k_rows = (col > row) & (col < mid[:, None]) & (~is_query)[:, None]
        sums[2 + level] = q_rows | k_rows
        masks[level] = ((block[:, None] == block[None, :]) & is_query[:, None]
                        & (~is_query)[None, :])
        half //= 2
        level += 1
    masks[6] = np.eye(c)
    assert level == 6 and np.array_equal(masks.sum(0), np.tril(np.ones((c, c))))
    sums = sums.reshape(8 * c, c)
    return np.concatenate([sums, sums], axis=1), masks


def _hgrn_kernel(h_ref, ngain_ref, w_ref, lbl_ref, gain_ref, sums_ref, masks_ref, o_ref,
                 state_ref, proj_ref, a_ref, b_ref, v_ref, s_ref, *, layer):
    c = HGRN_CHUNK
    width = HGRN_HEADS * HEAD_DIM

    @pl.when(pl.program_id(1) == 0)
    def _():
        state_ref[...] = jnp.zeros_like(state_ref)

    x = h_ref[...]
    u = (x * _rms_scale(x) * ngain_ref[...]).astype(_BF16)
    for j in range(w_ref.shape[1] // COL_TILE):
        cols = slice(j * COL_TILE, (j + 1) * COL_TILE)
        proj_ref[:, cols] = _dot(u, w_ref[:, cols])

    logits = lbl_ref[...]
    e = jnp.exp(logits - jnp.max(logits, axis=0, keepdims=True))
    lb = jnp.sum(e[:layer + 1], axis=0, keepdims=True) / jnp.sum(e, axis=0, keepdims=True)
    out_gain = gain_ref[...]

    def chunk(ci, carry):
        r0 = pl.multiple_of(ci * c, c)
        rows = pl.ds(r0, c)
        q = proj_ref[rows, 0:width]
        forget = lb + (1.0 - lb) * _sigmoid(proj_ref[rows, width:2 * width])
        glog = jnp.log(forget) * LOG2_E
        kk = 1.0 - forget
        qq = q * _sigmoid(q)
        g_hi = glog.astype(_BF16)
        g_lo = (glog - g_hi.astype(_F32)).astype(_BF16)
        factors = jnp.exp2(_dot(sums_ref[...], jnp.concatenate([g_hi, g_lo], axis=0)))

        from_start = factors[0:c]
        a_ref[0] = (qq * from_start).astype(_BF16)
        chunk_decay = from_start[c - 1:c, :]
        b_ref[0] = (kk * factors[c:2 * c]).astype(_BF16)
        for level in range(2, 8):
            fac = factors[level * c:(level + 1) * c]
            a_ref[level - 1] = (qq * fac).astype(_BF16)
            b_ref[level - 1] = (kk * fac).astype(_BF16)
        a_ref[7] = qq.astype(_BF16)
        b_ref[7] = kk.astype(_BF16)
        v_ref[...] = proj_ref[rows, 2 * width:3 * width].astype(_BF16)

        def lanes(h):
            return slice(h * HEAD_DIM, (h + 1) * HEAD_DIM)

        pairs = [(h, h + 1) for h in range(0, HGRN_HEADS, 2)]
        for pair in pairs:
            scores = [jnp.zeros((c, c), _F32) for _ in pair]
            for level in range(1, 8):
                for i, h in enumerate(pair):
                    scores[i] += masks_ref[level - 1] * _dot_nt(a_ref[level, :, lanes(h)],
                                                                b_ref[level, :, lanes(h)])
            for i, h in enumerate(pair):
                s_ref[h] = scores[i].astype(_BF16)
        for pair in pairs:
            states = [state_ref[h] for h in pair]
            intra = [_dot(s_ref[h], v_ref[:, lanes(h)]) for h in pair]
            inter = [_dot_nt(a_ref[0, :, lanes(h)], states[i].astype(_BF16))
                     for i, h in enumerate(pair)]
            update = [_dot_tn(v_ref[:, lanes(h)], b_ref[0, :, lanes(h)]) for h in pair]
            for i, h in enumerate(pair):
                state_ref[h] = states[i] * chunk_decay[:, lanes(h)] + update[i]
                o = intra[i] + inter[i]
                o = o * _rms_scale(o) * out_gain
                gate = proj_ref[rows, 3 * width + h * HEAD_DIM:3 * width + (h + 1) * HEAD_DIM]
                o_ref[rows, lanes(h)] = (o * (gate * _sigmoid(gate))).astype(o_ref.dtype)
        return carry

    lax.fori_loop(0, h_ref.shape[0] // c, chunk, 0)


def _hgrn_mixer(h, norm_gain, w_in, lb_logits, out_gain, *, batch, layer):
    t, d = h.shape
    width = HGRN_HEADS * HEAD_DIM
    tc = HGRN_STEP_TOKENS
    steps = t // batch // tc
    sums, masks = _hgrn_tables()
    c = HGRN_CHUNK
    return pl.pallas_call(
        functools.partial(_hgrn_kernel, layer=layer),
        grid=(batch, steps),
        in_specs=[
            pl.BlockSpec((tc, d), lambda b, s: (b * steps + s, 0)),
            _resident((1, d)),
            _resident(w_in.shape),
            _resident(lb_logits.shape),
            _resident((1, HEAD_DIM)),
            _resident(sums.shape),
            _resident(masks.shape),
        ],
        out_specs=pl.BlockSpec((tc, width), lambda b, s: (b * steps + s, 0)),
        out_shape=jax.ShapeDtypeStruct((t, width), _BF16),
        scratch_shapes=[
            pltpu.VMEM((HGRN_HEADS, HEAD_DIM, HEAD_DIM), _F32),
            pltpu.VMEM((tc, 4 * width), _F32),
            pltpu.VMEM((8, c, width), _BF16),
            pltpu.VMEM((8, c, width), _BF16),
            pltpu.VMEM((c, width), _BF16),
            pltpu.VMEM((HGRN_HEADS, c, c), _BF16),
        ],
        compiler_params=_params("parallel", "arbitrary"),
        name="hgrn_mixer",
    )(h, norm_gain.reshape(1, d), w_in, lb_logits, out_gain.reshape(1, HEAD_DIM),
      jnp.asarray(sums, _BF16), jnp.asarray(masks, _F32))


def _tail_kernel(a_ref, wo_ref, h_ref, gain_ref, wi_ref, wd_ref, fgain_ref, o_ref, u_ref, *,
                 final_norm, ff_tile):
    n_slabs, _, kw = a_ref.shape
    mixed = h_ref[...]
    for s in range(n_slabs):
        mixed += _dot(a_ref[s], wo_ref[s * kw:(s + 1) * kw, :])
    o_ref[...] = mixed
    u_ref[...] = (mixed * _rms_scale(mixed) * gain_ref[...]).astype(_BF16)
    d_ff = wd_ref.shape[0]
    for j in range(d_ff // ff_tile):
        u = u_ref[...]
        gate = _dot(u, wi_ref[:, j * ff_tile:(j + 1) * ff_tile])
        up = _dot(u, wi_ref[:, d_ff + j * ff_tile:d_ff + (j + 1) * ff_tile])
        act = (gate * _sigmoid(gate) * up).astype(_BF16)
        o_ref[...] += _dot(act, wd_ref[j * ff_tile:(j + 1) * ff_tile, :])
    if final_norm:
        y = o_ref[...]
        o_ref[...] = y * _rms_scale(y) * fgain_ref[...]


def _block_tail(a, w_out, h, gain, w_in, w_down, final_gain, *, layer, final_norm, ff_tile=256):
    n_slabs, t, kw = a.shape
    d = h.shape[1]
    tm = ROW_TILE
    return pl.pallas_call(
        functools.partial(_tail_kernel, final_norm=final_norm, ff_tile=ff_tile),
        grid=(t // tm,),
        in_specs=[
            pl.BlockSpec((n_slabs, tm, kw), lambda i: (0, i, 0)),
            _resident(w_out.shape),
            pl.BlockSpec((tm, d), lambda i: (i, 0)),
            _resident((1, d)),
            _resident(w_in.shape, layer=layer),
            _resident(w_down.shape, layer=layer),
            _resident((1, d)),
        ],
        out_specs=pl.BlockSpec((tm, d), lambda i: (i, 0)),
        out_shape=jax.ShapeDtypeStruct((t, d), _F32),
        scratch_shapes=[pltpu.VMEM((tm, d), _BF16)],
        compiler_params=_params("parallel"),
        name="block_tail",
    )(a, w_out, h, gain.reshape(1, d), w_in, w_down, final_gain.reshape(1, d))


def _attn_kernel(*refs):
    ins = refs[:15]
    out_ref, o_scr, l_scr = refs[15:]
    first_block = pl.program_id(1) == 0
    row = lax.broadcasted_iota(jnp.int32, (SPAN, SPAN), 0)
    col = lax.broadcasted_iota(jnp.int32, (SPAN, SPAN), 1)
    bias_cur = jnp.where(col <= row, 0.0, NEG_BIG).astype(_F32)
    bias_prev = jnp.where(col >= row, 0.0, NEG_BIG).astype(_F32)
    bias_halo = bias_prev + jnp.where(first_block, NEG_BIG, 0.0).astype(_F32)
    ones = jnp.ones((SPAN, HEAD_DIM), _BF16)
    chain_len = 4
    body_units = 8

    for g, dil in enumerate(DILATIONS):
        q_ref, k_ref, v_ref, kh_ref, vh_ref = ins[5 * g:5 * g + 5]

        def load(ref, start, dil=dil):
            idx = pl.ds(start, SPAN) if dil == 1 else pl.ds(start, SPAN, stride=dil)
            return ref[idx, :].astype(_BF16)

        def load_kv(kref, vref, start, load=load):
            return load(kref, start), jnp.concatenate([load(vref, start), ones], axis=1)

        def chain(starts, prev, prev_bias, g=g, dil=dil, q_ref=q_ref, k_ref=k_ref, v_ref=v_ref,
                  load=load, load_kv=load_kv):
            for start in starts:
                kp, vp = prev
                kc, vc = cur = load_kv(k_ref, v_ref, start)
                s = _dot_nt(load(q_ref, start), jnp.concatenate([kp, kc], axis=0))
                s = s + jnp.concatenate([prev_bias, bias_cur], axis=1)
                m = jnp.max(jnp.maximum(s[:, :SPAN], s[:, SPAN:]), axis=-1, keepdims=True)
                p = jnp.exp2(s - m).astype(_BF16)
                r = _dot(p, jnp.concatenate([vp, vc], axis=0))
                denom = r[:, HEAD_DIM:]
                idx = pl.ds(start, SPAN) if dil == 1 else pl.ds(start, SPAN, stride=dil)
                o_scr[g, idx, :] = r[:, :HEAD_DIM] / denom
                l_scr[g, idx, :] = m + jnp.log2(denom)
                prev, prev_bias = cur, bias_prev

        block_rows = SPAN * dil
        if dil == DILATIONS[-1]:
            def body(i, carry, chain=chain, load_kv=load_kv, kh_ref=kh_ref, vh_ref=vh_ref):
                for u in range(body_units):
                    r = i * body_units + u
                    chain([r], load_kv(kh_ref, vh_ref, r), bias_halo)
                return carry
            lax.fori_loop(0, dil // body_units, body, 0)
        elif dil > 1:
            assert DILATIONS[-1] // dil == chain_len

            def body(i, carry, chain=chain, load_kv=load_kv, kh_ref=kh_ref, vh_ref=vh_ref,
                     block_rows=block_rows):
                for u in range(body_units // chain_len):
                    r = i * (body_units // chain_len) + u
                    chain([r + b * block_rows for b in range(chain_len)],
                          load_kv(kh_ref, vh_ref, r), bias_halo)
                return carry
            lax.fori_loop(0, dil * chain_len // body_units, body, 0)
        else:
            for first in range(0, DILATIONS[-1], chain_len):
                prev = (load_kv(kh_ref, vh_ref, 0) if first == 0
                        else load_kv(k_ref, v_ref, (first - 1) * block_rows))
                chain([(first + b) * block_rows for b in range(chain_len)], prev,
                      bias_halo if first == 0 else bias_prev)

    merge_rows = 256

    def merge(ci, carry):
        rr = pl.ds(pl.multiple_of(ci * merge_rows, merge_rows), merge_rows)
        l0, l1, l2 = l_scr[0, rr, :], l_scr[1, rr, :], l_scr[2, rr, :]
        m = jnp.maximum(jnp.maximum(l0, l1), l2)
        e0, e1, e2 = jnp.exp2(l0 - m), jnp.exp2(l1 - m), jnp.exp2(l2 - m)
        inv = 1.0 / (e0 + e1 + e2)
        out_ref[0, rr, :] = (o_scr[0, rr, :] * (e0 * inv)).astype(out_ref.dtype)
        out_ref[1, rr, :] = (o_scr[1, rr, :] * (e1 * inv)).astype(out_ref.dtype)
        out_ref[2, rr, :] = (o_scr[2, rr, :] * (e2 * inv)).astype(out_ref.dtype)
        return carry

    lax.fori_loop(0, ATTN_BLOCK // merge_rows, merge, 0)


def _attention(qkv, *, batch):
    t = qkv.shape[0]
    tb = ATTN_BLOCK
    steps = t // batch // tb
    in_specs, operands = [], []
    for g, dil in enumerate(DILATIONS):
        halo = SPAN * dil
        ratio = tb // halo

        def cur(which, g=g):
            return pl.BlockSpec(
                (tb, HEAD_DIM),
                lambda b, i, j: (b * steps + i, which * ATTN_HEADS + g * HEADS_PER_GROUP + j))

        def prev(which, g=g, ratio=ratio, halo=halo):
            return pl.BlockSpec(
                (halo, HEAD_DIM),
                lambda b, i, j: (jnp.maximum((b * steps + i) * ratio - 1, 0),
                                 which * ATTN_HEADS + g * HEADS_PER_GROUP + j))

        in_specs += [cur(0), cur(1), cur(2), prev(1), prev(2)]
        operands += [qkv] * 5
    n_groups = len(DILATIONS)
    return pl.pallas_call(
        _attn_kernel,
        grid=(batch, steps, HEADS_PER_GROUP),
        in_specs=in_specs,
        out_specs=pl.BlockSpec((n_groups, tb, HEAD_DIM), lambda b, i, j: (0, b * steps + i, j)),
        out_shape=jax.ShapeDtypeStruct((n_groups, t, HEADS_PER_GROUP * HEAD_DIM), _BF16),
        scratch_shapes=[pltpu.VMEM((3, tb, HEAD_DIM), _F32), pltpu.VMEM((3, tb, HEAD_DIM), _F32)],
        compiler_params=_params("parallel", "arbitrary", "arbitrary"),
        name="dilated_attention",
    )(*operands)


def _rope_tables(seq_len):
    inv_freq = 1.0 / (ROPE_THETA ** (np.arange(0, HEAD_DIM, 2, dtype=np.float64) / HEAD_DIM))
    ang = np.arange(seq_len, dtype=np.float64)[:, None] * inv_freq[None, :]
    cos, sin = np.cos(ang), np.sin(ang)
    return (jnp.asarray(np.concatenate([cos, cos], axis=-1), _F32),
            jnp.asarray(np.concatenate([-sin, sin], axis=-1), _F32))


def kernel(x, norm_mix, norm_ffn, hgrn_w_in, hgrn_lb_logits, hgrn_out_norm, hgrn_w_out,
           attn_w_qkv, attn_w_out, ffn_w_in, ffn_w_down, final_norm):
    batch, seq, d = x.shape
    cos2, sin2 = _rope_tables(seq)
    h = x.reshape(batch * seq, d)
    bf = lambda w: w.astype(_BF16)

    gated = _hgrn_mixer(h, norm_mix[0], bf(hgrn_w_in[0]), hgrn_lb_logits, hgrn_out_norm[0],
                        batch=batch, layer=0)
    ffn_in, ffn_down = bf(ffn_w_in), bf(ffn_w_down)
    h = _block_tail(gated[None], bf(hgrn_w_out[0]), h, norm_ffn[0], ffn_in, ffn_down, final_norm,
                    layer=0, final_norm=False)

    qkv = _qkv_projection(h, norm_mix[1], bf(attn_w_qkv[0]), cos2, sin2,
                          scale=HEAD_DIM ** -0.5 * LOG2_E)
    attn = _attention(qkv, batch=batch)
    h = _block_tail(attn, bf(attn_w_out[0]), h, norm_ffn[1], ffn_in, ffn_down, final_norm,
                    layer=1, final_norm=True)
    return h.reshape(batch, seq, d)
```

```python
import functools

import numpy as np
import jax
import jax.numpy as jnp
from jax import lax
from jax.experimental import pallas as pl
from jax.experimental.pallas import tpu as pltpu

D_MODEL = 1024
HEAD_DIM = 128
HGRN_HEADS = 8
HGRN_CHUNK = 64
HGRN_STEP_TOKENS = 512
ATTN_HEADS = 12
ATTN_WIDTH = ATTN_HEADS * HEAD_DIM
DILATIONS = (1, 4, 16)
SPAN = 128
HEADS_PER_GROUP = 4
ATTN_BLOCK = SPAN * DILATIONS[-1]
ROW_TILE = 512
COL_TILE = 512
ROPE_THETA = 10000.0
NORM_EPS = 1e-6
NEG_BIG = -1e30
LOG2_E = float(np.log2(np.e))
VMEM_LIMIT_BYTES = 56 * 1024 * 1024

_F32 = jnp.float32
_BF16 = jnp.bfloat16


def _dot(a, b):
    return jnp.dot(a, b, preferred_element_type=_F32)


def _dot_nt(a, b):
    return lax.dot_general(a, b, (((1,), (1,)), ((), ())), preferred_element_type=_F32)


def _dot_tn(a, b):
    return lax.dot_general(a, b, (((0,), (0,)), ((), ())), preferred_element_type=_F32)


def _sigmoid(x):
    return 1.0 / (1.0 + jnp.exp2(x * (-LOG2_E)))


def _rms_scale(x):
    return lax.rsqrt(jnp.mean(x * x, axis=-1, keepdims=True) + NORM_EPS)


def _params(*sem):
    return pltpu.CompilerParams(dimension_semantics=sem, vmem_limit_bytes=VMEM_LIMIT_BYTES)


def _resident(shape, layer=None):
    if layer is None:
        index = (0,) * len(shape)
    else:
        index = (layer,) + (0,) * (len(shape) - 1)
        shape = (None,) + tuple(shape[1:])
    return pl.BlockSpec(shape, lambda *_: index, pipeline_mode=pl.Buffered(1))


def _qkv_kernel(h_ref, gain_ref, w_ref, cos_ref, sin_ref, o_ref, *, scale):
    x = h_ref[...]
    u = (x * _rms_scale(x) * gain_ref[...]).astype(_BF16)
    cos, sin = cos_ref[...], sin_ref[...]
    cos_q, sin_q = cos * scale, sin * scale
    for j in range(w_ref.shape[1] // COL_TILE):
        res = _dot(u, w_ref[:, j * COL_TILE:(j + 1) * COL_TILE])
        for hh in range(COL_TILE // HEAD_DIM):
            lo = j * COL_TILE + hh * HEAD_DIM
            xh = res[:, hh * HEAD_DIM:(hh + 1) * HEAD_DIM]
            if lo < ATTN_WIDTH:
                xh = xh * cos_q + pltpu.roll(xh, HEAD_DIM // 2, 1) * sin_q
            elif lo < 2 * ATTN_WIDTH:
                xh = xh * cos + pltpu.roll(xh, HEAD_DIM // 2, 1) * sin
            o_ref[:, lo:lo + HEAD_DIM] = xh


def _qkv_projection(h, gain, w, cos2, sin2, *, scale):
    t, d = h.shape
    n = w.shape[1]
    tm = ROW_TILE
    seq_tiles = cos2.shape[0] // tm
    return pl.pallas_call(
        functools.partial(_qkv_kernel, scale=scale),
        grid=(t // tm,),
        in_specs=[
            pl.BlockSpec((tm, d), lambda i: (i, 0)),
            _resident((1, d)),
            _resident((d, n)),
            pl.BlockSpec((tm, HEAD_DIM), lambda i: (i % seq_tiles, 0)),
            pl.BlockSpec((tm, HEAD_DIM), lambda i: (i % seq_tiles, 0)),
        ],
        out_specs=pl.BlockSpec((tm, n), lambda i: (i, 0)),
        out_shape=jax.ShapeDtypeStruct((t, n), _F32),
        compiler_params=_params("parallel"),
        name="qkv_projection",
    )(h, gain.reshape(1, d), w, cos2, sin2)


def _hgrn_tables():
    c = HGRN_CHUNK
    t = np.arange(c)
    col = t[None, :]
    row = t[:, None]
    sums = np.zeros((8, c, c), np.float32)
    sums[0] = col <= row
    sums[1] = col > row
    masks = np.zeros((7, c, c), np.float32)
    half = c // 2
    level = 0
    while half >= 1:
        block = t // (2 * half)
        mid = block * 2 * half + half
        is_query = t >= mid
        q_rows = (col >= mid[:, None]) & (col <= row) & is_query[:, None]
        k_rows = (col > row) & (col < mid[:, None]) & (~is_query)[:, None]
        sums[2 + level] = q_rows | k_rows
        masks[level] = ((block[:, None] == block[None, :]) & is_query[:, None]
                        & (~is_query)[None, :])
        half //= 2
        level += 1
    masks[6] = np.eye(c)
    assert level == 6 and np.array_equal(masks.sum(0), np.tril(np.ones((c, c))))
    sums = sums.reshape(8 * c, c)
    return np.concatenate([sums, sums], axis=1), masks


def _hgrn_kernel(h_ref, ngain_ref, w_ref, lbl_ref, gain_ref, sums_ref, masks_ref, o_ref,
                 state_ref, proj_ref, a2_ref, b2_ref, v2_ref, decay_ref, s_ref, *, layer):
    c = HGRN_CHUNK
    width = HGRN_HEADS * HEAD_DIM

    @pl.when(pl.program_id(1) == 0)
    def _():
        state_ref[...] = jnp.zeros_like(state_ref)

    x = h_ref[...]
    u = (x * _rms_scale(x) * ngain_ref[...]).astype(_BF16)
    for j in range(w_ref.shape[1] // COL_TILE):
        cols = slice(j * COL_TILE, (j + 1) * COL_TILE)
        proj_ref[:, cols] = _dot(u, w_ref[:, cols])

    logits = lbl_ref[...]
    e = jnp.exp(logits - jnp.max(logits, axis=0, keepdims=True))
    lb = jnp.sum(e[:layer + 1], axis=0, keepdims=True) / jnp.sum(e, axis=0, keepdims=True)
    out_gain = gain_ref[...]

    n_chunks = h_ref.shape[0] // c

    def chunk_rows(ci):
        return pl.ds(pl.multiple_of(ci * c, c), c)

    def prepare(ci, slot):
        a_ref, b_ref, v_ref = a2_ref.at[slot], b2_ref.at[slot], v2_ref.at[slot]
        rows = chunk_rows(ci)
        q = proj_ref[rows, 0:width]
        forget = lb + (1.0 - lb) * _sigmoid(proj_ref[rows, width:2 * width])
        glog = jnp.log(forget) * LOG2_E
        kk = 1.0 - forget
        qq = q * _sigmoid(q)
        g_hi = glog.astype(_BF16)
        g_lo = (glog - g_hi.astype(_F32)).astype(_BF16)
        factors = jnp.exp2(_dot(sums_ref[...], jnp.concatenate([g_hi, g_lo], axis=0)))

        from_start = factors[0:c]
        a_ref[0] = (qq * from_start).astype(_BF16)
        decay_ref[slot] = from_start[c - 1:c, :]
        b_ref[0] = (kk * factors[c:2 * c]).astype(_BF16)
        for level in range(2, 8):
            fac = factors[level * c:(level + 1) * c]
            a_ref[level - 1] = (qq * fac).astype(_BF16)
            b_ref[level - 1] = (kk * fac).astype(_BF16)
        a_ref[7] = qq.astype(_BF16)
        b_ref[7] = kk.astype(_BF16)
        v_ref[...] = proj_ref[rows, 2 * width:3 * width].astype(_BF16)

    def lanes(h):
        return slice(h * HEAD_DIM, (h + 1) * HEAD_DIM)

    def consume(ci, slot):
        a_ref, b_ref, v_ref = a2_ref.at[slot], b2_ref.at[slot], v2_ref.at[slot]
        rows = chunk_rows(ci)
        chunk_decay = decay_ref[slot]
        pairs = [(h, h + 1) for h in range(0, HGRN_HEADS, 2)]
        for pair in pairs:
            scores = [jnp.zeros((c, c), _F32) for _ in pair]
            for level in range(1, 8):
                for i, h in enumerate(pair):
                    scores[i] += masks_ref[level - 1] * _dot_nt(a_ref[level, :, lanes(h)],
                                                                b_ref[level, :, lanes(h)])
            for i, h in enumerate(pair):
                s_ref[h] = scores[i].astype(_BF16)
        for pair in pairs:
            states = [state_ref[h] for h in pair]
            intra = [_dot(s_ref[h], v_ref[:, lanes(h)]) for h in pair]
            inter = [_dot_nt(a_ref[0, :, lanes(h)], states[i].astype(_BF16))
                     for i, h in enumerate(pair)]
            update = [_dot_tn(v_ref[:, lanes(h)], b_ref[0, :, lanes(h)]) for h in pair]
            for i, h in enumerate(pair):
                state_ref[h] = states[i] * chunk_decay[:, lanes(h)] + update[i]
                o = intra[i] + inter[i]
                o = o * _rms_scale(o) * out_gain
                gate = proj_ref[rows, 3 * width + h * HEAD_DIM:3 * width + (h + 1) * HEAD_DIM]
                o_ref[rows, lanes(h)] = (o * (gate * _sigmoid(gate))).astype(o_ref.dtype)

    def two_chunks(i, carry):
        first = 2 * i
        prepare(first + 1, 1)
        consume(first, 0)
        prepare(jnp.minimum(first + 2, n_chunks - 1), 0)
        consume(first + 1, 1)
        return carry

    prepare(0, 0)
    lax.fori_loop(0, n_chunks // 2, two_chunks, 0)


def _hgrn_mixer(h, norm_gain, w_in, lb_logits, out_gain, *, batch, layer):
    t, d = h.shape
    width = HGRN_HEADS * HEAD_DIM
    tc = HGRN_STEP_TOKENS
    steps = t // batch // tc
    sums, masks = _hgrn_tables()
    c = HGRN_CHUNK
    return pl.pallas_call(
        functools.partial(_hgrn_kernel, layer=layer),
        grid=(batch, steps),
        in_specs=[
            pl.BlockSpec((tc, d), lambda b, s: (b * steps + s, 0)),
            _resident((1, d)),
            _resident(w_in.shape),
            _resident(lb_logits.shape),
            _resident((1, HEAD_DIM)),
            _resident(sums.shape),
            _resident(masks.shape),
        ],
        out_specs=pl.BlockSpec((tc, width), lambda b, s: (b * steps + s, 0)),
        out_shape=jax.ShapeDtypeStruct((t, width), _BF16),
        scratch_shapes=[
            pltpu.VMEM((HGRN_HEADS, HEAD_DIM, HEAD_DIM), _F32),
            pltpu.VMEM((tc, 4 * width), _F32),
            pltpu.VMEM((2, 8, c, width), _BF16),
            pltpu.VMEM((2, 8, c, width), _BF16),
            pltpu.VMEM((2, c, width), _BF16),
            pltpu.VMEM((2, 1, width), _F32),
            pltpu.VMEM((HGRN_HEADS, c, c), _BF16),
        ],
        compiler_params=_params("parallel", "arbitrary"),
        name="hgrn_mixer",
    )(h, norm_gain.reshape(1, d), w_in, lb_logits, out_gain.reshape(1, HEAD_DIM),
      jnp.asarray(sums, _BF16), jnp.asarray(masks, _F32))


def _tail_kernel(a_ref, wo_ref, h_ref, gain_ref, wi_ref, wd_ref, fgain_ref, o_ref, u_ref, *,
                 final_norm, ff_tile):
    n_slabs, _, kw = a_ref.shape
    mixed = h_ref[...]
    for s in range(n_slabs):
        mixed += _dot(a_ref[s], wo_ref[s * kw:(s + 1) * kw, :])
    o_ref[...] = mixed
    u_ref[...] = (mixed * _rms_scale(mixed) * gain_ref[...]).astype(_BF16)
    d_ff = wd_ref.shape[0]
    for j in range(d_ff // ff_tile):
        u = u_ref[...]
        gate = _dot(u, wi_ref[:, j * ff_tile:(j + 1) * ff_tile])
        up = _dot(u, wi_ref[:, d_ff + j * ff_tile:d_ff + (j + 1) * ff_tile])
        act = (gate * _sigmoid(gate) * up).astype(_BF16)
        o_ref[...] += _dot(act, wd_ref[j * ff_tile:(j + 1) * ff_tile, :])
    if final_norm:
        y = o_ref[...]
        o_ref[...] = y * _rms_scale(y) * fgain_ref[...]


def _block_tail(a, w_out, h, gain, w_in, w_down, final_gain, *, layer, final_norm, ff_tile=256):
    n_slabs, t, kw = a.shape
    d = h.shape[1]
    tm = ROW_TILE
    return pl.pallas_call(
        functools.partial(_tail_kernel, final_norm=final_norm, ff_tile=ff_tile),
        grid=(t // tm,),
        in_specs=[
            pl.BlockSpec((n_slabs, tm, kw), lambda i: (0, i, 0)),
            _resident(w_out.shape),
            pl.BlockSpec((tm, d), lambda i: (i, 0)),
            _resident((1, d)),
            _resident(w_in.shape, layer=layer),
            _resident(w_down.shape, layer=layer),
            _resident((1, d)),
        ],
        out_specs=pl.BlockSpec((tm, d), lambda i: (i, 0)),
        out_shape=jax.ShapeDtypeStruct((t, d), _F32),
        scratch_shapes=[pltpu.VMEM((tm, d), _BF16)],
        compiler_params=_params("parallel"),
        name="block_tail",
    )(a, w_out, h, gain.reshape(1, d), w_in, w_down, final_gain.reshape(1, d))


def _attn_kernel(*refs):
    ins = refs[:15]
    out_ref, o_scr, l_scr = refs[15:]
    first_block = pl.program_id(1) == 0
    row = lax.broadcasted_iota(jnp.int32, (SPAN, SPAN), 0)
    col = lax.broadcasted_iota(jnp.int32, (SPAN, SPAN), 1)
    bias_cur = jnp.where(col <= row, 0.0, NEG_BIG).astype(_F32)
    bias_prev = jnp.where(col >= row, 0.0, NEG_BIG).astype(_F32)
    bias_halo = bias_prev + jnp.where(first_block, NEG_BIG, 0.0).astype(_F32)
    ones = jnp.ones((SPAN, HEAD_DIM), _BF16)
    chain_len = 4
    body_units = 8

    for g, dil in enumerate(DILATIONS):
        q_ref, k_ref, v_ref, kh_ref, vh_ref = ins[5 * g:5 * g + 5]

        def load(ref, start, dil=dil):
            idx = pl.ds(start, SPAN) if dil == 1 else pl.ds(start, SPAN, stride=dil)
            return ref[idx, :].astype(_BF16)

        def load_kv(kref, vref, start, load=load):
            return load(kref, start), jnp.concatenate([load(vref, start), ones], axis=1)

        def chain(starts, prev, prev_bias, g=g, dil=dil, q_ref=q_ref, k_ref=k_ref, v_ref=v_ref,
                  load=load, load_kv=load_kv):
            for start in starts:
                kp, vp = prev
                kc, vc = cur = load_kv(k_ref, v_ref, start)
                s = _dot_nt(load(q_ref, start), jnp.concatenate([kp, kc], axis=0))
                s = s + jnp.concatenate([prev_bias, bias_cur], axis=1)
                m = jnp.max(jnp.maximum(s[:, :SPAN], s[:, SPAN:]), axis=-1, keepdims=True)
                p = jnp.exp2(s - m).astype(_BF16)
                r = _dot(p, jnp.concatenate([vp, vc], axis=0))
                denom = r[:, HEAD_DIM:]
                idx = pl.ds(start, SPAN) if dil == 1 else pl.ds(start, SPAN, stride=dil)
                o_scr[g, idx, :] = r[:, :HEAD_DIM] / denom
                l_scr[g, idx, :] = m + jnp.log2(denom)
                prev, prev_bias = cur, bias_prev

        block_rows = SPAN * dil
        if dil == DILATIONS[-1]:
            def body(i, carry, chain=chain, load_kv=load_kv, kh_ref=kh_ref, vh_ref=vh_ref):
                for u in range(body_units):
                    r = i * body_units + u
                    chain([r], load_kv(kh_ref, vh_ref, r), bias_halo)
                return carry
            lax.fori_loop(0, dil // body_units, body, 0)
        elif dil > 1:
            assert DILATIONS[-1] // dil == chain_len

            def body(i, carry, chain=chain, load_kv=load_kv, kh_ref=kh_ref, vh_ref=vh_ref,
                     block_rows=block_rows):
                for u in range(body_units // chain_len):
                    r = i * (body_units // chain_len) + u
                    chain([r + b * block_rows for b in range(chain_len)],
                          load_kv(kh_ref, vh_ref, r), bias_halo)
                return carry
            lax.fori_loop(0, dil * chain_len // body_units, body, 0)
        else:
            for first in range(0, DILATIONS[-1], chain_len):
                prev = (load_kv(kh_ref, vh_ref, 0) if first == 0
                        else load_kv(k_ref, v_ref, (first - 1) * block_rows))
                chain([(first + b) * block_rows for b in range(chain_len)], prev,
                      bias_halo if first == 0 else bias_prev)

    merge_rows = 256

    def merge(ci, carry):
        rr = pl.ds(pl.multiple_of(ci * merge_rows, merge_rows), merge_rows)
        l0, l1, l2 = l_scr[0, rr, :], l_scr[1, rr, :], l_scr[2, rr, :]
        m = jnp.maximum(jnp.maximum(l0, l1), l2)
        e0, e1, e2 = jnp.exp2(l0 - m), jnp.exp2(l1 - m), jnp.exp2(l2 - m)
        inv = 1.0 / (e0 + e1 + e2)
        out_ref[0, rr, :] = (o_scr[0, rr, :] * (e0 * inv)).astype(out_ref.dtype)
        out_ref[1, rr, :] = (o_scr[1, rr, :] * (e1 * inv)).astype(out_ref.dtype)
        out_ref[2, rr, :] = (o_scr[2, rr, :] * (e2 * inv)).astype(out_ref.dtype)
        return carry

    lax.fori_loop(0, ATTN_BLOCK // merge_rows, merge, 0)


def _attention(qkv, *, batch):
    t = qkv.shape[0]
    tb = ATTN_BLOCK
    steps = t // batch // tb
    in_specs, operands = [], []
    for g, dil in enumerate(DILATIONS):
        halo = SPAN * dil
        ratio = tb // halo

        def cur(which, g=g):
            return pl.BlockSpec(
                (tb, HEAD_DIM),
                lambda b, i, j: (b * steps + i, which * ATTN_HEADS + g * HEADS_PER_GROUP + j))

        def prev(which, g=g, ratio=ratio, halo=halo):
            return pl.BlockSpec(
                (halo, HEAD_DIM),
                lambda b, i, j: (jnp.maximum((b * steps + i) * ratio - 1, 0),
                                 which * ATTN_HEADS + g * HEADS_PER_GROUP + j))

        in_specs += [cur(0), cur(1), cur(2), prev(1), prev(2)]
        operands += [qkv] * 5
    n_groups = len(DILATIONS)
    return pl.pallas_call(
        _attn_kernel,
        grid=(batch, steps, HEADS_PER_GROUP),
        in_specs=in_specs,
        out_specs=pl.BlockSpec((n_groups, tb, HEAD_DIM), lambda b, i, j: (0, b * steps + i, j)),
        out_shape=jax.ShapeDtypeStruct((n_groups, t, HEADS_PER_GROUP * HEAD_DIM), _BF16),
        scratch_shapes=[pltpu.VMEM((3, tb, HEAD_DIM), _F32), pltpu.VMEM((3, tb, HEAD_DIM), _F32)],
        compiler_params=_params("parallel", "arbitrary", "arbitrary"),
        name="dilated_attention",
    )(*operands)


def _rope_tables(seq_len):
    inv_freq = 1.0 / (ROPE_THETA ** (np.arange(0, HEAD_DIM, 2, dtype=np.float64) / HEAD_DIM))
    ang = np.arange(seq_len, dtype=np.float64)[:, None] * inv_freq[None, :]
    cos, sin = np.cos(ang), np.sin(ang)
    return (jnp.asarray(np.concatenate([cos, cos], axis=-1), _F32),
            jnp.asarray(np.concatenate([-sin, sin], axis=-1), _F32))


def kernel(x, norm_mix, norm_ffn, hgrn_w_in, hgrn_lb_logits, hgrn_out_norm, hgrn_w_out,
           attn_w_qkv, attn_w_out, ffn_w_in, ffn_w_down, final_norm):
    batch, seq, d = x.shape
    cos2, sin2 = _rope_tables(seq)
    h = x.reshape(batch * seq, d)
    bf = lambda w: w.astype(_BF16)

    gated = _hgrn_mixer(h, norm_mix[0], bf(hgrn_w_in[0]), hgrn_lb_logits, hgrn_out_norm[0],
                        batch=batch, layer=0)
    ffn_in, ffn_down = bf(ffn_w_in), bf(ffn_w_down)
    h = _block_tail(gated[None], bf(hgrn_w_out[0]), h, norm_ffn[0], ffn_in, ffn_down, final_norm,
                    layer=0, final_norm=False)

    qkv = _qkv_projection(h, norm_mix[1], bf(attn_w_qkv[0]), cos2, sin2,
                          scale=HEAD_DIM ** -0.5 * LOG2_E)
    attn = _attention(qkv, batch=batch)
    h = _block_tail(attn, bf(attn_w_out[0]), h, norm_ffn[1], ffn_in, ffn_down, final_norm,
                    layer=1, final_norm=True)
    return h.reshape(batch, seq, d)
```

```python
import functools

import numpy as np
import jax
import jax.numpy as jnp
from jax import lax
from jax.experimental import pallas as pl
from jax.experimental.pallas import tpu as pltpu

D_MODEL = 1024
HEAD_DIM = 128
HGRN_HEADS = 8
HGRN_CHUNK = 64
HGRN_STEP_TOKENS = 512
ATTN_HEADS = 12
ATTN_WIDTH = ATTN_HEADS * HEAD_DIM
DILATIONS = (1, 4, 16)
SPAN = 128
HEADS_PER_GROUP = 4
ATTN_BLOCK = SPAN * DILATIONS[-1]
ROW_TILE = 512
COL_TILE = 512
ROPE_THETA = 10000.0
NORM_EPS = 1e-6
NEG_BIG = -1e30
LOG2_E = float(np.log2(np.e))
VMEM_LIMIT_BYTES = 56 * 1024 * 1024

_F32 = jnp.float32
_BF16 = jnp.bfloat16


def _dot(a, b):
    return jnp.dot(a, b, preferred_element_type=_F32)


def _dot_nt(a, b):
    return lax.dot_general(a, b, (((1,), (1,)), ((), ())), preferred_element_type=_F32)


def _dot_tn(a, b):
    return lax.dot_general(a, b, (((0,), (0,)), ((), ())), preferred_element_type=_F32)


def _sigmoid(x):
    return 1.0 / (1.0 + jnp.exp2(x * (-LOG2_E)))


def _rms_scale(x):
    return lax.rsqrt(jnp.mean(x * x, axis=-1, keepdims=True) + NORM_EPS)


def _params(*sem):
    return pltpu.CompilerParams(dimension_semantics=sem, vmem_limit_bytes=VMEM_LIMIT_BYTES)


def _resident(shape, layer=None):
    if layer is None:
        index = (0,) * len(shape)
    else:
        index = (layer,) + (0,) * (len(shape) - 1)
        shape = (None,) + tuple(shape[1:])
    return pl.BlockSpec(shape, lambda *_: index, pipeline_mode=pl.Buffered(1))


def _qkv_kernel(h_ref, gain_ref, w_ref, cos_ref, sin_ref, o_ref, *, scale):
    x = h_ref[...]
    u = (x * _rms_scale(x) * gain_ref[...]).astype(_BF16)
    cos, sin = cos_ref[...], sin_ref[...]
    cos_q, sin_q = cos * scale, sin * scale
    for j in range(w_ref.shape[1] // COL_TILE):
        res = _dot(u, w_ref[:, j * COL_TILE:(j + 1) * COL_TILE])
        for hh in range(COL_TILE // HEAD_DIM):
            lo = j * COL_TILE + hh * HEAD_DIM
            xh = res[:, hh * HEAD_DIM:(hh + 1) * HEAD_DIM]
            if lo < ATTN_WIDTH:
                xh = xh * cos_q + pltpu.roll(xh, HEAD_DIM // 2, 1) * sin_q
            elif lo < 2 * ATTN_WIDTH:
                xh = xh * cos + pltpu.roll(xh, HEAD_DIM // 2, 1) * sin
            o_ref[:, lo:lo + HEAD_DIM] = xh


def _qkv_projection(h, gain, w, cos2, sin2, *, scale):
    t, d = h.shape
    n = w.shape[1]
    tm = ROW_TILE
    seq_tiles = cos2.shape[0] // tm
    return pl.pallas_call(
        functools.partial(_qkv_kernel, scale=scale),
        grid=(t // tm,),
        in_specs=[
            pl.BlockSpec((tm, d), lambda i: (i, 0)),
            _resident((1, d)),
            _resident((d, n)),
            pl.BlockSpec((tm, HEAD_DIM), lambda i: (i % seq_tiles, 0)),
            pl.BlockSpec((tm, HEAD_DIM), lambda i: (i % seq_tiles, 0)),
        ],
        out_specs=pl.BlockSpec((tm, n), lambda i: (i, 0)),
        out_shape=jax.ShapeDtypeStruct((t, n), _F32),
        compiler_params=_params("parallel"),
        name="qkv_projection",
    )(h, gain.reshape(1, d), w, cos2, sin2)


def _hgrn_tables():
    c = HGRN_CHUNK
    t = np.arange(c)
    col = t[None, :]
    row = t[:, None]
    sums = np.zeros((8, c, c), np.float32)
    sums[0] = col <= row
    sums[1] = col > row
    masks = np.zeros((7, c, c), np.float32)
    half = c // 2
    level = 0
    while half >= 1:
        block = t // (2 * half)
        mid = block * 2 * half + half
        is_query = t >= mid
        q_rows = (col >= mid[:, None]) & (col <= row) & is_query[:, None]
        k_rows = (col > row) & (col < mid[:, None]) & (~is_query)[:, None]
        sums[2 + level] = q_rows | k_rows
        masks[level] = ((block[:, None] == block[None, :]) & is_query[:, None]
                        & (~is_query)[None, :])
        half //= 2
        level += 1
    masks[6] = np.eye(c)
    assert level == 6 and np.array_equal(masks.sum(0), np.tril(np.ones((c, c))))
    sums = sums.reshape(8 * c, c)
    return np.concatenate([sums, sums], axis=1), masks


def _hgrn_kernel(h_ref, ngain_ref, w_ref, lbl_ref, gain_ref, sums_ref, masks_ref, o_ref,
                 state_ref, proj_ref, a2_ref, b2_ref, v2_ref, decay_ref, s_ref, *, layer):
    c = HGRN_CHUNK
    width = HGRN_HEADS * HEAD_DIM

    @pl.when(pl.program_id(1) == 0)
    def _():
        state_ref[...] = jnp.zeros_like(state_ref)

    x = h_ref[...]
    u = (x * _rms_scale(x) * ngain_ref[...]).astype(_BF16)
    for j in range(w_ref.shape[1] // COL_TILE):
        cols = slice(j * COL_TILE, (j + 1) * COL_TILE)
        proj_ref[:, cols] = _dot(u, w_ref[:, cols])

    logits = lbl_ref[...]
    e = jnp.exp(logits - jnp.max(logits, axis=0, keepdims=True))
    lb = jnp.sum(e[:layer + 1], axis=0, keepdims=True) / jnp.sum(e, axis=0, keepdims=True)
    out_gain = gain_ref[...]

    n_chunks = h_ref.shape[0] // c

    def chunk_rows(ci):
        return pl.ds(pl.multiple_of(ci * c, c), c)

    def prepare(ci, slot):
        a_ref, b_ref, v_ref = a2_ref.at[slot], b2_ref.at[slot], v2_ref.at[slot]
        rows = chunk_rows(ci)
        q = proj_ref[rows, 0:width]
        forget = lb + (1.0 - lb) * _sigmoid(proj_ref[rows, width:2 * width])
        glog = jnp.log(forget) * LOG2_E
        kk = 1.0 - forget
        qq = q * _sigmoid(q)
        g_hi = glog.astype(_BF16)
        g_lo = (glog - g_hi.astype(_F32)).astype(_BF16)
        factors = jnp.exp2(_dot(sums_ref[...], jnp.concatenate([g_hi, g_lo], axis=0)))

        qq = qq.astype(_BF16)
        kk = kk.astype(_BF16)
        from_start = factors[0:c]
        a_ref[0] = qq * from_start.astype(_BF16)
        decay_ref[slot] = from_start[c - 1:c, :]
        b_ref[0] = kk * factors[c:2 * c].astype(_BF16)
        for level in range(2, 8):
            fac = factors[level * c:(level + 1) * c].astype(_BF16)
            a_ref[level - 1] = qq * fac
            b_ref[level - 1] = kk * fac
        a_ref[7] = qq
        b_ref[7] = kk
        v_ref[...] = proj_ref[rows, 2 * width:3 * width].astype(_BF16)

    def lanes(h):
        return slice(h * HEAD_DIM, (h + 1) * HEAD_DIM)

    def consume(ci, slot):
        a_ref, b_ref, v_ref = a2_ref.at[slot], b2_ref.at[slot], v2_ref.at[slot]
        rows = chunk_rows(ci)
        chunk_decay = decay_ref[slot]
        pairs = [(h, h + 1) for h in range(0, HGRN_HEADS, 2)]
        for pair in pairs:
            scores = [jnp.zeros((c, c), _F32) for _ in pair]
            for level in range(1, 8):
                for i, h in enumerate(pair):
                    scores[i] += masks_ref[level - 1] * _dot_nt(a_ref[level, :, lanes(h)],
                                                                b_ref[level, :, lanes(h)])
            for i, h in enumerate(pair):
                s_ref[h] = scores[i].astype(_BF16)
        for pair in pairs:
            states = [state_ref[h] for h in pair]
            intra = [_dot(s_ref[h], v_ref[:, lanes(h)]) for h in pair]
            inter = [_dot_nt(a_ref[0, :, lanes(h)], states[i].astype(_BF16))
                     for i, h in enumerate(pair)]
            update = [_dot_tn(v_ref[:, lanes(h)], b_ref[0, :, lanes(h)]) for h in pair]
            for i, h in enumerate(pair):
                state_ref[h] = states[i] * chunk_decay[:, lanes(h)] + update[i]
                o = intra[i] + inter[i]
                o = o * _rms_scale(o) * out_gain
                gate = proj_ref[rows, 3 * width + h * HEAD_DIM:3 * width + (h + 1) * HEAD_DIM]
                o_ref[rows, lanes(h)] = (o * (gate * _sigmoid(gate))).astype(o_ref.dtype)

    def two_chunks(i, carry):
        first = 2 * i
        prepare(first + 1, 1)
        consume(first, 0)
        prepare(jnp.minimum(first + 2, n_chunks - 1), 0)
        consume(first + 1, 1)
        return carry

    prepare(0, 0)
    lax.fori_loop(0, n_chunks // 2, two_chunks, 0)


def _hgrn_mixer(h, norm_gain, w_in, lb_logits, out_gain, *, batch, layer):
    t, d = h.shape
    width = HGRN_HEADS * HEAD_DIM
    tc = HGRN_STEP_TOKENS
    steps = t // batch // tc
    sums, masks = _hgrn_tables()
    c = HGRN_CHUNK
    return pl.pallas_call(
        functools.partial(_hgrn_kernel, layer=layer),
        grid=(batch, steps),
        in_specs=[
            pl.BlockSpec((tc, d), lambda b, s: (b * steps + s, 0)),
            _resident((1, d)),
            _resident(w_in.shape),
            _resident(lb_logits.shape),
            _resident((1, HEAD_DIM)),
            _resident(sums.shape),
            _resident(masks.shape),
        ],
        out_specs=pl.BlockSpec((tc, width), lambda b, s: (b * steps + s, 0)),
        out_shape=jax.ShapeDtypeStruct((t, width), _BF16),
        scratch_shapes=[
            pltpu.VMEM((HGRN_HEADS, HEAD_DIM, HEAD_DIM), _F32),
            pltpu.VMEM((tc, 4 * width), _F32),
            pltpu.VMEM((2, 8, c, width), _BF16),
            pltpu.VMEM((2, 8, c, width), _BF16),
            pltpu.VMEM((2, c, width), _BF16),
            pltpu.VMEM((2, 1, width), _F32),
            pltpu.VMEM((HGRN_HEADS, c, c), _BF16),
        ],
        compiler_params=_params("parallel", "arbitrary"),
        name="hgrn_mixer",
    )(h, norm_gain.reshape(1, d), w_in, lb_logits, out_gain.reshape(1, HEAD_DIM),
      jnp.asarray(sums, _BF16), jnp.asarray(masks, _F32))


def _tail_kernel(a_ref, wo_ref, h_ref, gain_ref, wi_ref, wd_ref, fgain_ref, o_ref, u_ref, *,
                 final_norm, ff_tile):
    n_slabs, _, kw = a_ref.shape
    mixed = h_ref[...]
    for s in range(n_slabs):
        mixed += _dot(a_ref[s], wo_ref[s * kw:(s + 1) * kw, :])
    o_ref[...] = mixed
    u_ref[...] = (mixed * _rms_scale(mixed) * gain_ref[...]).astype(_BF16)
    d_ff = wd_ref.shape[0]
    for j in range(d_ff // ff_tile):
        u = u_ref[...]
        gate = _dot(u, wi_ref[:, j * ff_tile:(j + 1) * ff_tile])
        up = _dot(u, wi_ref[:, d_ff + j * ff_tile:d_ff + (j + 1) * ff_tile])
        act = (gate * _sigmoid(gate) * up).astype(_BF16)
        o_ref[...] += _dot(act, wd_ref[j * ff_tile:(j + 1) * ff_tile, :])
    if final_norm:
        y = o_ref[...]
        o_ref[...] = y * _rms_scale(y) * fgain_ref[...]


def _block_tail(a, w_out, h, gain, w_in, w_down, final_gain, *, layer, final_norm, ff_tile=256):
    n_slabs, t, kw = a.shape
    d = h.shape[1]
    tm = ROW_TILE
    return pl.pallas_call(
        functools.partial(_tail_kernel, final_norm=final_norm, ff_tile=ff_tile),
        grid=(t // tm,),
        in_specs=[
            pl.BlockSpec((n_slabs, tm, kw), lambda i: (0, i, 0)),
            _resident(w_out.shape),
            pl.BlockSpec((tm, d), lambda i: (i, 0)),
            _resident((1, d)),
            _resident(w_in.shape, layer=layer),
            _resident(w_down.shape, layer=layer),
            _resident((1, d)),
        ],
        out_specs=pl.BlockSpec((tm, d), lambda i: (i, 0)),
        out_shape=jax.ShapeDtypeStruct((t, d), _F32),
        scratch_shapes=[pltpu.VMEM((tm, d), _BF16)],
        compiler_params=_params("parallel"),
        name="block_tail",
    )(a, w_out, h, gain.reshape(1, d), w_in, w_down, final_gain.reshape(1, d))


def _attn_kernel(*refs):
    ins = refs[:15]
    out_ref, o_scr, l_scr = refs[15:]
    first_block = pl.program_id(1) == 0
    row = lax.broadcasted_iota(jnp.int32, (SPAN, SPAN), 0)
    col = lax.broadcasted_iota(jnp.int32, (SPAN, SPAN), 1)
    bias_cur = jnp.where(col <= row, 0.0, NEG_BIG).astype(_F32)
    bias_prev = jnp.where(col >= row, 0.0, NEG_BIG).astype(_F32)
    bias_halo = bias_prev + jnp.where(first_block, NEG_BIG, 0.0).astype(_F32)
    ones = jnp.ones((SPAN, HEAD_DIM), _BF16)
    chain_len = 4
    body_units = 16

    for g, dil in enumerate(DILATIONS):
        q_ref, k_ref, v_ref, kh_ref, vh_ref = ins[5 * g:5 * g + 5]

        def load(ref, start, dil=dil):
            idx = pl.ds(start, SPAN) if dil == 1 else pl.ds(start, SPAN, stride=dil)
            return ref[idx, :].astype(_BF16)

        def load_kv(kref, vref, start, load=load):
            return load(kref, start), jnp.concatenate([load(vref, start), ones], axis=1)

        def chain(starts, prev, prev_bias, g=g, dil=dil, q_ref=q_ref, k_ref=k_ref, v_ref=v_ref,
                  load=load, load_kv=load_kv):
            for start in starts:
                kp, vp = prev
                kc, vc = cur = load_kv(k_ref, v_ref, start)
                s = _dot_nt(load(q_ref, start), jnp.concatenate([kp, kc], axis=0))
                s = s + jnp.concatenate([prev_bias, bias_cur], axis=1)
                m = jnp.max(jnp.maximum(s[:, :SPAN], s[:, SPAN:]), axis=-1, keepdims=True)
                p = jnp.exp2(s - m).astype(_BF16)
                r = _dot(p, jnp.concatenate([vp, vc], axis=0))
                denom = r[:, HEAD_DIM:]
                idx = pl.ds(start, SPAN) if dil == 1 else pl.ds(start, SPAN, stride=dil)
                o_scr[g, idx, :] = r[:, :HEAD_DIM] / denom
                l_scr[g, idx, :] = m + jnp.log2(denom)
                prev, prev_bias = cur, bias_prev

        block_rows = SPAN * dil
        if dil == DILATIONS[-1]:
            def body(i, carry, chain=chain, load_kv=load_kv, kh_ref=kh_ref, vh_ref=vh_ref):
                for u in range(body_units):
                    r = i * body_units + u
                    chain([r], load_kv(kh_ref, vh_ref, r), bias_halo)
                return carry
            lax.fori_loop(0, dil // body_units, body, 0)
        elif dil > 1:
            assert DILATIONS[-1] // dil == chain_len

            def body(i, carry, chain=chain, load_kv=load_kv, kh_ref=kh_ref, vh_ref=vh_ref,
                     block_rows=block_rows):
                for u in range(body_units // chain_len):
                    r = i * (body_units // chain_len) + u
                    chain([r + b * block_rows for b in range(chain_len)],
                          load_kv(kh_ref, vh_ref, r), bias_halo)
                return carry
            lax.fori_loop(0, dil * chain_len // body_units, body, 0)
        else:
            for first in range(0, DILATIONS[-1], chain_len):
                prev = (load_kv(kh_ref, vh_ref, 0) if first == 0
                        else load_kv(k_ref, v_ref, (first - 1) * block_rows))
                chain([(first + b) * block_rows for b in range(chain_len)], prev,
                      bias_halo if first == 0 else bias_prev)

    merge_rows = 256

    def merge(ci, carry):
        rr = pl.ds(pl.multiple_of(ci * merge_rows, merge_rows), merge_rows)
        l0, l1, l2 = l_scr[0, rr, :], l_scr[1, rr, :], l_scr[2, rr, :]
        m = jnp.maximum(jnp.maximum(l0, l1), l2)
        e0, e1, e2 = jnp.exp2(l0 - m), jnp.exp2(l1 - m), jnp.exp2(l2 - m)
        inv = 1.0 / (e0 + e1 + e2)
        out_ref[0, rr, :] = (o_scr[0, rr, :] * (e0 * inv)).astype(out_ref.dtype)
        out_ref[1, rr, :] = (o_scr[1, rr, :] * (e1 * inv)).astype(out_ref.dtype)
        out_ref[2, rr, :] = (o_scr[2, rr, :] * (e2 * inv)).astype(out_ref.dtype)
        return carry

    lax.fori_loop(0, ATTN_BLOCK // merge_rows, merge, 0)


def _attention(qkv, *, batch):
    t = qkv.shape[0]
    tb = ATTN_BLOCK
    steps = t // batch // tb
    in_specs, operands = [], []
    for g, dil in enumerate(DILATIONS):
        halo = SPAN * dil
        ratio = tb // halo

        def cur(which, g=g):
            return pl.BlockSpec(
                (tb, HEAD_DIM),
                lambda b, i, j: (b * steps + i, which * ATTN_HEADS + g * HEADS_PER_GROUP + j))

        def prev(which, g=g, ratio=ratio, halo=halo):
            return pl.BlockSpec(
                (halo, HEAD_DIM),
                lambda b, i, j: (jnp.maximum((b * steps + i) * ratio - 1, 0),
                                 which * ATTN_HEADS + g * HEADS_PER_GROUP + j))

        in_specs += [cur(0), cur(1), cur(2), prev(1), prev(2)]
        operands += [qkv] * 5
    n_groups = len(DILATIONS)
    return pl.pallas_call(
        _attn_kernel,
        grid=(batch, steps, HEADS_PER_GROUP),
        in_specs=in_specs,
        out_specs=pl.BlockSpec((n_groups, tb, HEAD_DIM), lambda b, i, j: (0, b * steps + i, j)),
        out_shape=jax.ShapeDtypeStruct((n_groups, t, HEADS_PER_GROUP * HEAD_DIM), _BF16),
        scratch_shapes=[pltpu.VMEM((3, tb, HEAD_DIM), _F32), pltpu.VMEM((3, tb, HEAD_DIM), _F32)],
        compiler_params=_params("parallel", "arbitrary", "arbitrary"),
        name="dilated_attention",
    )(*operands)


def _rope_tables(seq_len):
    inv_freq = 1.0 / (ROPE_THETA ** (np.arange(0, HEAD_DIM, 2, dtype=np.float64) / HEAD_DIM))
    ang = np.arange(seq_len, dtype=np.float64)[:, None] * inv_freq[None, :]
    cos, sin = np.cos(ang), np.sin(ang)
    return (jnp.asarray(np.concatenate([cos, cos], axis=-1), _F32),
            jnp.asarray(np.concatenate([-sin, sin], axis=-1), _F32))


def kernel(x, norm_mix, norm_ffn, hgrn_w_in, hgrn_lb_logits, hgrn_out_norm, hgrn_w_out,
           attn_w_qkv, attn_w_out, ffn_w_in, ffn_w_down, final_norm):
    batch, seq, d = x.shape
    cos2, sin2 = _rope_tables(seq)
    h = x.reshape(batch * seq, d)
    bf = lambda w: w.astype(_BF16)

    gated = _hgrn_mixer(h, norm_mix[0], bf(hgrn_w_in[0]), hgrn_lb_logits, hgrn_out_norm[0],
                        batch=batch, layer=0)
    ffn_in, ffn_down = bf(ffn_w_in), bf(ffn_w_down)
    h = _block_tail(gated[None], bf(hgrn_w_out[0]), h, norm_ffn[0], ffn_in, ffn_down, final_norm,
                    layer=0, final_norm=False)

    qkv = _qkv_projection(h, norm_mix[1], bf(attn_w_qkv[0]), cos2, sin2,
                          scale=HEAD_DIM ** -0.5 * LOG2_E)
    attn = _attention(qkv, batch=batch)
    h = _block_tail(attn, bf(attn_w_out[0]), h, norm_ffn[1], ffn_in, ffn_down, final_norm,
                    layer=1, final_norm=True)
    return h.reshape(batch, seq, d)
```

```python
import functools

import numpy as np
import jax
import jax.numpy as jnp
from jax import lax
from jax.experimental import pallas as pl
from jax.experimental.pallas import tpu as pltpu

D_MODEL = 1024
HEAD_DIM = 128
HGRN_HEADS = 8
HGRN_CHUNK = 64
HGRN_STEP_TOKENS = 512
ATTN_HEADS = 12
ATTN_WIDTH = ATTN_HEADS * HEAD_DIM
DILATIONS = (1, 4, 16)
SPAN = 128
HEADS_PER_GROUP = 4
ATTN_BLOCK = SPAN * DILATIONS[-1]
ROW_TILE = 512
COL_TILE = 512
ROPE_THETA = 10000.0
NORM_EPS = 1e-6
NEG_BIG = -1e30
LOG2_E = float(np.log2(np.e))
VMEM_LIMIT_BYTES = 56 * 1024 * 1024

_F32 = jnp.float32
_BF16 = jnp.bfloat16


def _dot(a, b):
    return lax.dot_general(a, b, (((1,), (0,)), ((), ())), preferred_element_type=_F32)


def _dot_nt(a, b):
    return lax.dot_general(a, b, (((1,), (1,)), ((), ())), preferred_element_type=_F32)


def _dot_tn(a, b):
    return lax.dot_general(a, b, (((0,), (0,)), ((), ())), preferred_element_type=_F32)


def _sigmoid(x):
    return 1.0 / (1.0 + jnp.exp2(x * (-LOG2_E)))


def _rms_scale(x):
    return lax.rsqrt(jnp.mean(x * x, axis=-1, keepdims=True) + NORM_EPS)


def _params(*sem):
    return pltpu.CompilerParams(dimension_semantics=sem, vmem_limit_bytes=VMEM_LIMIT_BYTES)


def _resident(shape, layer=None):
    if layer is None:
        index = (0,) * len(shape)
    else:
        index = (layer,) + (0,) * (len(shape) - 1)
        shape = (None,) + tuple(shape[1:])
    return pl.BlockSpec(shape, lambda *_: index, pipeline_mode=pl.Buffered(1))


def _qkv_kernel(h_ref, gain_ref, w_ref, cos_ref, sin_ref, o_ref, *, scale):
    x = h_ref[...]
    u = (x * _rms_scale(x) * gain_ref[...]).astype(_BF16)
    cos, sin = cos_ref[...], sin_ref[...]
    cos_q, sin_q = cos * scale, sin * scale
    for j in range(w_ref.shape[1] // COL_TILE):
        res = _dot(u, w_ref[:, j * COL_TILE:(j + 1) * COL_TILE])
        for hh in range(COL_TILE // HEAD_DIM):
            lo = j * COL_TILE + hh * HEAD_DIM
            xh = res[:, hh * HEAD_DIM:(hh + 1) * HEAD_DIM]
            if lo < ATTN_WIDTH:
                xh = xh * cos_q + pltpu.roll(xh, HEAD_DIM // 2, 1) * sin_q
            elif lo < 2 * ATTN_WIDTH:
                xh = xh * cos + pltpu.roll(xh, HEAD_DIM // 2, 1) * sin
            o_ref[:, lo:lo + HEAD_DIM] = xh


def _qkv_projection(h, gain, w, cos2, sin2, *, scale):
    t, d = h.shape
    n = w.shape[1]
    tm = ROW_TILE
    seq_tiles = cos2.shape[0] // tm
    return pl.pallas_call(
        functools.partial(_qkv_kernel, scale=scale),
        grid=(t // tm,),
        in_specs=[
            pl.BlockSpec((tm, d), lambda i: (i, 0)),
            _resident((1, d)),
            _resident((d, n)),
            pl.BlockSpec((tm, HEAD_DIM), lambda i: (i % seq_tiles, 0)),
            pl.BlockSpec((tm, HEAD_DIM), lambda i: (i % seq_tiles, 0)),
        ],
        out_specs=pl.BlockSpec((tm, n), lambda i: (i, 0)),
        out_shape=jax.ShapeDtypeStruct((t, n), _F32),
        compiler_params=_params("parallel"),
        name="qkv_projection",
    )(h, gain.reshape(1, d), w, cos2, sin2)


def _hgrn_tables():
    c = HGRN_CHUNK
    t = np.arange(c)
    col = t[None, :]
    row = t[:, None]
    sums = np.zeros((8, c, c), np.float32)
    sums[0] = col <= row
    sums[1] = col > row
    masks = np.zeros((7, c, c), np.float32)
    half = c // 2
    level = 0
    while half >= 1:
        block = t // (2 * half)
        mid = block * 2 * half + half
        is_query = t >= mid
        q_rows = (col >= mid[:, None]) & (col <= row) & is_query[:, None]
        k_rows = (col > row) & (col < mid[:, None]) & (~is_query)[:, None]
        sums[2 + level] = q_rows | k_rows
        masks[level] = ((block[:, None] == block[None, :]) & is_query[:, None]
                        & (~is_query)[None, :])
        half //= 2
        level += 1
    masks[6] = np.eye(c)
    assert level == 6 and np.array_equal(masks.sum(0), np.tril(np.ones((c, c))))
    sums = sums.reshape(8 * c, c)
    return np.concatenate([sums, sums], axis=1), masks


def _hgrn_kernel(h_ref, ngain_ref, w_ref, lbl_ref, gain_ref, sums_ref, masks_ref, o_ref,
                 state_ref, proj_ref, a2_ref, b2_ref, v2_ref, decay_ref, s_ref, *, layer):
    c = HGRN_CHUNK
    width = HGRN_HEADS * HEAD_DIM

    @pl.when(pl.program_id(1) == 0)
    def _():
        state_ref[...] = jnp.zeros_like(state_ref)

    x = h_ref[...]
    u = (x * _rms_scale(x) * ngain_ref[...]).astype(_BF16)
    for j in range(w_ref.shape[1] // COL_TILE):
        cols = slice(j * COL_TILE, (j + 1) * COL_TILE)
        proj_ref[:, cols] = _dot(u, w_ref[:, cols])

    logits = lbl_ref[...]
    e = jnp.exp(logits - jnp.max(logits, axis=0, keepdims=True))
    lb = jnp.sum(e[:layer + 1], axis=0, keepdims=True) / jnp.sum(e, axis=0, keepdims=True)
    out_gain = gain_ref[...]

    n_chunks = h_ref.shape[0] // c

    def chunk_rows(ci):
        return pl.ds(pl.multiple_of(ci * c, c), c)

    def prepare(ci, slot):
        a_ref, b_ref, v_ref = a2_ref.at[slot], b2_ref.at[slot], v2_ref.at[slot]
        rows = chunk_rows(ci)
        q = proj_ref[rows, 0:width]
        forget = lb + (1.0 - lb) * _sigmoid(proj_ref[rows, width:2 * width])
        glog = jnp.log(forget) * LOG2_E
        kk = 1.0 - forget
        qq = q * _sigmoid(q)
        g_hi = glog.astype(_BF16)
        g_lo = (glog - g_hi.astype(_F32)).astype(_BF16)
        factors = jnp.exp2(_dot(sums_ref[...], jnp.concatenate([g_hi, g_lo], axis=0)))

        qq = qq.astype(_BF16)
        kk = kk.astype(_BF16)
        from_start = factors[0:c]
        a_ref[0] = qq * from_start.astype(_BF16)
        decay_ref[slot] = from_start[c - 1:c, :]
        b_ref[0] = kk * factors[c:2 * c].astype(_BF16)
        for level in range(2, 8):
            fac = factors[level * c:(level + 1) * c].astype(_BF16)
            a_ref[level - 1] = qq * fac
            b_ref[level - 1] = kk * fac
        a_ref[7] = qq
        b_ref[7] = kk
        v_ref[...] = proj_ref[rows, 2 * width:3 * width].astype(_BF16)

    def lanes(h):
        return slice(h * HEAD_DIM, (h + 1) * HEAD_DIM)

    def consume(ci, slot):
        a_ref, b_ref, v_ref = a2_ref.at[slot], b2_ref.at[slot], v2_ref.at[slot]
        rows = chunk_rows(ci)
        chunk_decay = decay_ref[slot]
        pairs = [(h, h + 1) for h in range(0, HGRN_HEADS, 2)]
        for pair in pairs:
            scores = [jnp.zeros((c, c), _F32) for _ in pair]
            for level in range(1, 8):
                for i, h in enumerate(pair):
                    scores[i] += masks_ref[level - 1] * _dot_nt(a_ref[level, :, lanes(h)],
                                                                b_ref[level, :, lanes(h)])
            for i, h in enumerate(pair):
                s_ref[h] = scores[i].astype(_BF16)
        for pair in pairs:
            states = [state_ref[h] for h in pair]
            intra = [_dot(s_ref[h], v_ref[:, lanes(h)]) for h in pair]
            inter = [_dot_nt(a_ref[0, :, lanes(h)], states[i].astype(_BF16))
                     for i, h in enumerate(pair)]
            update = [_dot_tn(v_ref[:, lanes(h)], b_ref[0, :, lanes(h)]) for h in pair]
            for i, h in enumerate(pair):
                state_ref[h] = states[i] * chunk_decay[:, lanes(h)] + update[i]
                o = intra[i] + inter[i]
                o = o * _rms_scale(o) * out_gain
                gate = proj_ref[rows, 3 * width + h * HEAD_DIM:3 * width + (h + 1) * HEAD_DIM]
                o_ref[rows, lanes(h)] = (o * (gate * _sigmoid(gate))).astype(o_ref.dtype)

    def two_chunks(i, carry):
        first = 2 * i
        prepare(first + 1, 1)
        consume(first, 0)
        prepare(jnp.minimum(first + 2, n_chunks - 1), 0)
        consume(first + 1, 1)
        return carry

    prepare(0, 0)
    lax.fori_loop(0, n_chunks // 2, two_chunks, 0)


def _hgrn_mixer(h, norm_gain, w_in, lb_logits, out_gain, *, batch, layer):
    t, d = h.shape
    width = HGRN_HEADS * HEAD_DIM
    tc = HGRN_STEP_TOKENS
    steps = t // batch // tc
    sums, masks = _hgrn_tables()
    c = HGRN_CHUNK
    return pl.pallas_call(
        functools.partial(_hgrn_kernel, layer=layer),
        grid=(batch, steps),
        in_specs=[
            pl.BlockSpec((tc, d), lambda b, s: (b * steps + s, 0)),
            _resident((1, d)),
            _resident(w_in.shape),
            _resident(lb_logits.shape),
            _resident((1, HEAD_DIM)),
            _resident(sums.shape),
            _resident(masks.shape),
        ],
        out_specs=pl.BlockSpec((tc, width), lambda b, s: (b * steps + s, 0)),
        out_shape=jax.ShapeDtypeStruct((t, width), _BF16),
        scratch_shapes=[
            pltpu.VMEM((HGRN_HEADS, HEAD_DIM, HEAD_DIM), _F32),
            pltpu.VMEM((tc, 4 * width), _F32),
            pltpu.VMEM((2, 8, c, width), _BF16),
            pltpu.VMEM((2, 8, c, width), _BF16),
            pltpu.VMEM((2, c, width), _BF16),
            pltpu.VMEM((2, 1, width), _F32),
            pltpu.VMEM((HGRN_HEADS, c, c), _BF16),
        ],
        compiler_params=_params("parallel", "arbitrary"),
        name="hgrn_mixer",
    )(h, norm_gain.reshape(1, d), w_in, lb_logits, out_gain.reshape(1, HEAD_DIM),
      jnp.asarray(sums, _BF16), jnp.asarray(masks, _F32))


def _tail_kernel(a_ref, wo_ref, h_ref, gain_ref, wi_ref, wd_ref, fgain_ref, o_ref, u_ref, *,
                 final_norm, ff_tile):
    n_slabs, _, kw = a_ref.shape
    mixed = h_ref[...]
    for s in range(n_slabs):
        mixed += _dot(a_ref[s], wo_ref[s * kw:(s + 1) * kw, :])
    o_ref[...] = mixed
    u_ref[...] = (mixed * _rms_scale(mixed) * gain_ref[...]).astype(_BF16)
    d_ff = wd_ref.shape[0]
    for j in range(d_ff // ff_tile):
        u = u_ref[...]
        gate = _dot(u, wi_ref[:, j * ff_tile:(j + 1) * ff_tile])
        up = _dot(u, wi_ref[:, d_ff + j * ff_tile:d_ff + (j + 1) * ff_tile])
        act = (gate * _sigmoid(gate) * up).astype(_BF16)
        o_ref[...] += _dot(act, wd_ref[j * ff_tile:(j + 1) * ff_tile, :])
    if final_norm:
        y = o_ref[...]
        o_ref[...] = y * _rms_scale(y) * fgain_ref[...]


def _block_tail(a, w_out, h, gain, w_in, w_down, final_gain, *, layer, final_norm, ff_tile=256):
    n_slabs, t, kw = a.shape
    d = h.shape[1]
    tm = ROW_TILE
    return pl.pallas_call(
        functools.partial(_tail_kernel, final_norm=final_norm, ff_tile=ff_tile),
        grid=(t // tm,),
        in_specs=[
            pl.BlockSpec((n_slabs, tm, kw), lambda i: (0, i, 0)),
            _resident(w_out.shape),
            pl.BlockSpec((tm, d), lambda i: (i, 0)),
            _resident((1, d)),
            _resident(w_in.shape, layer=layer),
            _resident(w_down.shape, layer=layer),
            _resident((1, d)),
        ],
        out_specs=pl.BlockSpec((tm, d), lambda i: (i, 0)),
        out_shape=jax.ShapeDtypeStruct((t, d), _F32),
        scratch_shapes=[pltpu.VMEM((tm, d), _BF16)],
        compiler_params=_params("parallel"),
        name="block_tail",
    )(a, w_out, h, gain.reshape(1, d), w_in, w_down, final_gain.reshape(1, d))


def _attn_kernel(*refs):
    ins = refs[:15]
    out_ref, o_scr, l_scr = refs[15:]
    first_block = pl.program_id(1) == 0
    row = lax.broadcasted_iota(jnp.int32, (SPAN, SPAN), 0)
    col = lax.broadcasted_iota(jnp.int32, (SPAN, SPAN), 1)
    bias_cur = jnp.where(col <= row, 0.0, NEG_BIG).astype(_F32)
    bias_prev = jnp.where(col >= row, 0.0, NEG_BIG).astype(_F32)
    bias_halo = bias_prev + jnp.where(first_block, NEG_BIG, 0.0).astype(_F32)
    ones = jnp.ones((SPAN, HEAD_DIM), _BF16)
    chain_len = 4
    body_units = 16

    for g, dil in enumerate(DILATIONS):
        q_ref, k_ref, v_ref, kh_ref, vh_ref = ins[5 * g:5 * g + 5]

        def load(ref, start, dil=dil):
            idx = pl.ds(start, SPAN) if dil == 1 else pl.ds(start, SPAN, stride=dil)
            return ref[idx, :].astype(_BF16)

        def load_kv(kref, vref, start, load=load):
            return load(kref, start), jnp.concatenate([load(vref, start), ones], axis=1)

        def chain(starts, prev, prev_bias, g=g, dil=dil, q_ref=q_ref, k_ref=k_ref, v_ref=v_ref,
                  load=load, load_kv=load_kv):
            for start in starts:
                kp, vp = prev
                kc, vc = cur = load_kv(k_ref, v_ref, start)
                s = _dot_nt(load(q_ref, start), jnp.concatenate([kp, kc], axis=0))
                s = s + jnp.concatenate([prev_bias, bias_cur], axis=1)
                m = jnp.max(jnp.maximum(s[:, :SPAN], s[:, SPAN:]), axis=-1, keepdims=True)
                p = jnp.exp2(s - m).astype(_BF16)
                r = _dot(p, jnp.concatenate([vp, vc], axis=0))
                denom = r[:, HEAD_DIM:]
                idx = pl.ds(start, SPAN) if dil == 1 else pl.ds(start, SPAN, stride=dil)
                o_scr[g, idx, :] = r[:, :HEAD_DIM] / denom
                l_scr[g, idx, :] = m + jnp.log2(denom)
                prev, prev_bias = cur, bias_prev

        block_rows = SPAN * dil
        if dil == DILATIONS[-1]:
            def body(i, carry, chain=chain, load_kv=load_kv, kh_ref=kh_ref, vh_ref=vh_ref):
                for u in range(body_units):
                    r = i * body_units + u
                    chain([r], load_kv(kh_ref, vh_ref, r), bias_halo)
                return carry
            lax.fori_loop(0, dil // body_units, body, 0)
        elif dil > 1:
            assert DILATIONS[-1] // dil == chain_len

            def body(i, carry, chain=chain, load_kv=load_kv, kh_ref=kh_ref, vh_ref=vh_ref,
                     block_rows=block_rows):
                for u in range(body_units // chain_len):
                    r = i * (body_units // chain_len) + u
                    chain([r + b * block_rows for b in range(chain_len)],
                          load_kv(kh_ref, vh_ref, r), bias_halo)
                return carry
            lax.fori_loop(0, dil * chain_len // body_units, body, 0)
        else:
            for first in range(0, DILATIONS[-1], chain_len):
                prev = (load_kv(kh_ref, vh_ref, 0) if first == 0
                        else load_kv(k_ref, v_ref, (first - 1) * block_rows))
                chain([(first + b) * block_rows for b in range(chain_len)], prev,
                      bias_halo if first == 0 else bias_prev)

    merge_rows = 256

    def merge(ci, carry):
        rr = pl.ds(pl.multiple_of(ci * merge_rows, merge_rows), merge_rows)
        l0, l1, l2 = l_scr[0, rr, :], l_scr[1, rr, :], l_scr[2, rr, :]
        m = jnp.maximum(jnp.maximum(l0, l1), l2)
        e0, e1, e2 = jnp.exp2(l0 - m), jnp.exp2(l1 - m), jnp.exp2(l2 - m)
        inv = 1.0 / (e0 + e1 + e2)
        out_ref[0, rr, :] = (o_scr[0, rr, :] * (e0 * inv)).astype(out_ref.dtype)
        out_ref[1, rr, :] = (o_scr[1, rr, :] * (e1 * inv)).astype(out_ref.dtype)
        out_ref[2, rr, :] = (o_scr[2, rr, :] * (e2 * inv)).astype(out_ref.dtype)
        return carry

    lax.fori_loop(0, ATTN_BLOCK // merge_rows, merge, 0)


def _attention(qkv, *, batch):
    t = qkv.shape[0]
    tb = ATTN_BLOCK
    steps = t // batch // tb
    in_specs, operands = [], []
    for g, dil in enumerate(DILATIONS):
        halo = SPAN * dil
        ratio = tb // halo

        def cur(which, g=g):
            return pl.BlockSpec(
                (tb, HEAD_DIM),
                lambda b, i, j: (b * steps + i, which * ATTN_HEADS + g * HEADS_PER_GROUP + j))

        def prev(which, g=g, ratio=ratio, halo=halo):
            return pl.BlockSpec(
                (halo, HEAD_DIM),
                lambda b, i, j: (jnp.maximum((b * steps + i) * ratio - 1, 0),
                                 which * ATTN_HEADS + g * HEADS_PER_GROUP + j))

        in_specs += [cur(0), cur(1), cur(2), prev(1), prev(2)]
        operands += [qkv] * 5
    n_groups = len(DILATIONS)
    return pl.pallas_call(
        _attn_kernel,
        grid=(batch, steps, HEADS_PER_GROUP),
        in_specs=in_specs,
        out_specs=pl.BlockSpec((n_groups, tb, HEAD_DIM), lambda b, i, j: (0, b * steps + i, j)),
        out_shape=jax.ShapeDtypeStruct((n_groups, t, HEADS_PER_GROUP * HEAD_DIM), _BF16),
        scratch_shapes=[pltpu.VMEM((3, tb, HEAD_DIM), _F32), pltpu.VMEM((3, tb, HEAD_DIM), _F32)],
        compiler_params=_params("parallel", "arbitrary", "arbitrary"),
        name="dilated_attention",
    )(*operands)


def _rope_tables(seq_len):
    inv_freq = 1.0 / (ROPE_THETA ** (np.arange(0, HEAD_DIM, 2, dtype=np.float64) / HEAD_DIM))
    ang = np.arange(seq_len, dtype=np.float64)[:, None] * inv_freq[None, :]
    cos, sin = np.cos(ang), np.sin(ang)
    return (jnp.asarray(np.concatenate([cos, cos], axis=-1), _F32),
            jnp.asarray(np.concatenate([-sin, sin], axis=-1), _F32))


def kernel(x, norm_mix, norm_ffn, hgrn_w_in, hgrn_lb_logits, hgrn_out_norm, hgrn_w_out,
           attn_w_qkv, attn_w_out, ffn_w_in, ffn_w_down, final_norm):
    batch, seq, d = x.shape
    cos2, sin2 = _rope_tables(seq)
    h = x.reshape(batch * seq, d)

    gated = _hgrn_mixer(h, norm_mix[0], hgrn_w_in[0], hgrn_lb_logits, hgrn_out_norm[0],
                        batch=batch, layer=0)
    ffn_in, ffn_down = ffn_w_in, ffn_w_down
    h = _block_tail(gated[None], hgrn_w_out[0], h, norm_ffn[0], ffn_in, ffn_down, final_norm,
                    layer=0, final_norm=False)

    qkv = _qkv_projection(h, norm_mix[1], attn_w_qkv[0], cos2, sin2,
                          scale=HEAD_DIM ** -0.5 * LOG2_E)
    attn = _attention(qkv, batch=batch)
    h = _block_tail(attn, attn_w_out[0], h, norm_ffn[1], ffn_in, ffn_down, final_norm,
                    layer=1, final_norm=True)
    return h.reshape(batch, seq, d)
```

```python
import functools

import numpy as np
import jax
import jax.numpy as jnp
from jax import lax
from jax.experimental import pallas as pl
from jax.experimental.pallas import tpu as pltpu

D_MODEL = 1024
HEAD_DIM = 128
HGRN_HEADS = 8
HGRN_CHUNK = 64
HGRN_STEP_TOKENS = 512
ATTN_HEADS = 12
ATTN_WIDTH = ATTN_HEADS * HEAD_DIM
DILATIONS = (1, 4, 16)
SPAN = 128
HEADS_PER_GROUP = 4
ATTN_BLOCK = SPAN * DILATIONS[-1]
ROW_TILE = 512
COL_TILE = 512
ROPE_THETA = 10000.0
NORM_EPS = 1e-6
NEG_BIG = -1e30
LOG2_E = float(np.log2(np.e))
VMEM_LIMIT_BYTES = 56 * 1024 * 1024

_F32 = jnp.float32
_BF16 = jnp.bfloat16


def _dot(a, b):
    return lax.dot_general(a, b, (((1,), (0,)), ((), ())), preferred_element_type=_F32)


def _dot_nt(a, b):
    return lax.dot_general(a, b, (((1,), (1,)), ((), ())), preferred_element_type=_F32)


def _dot_tn(a, b):
    return lax.dot_general(a, b, (((0,), (0,)), ((), ())), preferred_element_type=_F32)


def _sigmoid(x):
    return 1.0 / (1.0 + jnp.exp2(x * (-LOG2_E)))


def _rms_scale(x):
    return lax.rsqrt(jnp.mean(x * x, axis=-1, keepdims=True) + NORM_EPS)


def _params(*sem):
    return pltpu.CompilerParams(dimension_semantics=sem, vmem_limit_bytes=VMEM_LIMIT_BYTES)


def _resident(shape, layer=None):
    if layer is None:
        index = (0,) * len(shape)
    else:
        index = (layer,) + (0,) * (len(shape) - 1)
        shape = (None,) + tuple(shape[1:])
    return pl.BlockSpec(shape, lambda *_: index, pipeline_mode=pl.Buffered(1))


def _qkv_kernel(h_ref, gain_ref, w_ref, cos_ref, sin_ref, o_ref, *, scale):
    x = h_ref[...]
    u = (x * _rms_scale(x) * gain_ref[...]).astype(_BF16)
    cos, sin = cos_ref[...], sin_ref[...]
    cos_q, sin_q = cos * scale, sin * scale
    for j in range(w_ref.shape[1] // COL_TILE):
        res = _dot(u, w_ref[:, j * COL_TILE:(j + 1) * COL_TILE])
        for hh in range(COL_TILE // HEAD_DIM):
            lo = j * COL_TILE + hh * HEAD_DIM
            xh = res[:, hh * HEAD_DIM:(hh + 1) * HEAD_DIM]
            if lo < ATTN_WIDTH:
                xh = xh * cos_q + pltpu.roll(xh, HEAD_DIM // 2, 1) * sin_q
            elif lo < 2 * ATTN_WIDTH:
                xh = xh * cos + pltpu.roll(xh, HEAD_DIM // 2, 1) * sin
            o_ref[:, lo:lo + HEAD_DIM] = xh


def _qkv_projection(h, gain, w, cos2, sin2, *, scale):
    t, d = h.shape
    n = w.shape[1]
    tm = ROW_TILE
    seq_tiles = cos2.shape[0] // tm
    return pl.pallas_call(
        functools.partial(_qkv_kernel, scale=scale),
        grid=(t // tm,),
        in_specs=[
            pl.BlockSpec((tm, d), lambda i: (i, 0)),
            _resident((1, d)),
            _resident((d, n)),
            pl.BlockSpec((tm, HEAD_DIM), lambda i: (i % seq_tiles, 0)),
            pl.BlockSpec((tm, HEAD_DIM), lambda i: (i % seq_tiles, 0)),
        ],
        out_specs=pl.BlockSpec((tm, n), lambda i: (i, 0)),
        out_shape=jax.ShapeDtypeStruct((t, n), _F32),
        compiler_params=_params("parallel"),
        name="qkv_projection",
    )(h, gain.reshape(1, d), w, cos2, sin2)


def _hgrn_tables():
    c = HGRN_CHUNK
    t = np.arange(c)
    col = t[None, :]
    row = t[:, None]
    sums = np.zeros((8, c, c), np.float32)
    sums[0] = col <= row
    sums[1] = col > row
    masks = np.zeros((7, c, c), np.float32)
    half = c // 2
    level = 0
    while half >= 1:
        block = t // (2 * half)
        mid = block * 2 * half + half
        is_query = t >= mid
        q_rows = (col >= mid[:, None]) & (col <= row) & is_query[:, None]
        k_rows = (col > row) & (col < mid[:, None]) & (~is_query)[:, None]
        sums[2 + level] = q_rows | k_rows
        masks[level] = ((block[:, None] == block[None, :]) & is_query[:, None]
                        & (~is_query)[None, :])
        half //= 2
        level += 1
    masks[6] = np.eye(c)
    assert level == 6 and np.array_equal(masks.sum(0), np.tril(np.ones((c, c))))
    sums = sums.reshape(8 * c, c)
    return np.concatenate([sums, sums], axis=1), masks


def _hgrn_kernel(h_ref, ngain_ref, w_ref, lbl_ref, gain_ref, sums_ref, masks_ref, pmasks_ref,
                 o_ref, state_ref, proj_ref, a2_ref, b2_ref, v2_ref, decay_ref, s_ref, bt2_ref,
                 *, layer):
    c = HGRN_CHUNK
    width = HGRN_HEADS * HEAD_DIM

    @pl.when(pl.program_id(1) == 0)
    def _():
        state_ref[...] = jnp.zeros_like(state_ref)

    x = h_ref[...]
    u = (x * _rms_scale(x) * ngain_ref[...]).astype(_BF16)
    for j in range(w_ref.shape[1] // COL_TILE):
        cols = slice(j * COL_TILE, (j + 1) * COL_TILE)
        proj_ref[:, cols] = _dot(u, w_ref[:, cols])

    logits = lbl_ref[...]
    e = jnp.exp(logits - jnp.max(logits, axis=0, keepdims=True))
    lb = jnp.sum(e[:layer + 1], axis=0, keepdims=True) / jnp.sum(e, axis=0, keepdims=True)
    out_gain = gain_ref[...]

    n_chunks = h_ref.shape[0] // c

    def chunk_rows(ci):
        return pl.ds(pl.multiple_of(ci * c, c), c)

    def prepare(ci, slot):
        a_ref, b_ref, v_ref = a2_ref.at[slot], b2_ref.at[slot], v2_ref.at[slot]
        rows = chunk_rows(ci)
        q = proj_ref[rows, 0:width]
        forget = lb + (1.0 - lb) * _sigmoid(proj_ref[rows, width:2 * width])
        glog = jnp.log(forget) * LOG2_E
        kk = 1.0 - forget
        qq = q * _sigmoid(q)
        g_hi = glog.astype(_BF16)
        g_lo = (glog - g_hi.astype(_F32)).astype(_BF16)
        factors = jnp.exp2(_dot(sums_ref[...], jnp.concatenate([g_hi, g_lo], axis=0)))

        qq = qq.astype(_BF16)
        kk = kk.astype(_BF16)
        from_start = factors[0:c]
        a_ref[0] = qq * from_start.astype(_BF16)
        decay_ref[slot] = from_start[c - 1:c, :]
        b_ref[0] = kk * factors[c:2 * c].astype(_BF16)
        key_side = []
        for level in range(2, 8):
            fac = factors[level * c:(level + 1) * c].astype(_BF16)
            a_ref[level - 1] = qq * fac
            key_side.append(kk * fac)
        a_ref[7] = qq
        b_ref[7] = kk
        bt_ref = bt2_ref.at[slot]
        for pair in range(3):
            both = jnp.concatenate(key_side[2 * pair:2 * pair + 2], axis=0)
            for h in range(HGRN_HEADS):
                bt_ref[h, pair] = both[:, h * HEAD_DIM:(h + 1) * HEAD_DIM].T
        v = proj_ref[rows, 2 * width:3 * width].astype(_BF16)
        v_ref[...] = jnp.concatenate([v, v], axis=0)

    def lanes(h):
        return slice(h * HEAD_DIM, (h + 1) * HEAD_DIM)

    def consume(ci, slot):
        a_ref, b_ref, v_ref = a2_ref.at[slot], b2_ref.at[slot], v2_ref.at[slot]
        rows = chunk_rows(ci)
        chunk_decay = decay_ref[slot]
        bt_ref = bt2_ref.at[slot]
        left = lax.broadcasted_iota(jnp.int32, (c, 2 * c), 1) < c
        pairs = [(h, h + 1) for h in range(0, HGRN_HEADS, 2)]
        for pair in pairs:
            scores = [jnp.zeros((c, 2 * c), _F32) for _ in pair]
            for lp in range(3):
                for i, h in enumerate(pair):
                    prod = _dot(a_ref[1 + 2 * lp:3 + 2 * lp, :, lanes(h)].reshape(2 * c, HEAD_DIM),
                                bt_ref[h, lp])
                    scores[i] += pmasks_ref[lp] * jnp.where(left, prod[:c], prod[c:])
            for i, h in enumerate(pair):
                diag = masks_ref[6] * _dot_nt(a_ref[7, :, lanes(h)], b_ref[7, :, lanes(h)])
                scores[i] += jnp.concatenate([diag, jnp.zeros_like(diag)], axis=1)
                s_ref[h] = scores[i].astype(_BF16)
        for pair in pairs:
            states = [state_ref[h] for h in pair]
            intra = [_dot(s_ref[h], v_ref[:, lanes(h)]) for h in pair]
            inter = [_dot_nt(a_ref[0, :, lanes(h)], states[i].astype(_BF16))
                     for i, h in enumerate(pair)]
            update = [_dot_tn(v_ref[0:c, lanes(h)], b_ref[0, :, lanes(h)]) for h in pair]
            for i, h in enumerate(pair):
                state_ref[h] = states[i] * chunk_decay[:, lanes(h)] + update[i]
                o = intra[i] + inter[i]
                o = o * _rms_scale(o) * out_gain
                gate = proj_ref[rows, 3 * width + h * HEAD_DIM:3 * width + (h + 1) * HEAD_DIM]
                o_ref[rows, lanes(h)] = (o * (gate * _sigmoid(gate))).astype(o_ref.dtype)

    def two_chunks(i, carry):
        first = 2 * i
        prepare(first + 1, 1)
        consume(first, 0)
        prepare(jnp.minimum(first + 2, n_chunks - 1), 0)
        consume(first + 1, 1)
        return carry

    prepare(0, 0)
    lax.fori_loop(0, n_chunks // 2, two_chunks, 0)


def _hgrn_mixer(h, norm_gain, w_in, lb_logits, out_gain, *, batch, layer):
    t, d = h.shape
    width = HGRN_HEADS * HEAD_DIM
    tc = HGRN_STEP_TOKENS
    steps = t // batch // tc
    sums, masks = _hgrn_tables()
    pair_masks = np.stack([np.concatenate([masks[2 * p], masks[2 * p + 1]], axis=1)
                           for p in range(3)])
    c = HGRN_CHUNK
    return pl.pallas_call(
        functools.partial(_hgrn_kernel, layer=layer),
        grid=(batch, steps),
        in_specs=[
            pl.BlockSpec((tc, d), lambda b, s: (b * steps + s, 0)),
            _resident((1, d)),
            _resident(w_in.shape),
            _resident(lb_logits.shape),
            _resident((1, HEAD_DIM)),
            _resident(sums.shape),
            _resident(masks.shape),
            _resident(pair_masks.shape),
        ],
        out_specs=pl.BlockSpec((tc, width), lambda b, s: (b * steps + s, 0)),
        out_shape=jax.ShapeDtypeStruct((t, width), _BF16),
        scratch_shapes=[
            pltpu.VMEM((HGRN_HEADS, HEAD_DIM, HEAD_DIM), _F32),
            pltpu.VMEM((tc, 4 * width), _F32),
            pltpu.VMEM((2, 8, c, width), _BF16),
            pltpu.VMEM((2, 8, c, width), _BF16),
            pltpu.VMEM((2, 2 * c, width), _BF16),
            pltpu.VMEM((2, 1, width), _F32),
            pltpu.VMEM((HGRN_HEADS, c, 2 * c), _BF16),
            pltpu.VMEM((2, HGRN_HEADS, 3, HEAD_DIM, 2 * c), _BF16),
        ],
        compiler_params=_params("parallel", "arbitrary"),
        name="hgrn_mixer",
    )(h, norm_gain.reshape(1, d), w_in, lb_logits, out_gain.reshape(1, HEAD_DIM),
      jnp.asarray(sums, _BF16), jnp.asarray(masks, _F32), jnp.asarray(pair_masks, _F32))


def _tail_kernel(a_ref, wo_ref, h_ref, gain_ref, wi_ref, wd_ref, fgain_ref, o_ref, u_ref, *,
                 final_norm, ff_tile):
    n_slabs, _, kw = a_ref.shape
    mixed = h_ref[...]
    for s in range(n_slabs):
        mixed += _dot(a_ref[s], wo_ref[s * kw:(s + 1) * kw, :])
    o_ref[...] = mixed
    u_ref[...] = (mixed * _rms_scale(mixed) * gain_ref[...]).astype(_BF16)
    d_ff = wd_ref.shape[0]
    for j in range(d_ff // ff_tile):
        u = u_ref[...]
        gate = _dot(u, wi_ref[:, j * ff_tile:(j + 1) * ff_tile])
        up = _dot(u, wi_ref[:, d_ff + j * ff_tile:d_ff + (j + 1) * ff_tile])
        act = (gate * _sigmoid(gate) * up).astype(_BF16)
        o_ref[...] += _dot(act, wd_ref[j * ff_tile:(j + 1) * ff_tile, :])
    if final_norm:
        y = o_ref[...]
        o_ref[...] = y * _rms_scale(y) * fgain_ref[...]


def _block_tail(a, w_out, h, gain, w_in, w_down, final_gain, *, layer, final_norm, ff_tile=256):
    n_slabs, t, kw = a.shape
    d = h.shape[1]
    tm = ROW_TILE
    return pl.pallas_call(
        functools.partial(_tail_kernel, final_norm=final_norm, ff_tile=ff_tile),
        grid=(t // tm,),
        in_specs=[
            pl.BlockSpec((n_slabs, tm, kw), lambda i: (0, i, 0)),
            _resident(w_out.shape),
            pl.BlockSpec((tm, d), lambda i: (i, 0)),
            _resident((1, d)),
            _resident(w_in.shape, layer=layer),
            _resident(w_down.shape, layer=layer),
            _resident((1, d)),
        ],
        out_specs=pl.BlockSpec((tm, d), lambda i: (i, 0)),
        out_shape=jax.ShapeDtypeStruct((t, d), _F32),
        scratch_shapes=[pltpu.VMEM((tm, d), _BF16)],
        compiler_params=_params("parallel"),
        name="block_tail",
    )(a, w_out, h, gain.reshape(1, d), w_in, w_down, final_gain.reshape(1, d))


def _attn_kernel(*refs):
    ins = refs[:15]
    out_ref, o_scr, l_scr = refs[15:]
    first_block = pl.program_id(1) == 0
    row = lax.broadcasted_iota(jnp.int32, (SPAN, SPAN), 0)
    col = lax.broadcasted_iota(jnp.int32, (SPAN, SPAN), 1)
    bias_cur = jnp.where(col <= row, 0.0, NEG_BIG).astype(_F32)
    bias_prev = jnp.where(col >= row, 0.0, NEG_BIG).astype(_F32)
    bias_halo = bias_prev + jnp.where(first_block, NEG_BIG, 0.0).astype(_F32)
    ones = jnp.ones((SPAN, HEAD_DIM), _BF16)
    chain_len = 4
    body_units = 16

    for g, dil in enumerate(DILATIONS):
        q_ref, k_ref, v_ref, kh_ref, vh_ref = ins[5 * g:5 * g + 5]

        def load(ref, start, dil=dil):
            idx = pl.ds(start, SPAN) if dil == 1 else pl.ds(start, SPAN, stride=dil)
            return ref[idx, :].astype(_BF16)

        def load_kv(kref, vref, start, load=load):
            return load(kref, start), jnp.concatenate([load(vref, start), ones], axis=1)

        def chain(starts, prev, prev_bias, g=g, dil=dil, q_ref=q_ref, k_ref=k_ref, v_ref=v_ref,
                  load=load, load_kv=load_kv):
            for start in starts:
                kp, vp = prev
                kc, vc = cur = load_kv(k_ref, v_ref, start)
                s = _dot_nt(load(q_ref, start), jnp.concatenate([kp, kc], axis=0))
                s = s + jnp.concatenate([prev_bias, bias_cur], axis=1)
                m = jnp.max(jnp.maximum(s[:, :SPAN], s[:, SPAN:]), axis=-1, keepdims=True)
                p = jnp.exp2(s - m).astype(_BF16)
                r = _dot(p, jnp.concatenate([vp, vc], axis=0))
                denom = r[:, HEAD_DIM:]
                idx = pl.ds(start, SPAN) if dil == 1 else pl.ds(start, SPAN, stride=dil)
                o_scr[g, idx, :] = r[:, :HEAD_DIM] / denom
                l_scr[g, idx, :] = m + jnp.log2(denom)
                prev, prev_bias = cur, bias_prev

        block_rows = SPAN * dil
        if dil == DILATIONS[-1]:
            def body(i, carry, chain=chain, load_kv=load_kv, kh_ref=kh_ref, vh_ref=vh_ref):
                for u in range(body_units):
                    r = i * body_units + u
                    chain([r], load_kv(kh_ref, vh_ref, r), bias_halo)
                return carry
            lax.fori_loop(0, dil // body_units, body, 0)
        elif dil > 1:
            assert DILATIONS[-1] // dil == chain_len

            def body(i, carry, chain=chain, load_kv=load_kv, kh_ref=kh_ref, vh_ref=vh_ref,
                     block_rows=block_rows):
                for u in range(body_units // chain_len):
                    r = i * (body_units // chain_len) + u
                    chain([r + b * block_rows for b in range(chain_len)],
                          load_kv(kh_ref, vh_ref, r), bias_halo)
                return carry
            lax.fori_loop(0, dil * chain_len // body_units, body, 0)
        else:
            for first in range(0, DILATIONS[-1], chain_len):
                prev = (load_kv(kh_ref, vh_ref, 0) if first == 0
                        else load_kv(k_ref, v_ref, (first - 1) * block_rows))
                chain([(first + b) * block_rows for b in range(chain_len)], prev,
                      bias_halo if first == 0 else bias_prev)

    merge_rows = 256

    def merge(ci, carry):
        rr = pl.ds(pl.multiple_of(ci * merge_rows, merge_rows), merge_rows)
        l0, l1, l2 = l_scr[0, rr, :], l_scr[1, rr, :], l_scr[2, rr, :]
        m = jnp.maximum(jnp.maximum(l0, l1), l2)
        e0, e1, e2 = jnp.exp2(l0 - m), jnp.exp2(l1 - m), jnp.exp2(l2 - m)
        inv = 1.0 / (e0 + e1 + e2)
        out_ref[0, rr, :] = (o_scr[0, rr, :] * (e0 * inv)).astype(out_ref.dtype)
        out_ref[1, rr, :] = (o_scr[1, rr, :] * (e1 * inv)).astype(out_ref.dtype)
        out_ref[2, rr, :] = (o_scr[2, rr, :] * (e2 * inv)).astype(out_ref.dtype)
        return carry

    lax.fori_loop(0, ATTN_BLOCK // merge_rows, merge, 0)


def _attention(qkv, *, batch):
    t = qkv.shape[0]
    tb = ATTN_BLOCK
    steps = t // batch // tb
    in_specs, operands = [], []
    for g, dil in enumerate(DILATIONS):
        halo = SPAN * dil
        ratio = tb // halo

        def cur(which, g=g):
            return pl.BlockSpec(
                (tb, HEAD_DIM),
                lambda b, i, j: (b * steps + i, which * ATTN_HEADS + g * HEADS_PER_GROUP + j))

        def prev(which, g=g, ratio=ratio, halo=halo):
            return pl.BlockSpec(
                (halo, HEAD_DIM),
                lambda b, i, j: (jnp.maximum((b * steps + i) * ratio - 1, 0),
                                 which * ATTN_HEADS + g * HEADS_PER_GROUP + j))

        in_specs += [cur(0), cur(1), cur(2), prev(1), prev(2)]
        operands += [qkv] * 5
    n_groups = len(DILATIONS)
    return pl.pallas_call(
        _attn_kernel,
        grid=(batch, steps, HEADS_PER_GROUP),
        in_specs=in_specs,
        out_specs=pl.BlockSpec((n_groups, tb, HEAD_DIM), lambda b, i, j: (0, b * steps + i, j)),
        out_shape=jax.ShapeDtypeStruct((n_groups, t, HEADS_PER_GROUP * HEAD_DIM), _BF16),
        scratch_shapes=[pltpu.VMEM((3, tb, HEAD_DIM), _F32), pltpu.VMEM((3, tb, HEAD_DIM), _F32)],
        compiler_params=_params("parallel", "arbitrary", "arbitrary"),
        name="dilated_attention",
    )(*operands)


def _rope_tables(seq_len):
    inv_freq = 1.0 / (ROPE_THETA ** (np.arange(0, HEAD_DIM, 2, dtype=np.float64) / HEAD_DIM))
    ang = np.arange(seq_len, dtype=np.float64)[:, None] * inv_freq[None, :]
    cos, sin = np.cos(ang), np.sin(ang)
    return (jnp.asarray(np.concatenate([cos, cos], axis=-1), _F32),
            jnp.asarray(np.concatenate([-sin, sin], axis=-1), _F32))


def kernel(x, norm_mix, norm_ffn, hgrn_w_in, hgrn_lb_logits, hgrn_out_norm, hgrn_w_out,
           attn_w_qkv, attn_w_out, ffn_w_in, ffn_w_down, final_norm):
    batch, seq, d = x.shape
    cos2, sin2 = _rope_tables(seq)
    h = x.reshape(batch * seq, d)

    gated = _hgrn_mixer(h, norm_mix[0], hgrn_w_in[0], hgrn_lb_logits, hgrn_out_norm[0],
                        batch=batch, layer=0)
    ffn_in, ffn_down = ffn_w_in, ffn_w_down
    h = _block_tail(gated[None], hgrn_w_out[0], h, norm_ffn[0], ffn_in, ffn_down, final_norm,
                    layer=0, final_norm=False)

    qkv = _qkv_projection(h, norm_mix[1], attn_w_qkv[0], cos2, sin2,
                          scale=HEAD_DIM ** -0.5 * LOG2_E)
    attn = _attention(qkv, batch=batch)
    h = _block_tail(attn, attn_w_out[0], h, norm_ffn[1], ffn_in, ffn_down, final_norm,
                    layer=1, final_norm=True)
    return h.reshape(batch, seq, d)
```

```python
import functools

import numpy as np
import jax
import jax.numpy as jnp
from jax import lax
from jax.experimental import pallas as pl
from jax.experimental.pallas import tpu as pltpu

D_MODEL = 1024
HEAD_DIM = 128
HGRN_HEADS = 8
HGRN_CHUNK = 64
HGRN_LEVELS = 6
HGRN_STEP_TOKENS = 512
ATTN_HEADS = 12
ATTN_WIDTH = ATTN_HEADS * HEAD_DIM
DILATIONS = (1, 4, 16)
SPAN = 128
HEADS_PER_GROUP = 4
ATTN_BLOCK = SPAN * DILATIONS[-1]
ROW_TILE = 512
COL_TILE = 512
ROPE_THETA = 10000.0
NORM_EPS = 1e-6
NEG_BIG = -1e30
LOG2_E = float(np.log2(np.e))
VMEM_LIMIT_BYTES = 56 * 1024 * 1024

_F32 = jnp.float32
_BF16 = jnp.bfloat16


def _dot(a, b):
    return lax.dot_general(a, b, (((1,), (0,)), ((), ())), preferred_element_type=_F32)


def _dot_nt(a, b):
    return lax.dot_general(a, b, (((1,), (1,)), ((), ())), preferred_element_type=_F32)


def _dot_tn(a, b):
    return lax.dot_general(a, b, (((0,), (0,)), ((), ())), preferred_element_type=_F32)


def _sigmoid(x):
    return 1.0 / (1.0 + jnp.exp2(x * (-LOG2_E)))


def _rms_scale(x):
    return lax.rsqrt(jnp.mean(x * x, axis=-1, keepdims=True) + NORM_EPS)


def _params(*sem):
    return pltpu.CompilerParams(dimension_semantics=sem, vmem_limit_bytes=VMEM_LIMIT_BYTES)


def _resident(shape, layer=None):
    if layer is None:
        index = (0,) * len(shape)
    else:
        index = (layer,) + (0,) * (len(shape) - 1)
        shape = (None,) + tuple(shape[1:])
    return pl.BlockSpec(shape, lambda *_: index, pipeline_mode=pl.Buffered(1))


def _qkv_kernel(h_ref, gain_ref, w_ref, cos_ref, sin_ref, o_ref, *, scale):
    x = h_ref[...]
    u = (x * _rms_scale(x) * gain_ref[...]).astype(_BF16)
    cos, sin = cos_ref[...], sin_ref[...]
    cos_q, sin_q = cos * scale, sin * scale
    for j in range(w_ref.shape[1] // COL_TILE):
        res = _dot(u, w_ref[:, j * COL_TILE:(j + 1) * COL_TILE])
        for hh in range(COL_TILE // HEAD_DIM):
            lo = j * COL_TILE + hh * HEAD_DIM
            xh = res[:, hh * HEAD_DIM:(hh + 1) * HEAD_DIM]
            if lo < ATTN_WIDTH:
                xh = xh * cos_q + pltpu.roll(xh, HEAD_DIM // 2, 1) * sin_q
            elif lo < 2 * ATTN_WIDTH:
                xh = xh * cos + pltpu.roll(xh, HEAD_DIM // 2, 1) * sin
            o_ref[:, lo:lo + HEAD_DIM] = xh


def _qkv_projection(h, gain, w, cos2, sin2, *, scale):
    t, d = h.shape
    n = w.shape[1]
    tm = ROW_TILE
    seq_tiles = cos2.shape[0] // tm
    return pl.pallas_call(
        functools.partial(_qkv_kernel, scale=scale),
        grid=(t // tm,),
        in_specs=[
            pl.BlockSpec((tm, d), lambda i: (i, 0)),
            _resident((1, d)),
            _resident((d, n)),
            pl.BlockSpec((tm, HEAD_DIM), lambda i: (i % seq_tiles, 0)),
            pl.BlockSpec((tm, HEAD_DIM), lambda i: (i % seq_tiles, 0)),
        ],
        out_specs=pl.BlockSpec((tm, n), lambda i: (i, 0)),
        out_shape=jax.ShapeDtypeStruct((t, n), _F32),
        compiler_params=_params("parallel"),
        name="qkv_projection",
    )(h, gain.reshape(1, d), w, cos2, sin2)


def _hgrn_tables():
    c = HGRN_CHUNK
    assert 1 << HGRN_LEVELS == c
    t = np.arange(c)
    col = t[None, :]
    row = t[:, None]
    sums = np.zeros((HGRN_LEVELS + 1, c, c), np.float32)
    sums[0] = col <= row
    sums[1] = col > row
    masks = np.zeros((HGRN_LEVELS + 1, c, c), np.float32)
    for level in range(HGRN_LEVELS):
        half = c >> (level + 1)
        block = t // (2 * half)
        mid = block * 2 * half + half
        is_query = t >= mid
        if level < HGRN_LEVELS - 1:
            q_rows = (col >= mid[:, None]) & (col <= row) & is_query[:, None]
            k_rows = (col > row) & (col < mid[:, None]) & (~is_query)[:, None]
            sums[2 + level] = q_rows | k_rows
        masks[level] = ((block[:, None] == block[None, :]) & is_query[:, None]
                        & (~is_query)[None, :])
    masks[HGRN_LEVELS] = np.eye(c)
    assert np.array_equal(masks.sum(0), np.tril(np.ones((c, c))))
    sums = sums.reshape((HGRN_LEVELS + 1) * c, c)
    return np.concatenate([sums, sums], axis=1), masks


def _hgrn_kernel(h_ref, ngain_ref, w_ref, lbl_ref, gain_ref, sums_ref, masks_ref, o_ref,
                 state_ref, proj_ref, a2_ref, b2_ref, v2_ref, decay_ref, s_ref, *, layer):
    c = HGRN_CHUNK
    width = HGRN_HEADS * HEAD_DIM

    @pl.when(pl.program_id(1) == 0)
    def _():
        state_ref[...] = jnp.zeros_like(state_ref)

    x = h_ref[...]
    u = (x * _rms_scale(x) * ngain_ref[...]).astype(_BF16)
    for j in range(w_ref.shape[1] // COL_TILE):
        cols = slice(j * COL_TILE, (j + 1) * COL_TILE)
        proj_ref[:, cols] = _dot(u, w_ref[:, cols])

    logits = lbl_ref[...]
    e = jnp.exp(logits - jnp.max(logits, axis=0, keepdims=True))
    lb = jnp.sum(e[:layer + 1], axis=0, keepdims=True) / jnp.sum(e, axis=0, keepdims=True)
    out_gain = gain_ref[...]

    n_chunks = h_ref.shape[0] // c

    def chunk_rows(ci):
        return pl.ds(pl.multiple_of(ci * c, c), c)

    def prepare(ci, slot):
        a_ref, b_ref, v_ref = a2_ref.at[slot], b2_ref.at[slot], v2_ref.at[slot]
        rows = chunk_rows(ci)
        q = proj_ref[rows, 0:width]
        forget = lb + (1.0 - lb) * _sigmoid(proj_ref[rows, width:2 * width])
        glog = jnp.log(forget) * LOG2_E
        kk = 1.0 - forget
        qq = q * _sigmoid(q)
        g_hi = glog.astype(_BF16)
        g_lo = (glog - g_hi.astype(_F32)).astype(_BF16)
        factors = jnp.exp2(_dot(sums_ref[...], jnp.concatenate([g_hi, g_lo], axis=0)))

        qq = qq.astype(_BF16)
        kk = kk.astype(_BF16)
        from_start = factors[0:c]
        a_ref[0] = qq * from_start.astype(_BF16)
        decay_ref[slot] = from_start[c - 1:c, :]
        b_ref[0] = kk * factors[c:2 * c].astype(_BF16)
        for level in range(HGRN_LEVELS - 1):
            fac = factors[(2 + level) * c:(3 + level) * c].astype(_BF16)
            a_ref[1 + level] = qq * fac
            b_ref[1 + level] = kk * fac
        a_ref[HGRN_LEVELS] = qq * forget.astype(_BF16)
        a_ref[HGRN_LEVELS + 1] = qq
        b_ref[HGRN_LEVELS] = kk
        v_ref[...] = proj_ref[rows, 2 * width:3 * width].astype(_BF16)

    def lanes(h):
        return slice(h * HEAD_DIM, (h + 1) * HEAD_DIM)

    def consume(ci, slot):
        a_ref, b_ref, v_ref = a2_ref.at[slot], b2_ref.at[slot], v2_ref.at[slot]
        rows = chunk_rows(ci)
        chunk_decay = decay_ref[slot]
        pairs = [(h, h + 1) for h in range(0, HGRN_HEADS, 2)]
        for pair in pairs:
            scores = [jnp.zeros((c, c), _F32) for _ in pair]
            for level in range(HGRN_LEVELS - 1):
                for i, h in enumerate(pair):
                    scores[i] += masks_ref[level] * _dot_nt(a_ref[1 + level, :, lanes(h)],
                                                            b_ref[1 + level, :, lanes(h)])
            for i, h in enumerate(pair):
                both = a_ref[HGRN_LEVELS:HGRN_LEVELS + 2, :, lanes(h)].reshape(2 * c, HEAD_DIM)
                prod = _dot_nt(both, b_ref[HGRN_LEVELS, :, lanes(h)])
                scores[i] += (masks_ref[HGRN_LEVELS - 1] * prod[:c]
                              + masks_ref[HGRN_LEVELS] * prod[c:])
                s_ref[h] = scores[i].astype(_BF16)
        for pair in pairs:
            states = [state_ref[h] for h in pair]
            intra = [_dot(s_ref[h], v_ref[:, lanes(h)]) for h in pair]
            inter = [_dot(a_ref[0, :, lanes(h)], states[i].astype(_BF16))
                     for i, h in enumerate(pair)]
            update = [_dot_tn(b_ref[0, :, lanes(h)], v_ref[:, lanes(h)]) for h in pair]
            for i, h in enumerate(pair):
                decay_col = jnp.transpose(
                    jnp.broadcast_to(chunk_decay[:, lanes(h)], (8, HEAD_DIM)))[:, 0:1]
                state_ref[h] = states[i] * decay_col + update[i]
                o = intra[i] + inter[i]
                o = o * _rms_scale(o) * out_gain
                gate = proj_ref[rows, 3 * width + h * HEAD_DIM:3 * width + (h + 1) * HEAD_DIM]
                o_ref[rows, lanes(h)] = (o * (gate * _sigmoid(gate))).astype(o_ref.dtype)

    def two_chunks(i, carry):
        first = 2 * i
        prepare(first + 1, 1)
        consume(first, 0)
        prepare(jnp.minimum(first + 2, n_chunks - 1), 0)
        consume(first + 1, 1)
        return carry

    prepare(0, 0)
    lax.fori_loop(0, n_chunks // 2, two_chunks, 0)


def _hgrn_mixer(h, norm_gain, w_in, lb_logits, out_gain, *, batch, layer):
    t, d = h.shape
    width = HGRN_HEADS * HEAD_DIM
    tc = HGRN_STEP_TOKENS
    steps = t // batch // tc
    sums, masks = _hgrn_tables()
    c = HGRN_CHUNK
    return pl.pallas_call(
        functools.partial(_hgrn_kernel, layer=layer),
        grid=(batch, steps),
        in_specs=[
            pl.BlockSpec((tc, d), lambda b, s: (b * steps + s, 0)),
            _resident((1, d)),
            _resident(w_in.shape),
            _resident(lb_logits.shape),
            _resident((1, HEAD_DIM)),
            _resident(sums.shape),
            _resident(masks.shape),
        ],
        out_specs=pl.BlockSpec((tc, width), lambda b, s: (b * steps + s, 0)),
        out_shape=jax.ShapeDtypeStruct((t, width), _BF16),
        scratch_shapes=[
            pltpu.VMEM((HGRN_HEADS, HEAD_DIM, HEAD_DIM), _F32),
            pltpu.VMEM((tc, 4 * width), _F32),
            pltpu.VMEM((2, HGRN_LEVELS + 2, c, width), _BF16),
            pltpu.VMEM((2, HGRN_LEVELS + 1, c, width), _BF16),
            pltpu.VMEM((2, c, width), _BF16),
            pltpu.VMEM((2, 1, width), _F32),
            pltpu.VMEM((HGRN_HEADS, c, c), _BF16),
        ],
        compiler_params=_params("parallel", "arbitrary"),
        name="hgrn_mixer",
    )(h, norm_gain.reshape(1, d), w_in, lb_logits, out_gain.reshape(1, HEAD_DIM),
      jnp.asarray(sums, _BF16), jnp.asarray(masks, _F32))


def _tail_kernel(a_ref, wo_ref, h_ref, gain_ref, wi_ref, wd_ref, fgain_ref, o_ref, u_ref, *,
                 final_norm, ff_tile):
    n_slabs, _, kw = a_ref.shape
    mixed = h_ref[...]
    for s in range(n_slabs):
        mixed += _dot(a_ref[s], wo_ref[s * kw:(s + 1) * kw, :])
    o_ref[...] = mixed
    u_ref[...] = (mixed * _rms_scale(mixed) * gain_ref[...]).astype(_BF16)
    d_ff = wd_ref.shape[0]
    for j in range(d_ff // ff_tile):
        u = u_ref[...]
        gate = _dot(u, wi_ref[:, j * ff_tile:(j + 1) * ff_tile])
        up = _dot(u, wi_ref[:, d_ff + j * ff_tile:d_ff + (j + 1) * ff_tile])
        act = (gate * _sigmoid(gate) * up).astype(_BF16)
        o_ref[...] += _dot(act, wd_ref[j * ff_tile:(j + 1) * ff_tile, :])
    if final_norm:
        y = o_ref[...]
        o_ref[...] = y * _rms_scale(y) * fgain_ref[...]


def _block_tail(a, w_out, h, gain, w_in, w_down, final_gain, *, layer, final_norm, ff_tile=256):
    n_slabs, t, kw = a.shape
    d = h.shape[1]
    tm = ROW_TILE
    return pl.pallas_call(
        functools.partial(_tail_kernel, final_norm=final_norm, ff_tile=ff_tile),
        grid=(t // tm,),
        in_specs=[
            pl.BlockSpec((n_slabs, tm, kw), lambda i: (0, i, 0)),
            _resident(w_out.shape),
            pl.BlockSpec((tm, d), lambda i: (i, 0)),
            _resident((1, d)),
            _resident(w_in.shape, layer=layer),
            _resident(w_down.shape, layer=layer),
            _resident((1, d)),
        ],
        out_specs=pl.BlockSpec((tm, d), lambda i: (i, 0)),
        out_shape=jax.ShapeDtypeStruct((t, d), _F32),
        scratch_shapes=[pltpu.VMEM((tm, d), _BF16)],
        compiler_params=_params("parallel"),
        name="block_tail",
    )(a, w_out, h, gain.reshape(1, d), w_in, w_down, final_gain.reshape(1, d))


def _attn_kernel(*refs):
    ins = refs[:15]
    out_ref, o_scr, l_scr = refs[15:]
    first_block = pl.program_id(1) == 0
    row = lax.broadcasted_iota(jnp.int32, (SPAN, SPAN), 0)
    col = lax.broadcasted_iota(jnp.int32, (SPAN, SPAN), 1)
    bias_cur = jnp.where(col <= row, 0.0, NEG_BIG).astype(_F32)
    bias_prev = jnp.where(col >= row, 0.0, NEG_BIG).astype(_F32)
    bias_halo = bias_prev + jnp.where(first_block, NEG_BIG, 0.0).astype(_F32)
    ones = jnp.ones((SPAN, HEAD_DIM), _BF16)
    chain_len = 4
    body_units = 16

    for g, dil in enumerate(DILATIONS):
        q_ref, k_ref, v_ref, kh_ref, vh_ref = ins[5 * g:5 * g + 5]

        def load(ref, start, dil=dil):
            idx = pl.ds(start, SPAN) if dil == 1 else pl.ds(start, SPAN, stride=dil)
            return ref[idx, :].astype(_BF16)

        def load_kv(kref, vref, start, load=load):
            return load(kref, start), jnp.concatenate([load(vref, start), ones], axis=1)

        def chain(starts, prev, prev_bias, g=g, dil=dil, q_ref=q_ref, k_ref=k_ref, v_ref=v_ref,
                  load=load, load_kv=load_kv):
            for start in starts:
                kp, vp = prev
                kc, vc = cur = load_kv(k_ref, v_ref, start)
                s = _dot_nt(load(q_ref, start), jnp.concatenate([kp, kc], axis=0))
                s = s + jnp.concatenate([prev_bias, bias_cur], axis=1)
                m = jnp.max(jnp.maximum(s[:, :SPAN], s[:, SPAN:]), axis=-1, keepdims=True)
                p = jnp.exp2(s - m).astype(_BF16)
                r = _dot(p, jnp.concatenate([vp, vc], axis=0))
                denom = r[:, HEAD_DIM:]
                idx = pl.ds(start, SPAN) if dil == 1 else pl.ds(start, SPAN, stride=dil)
                o_scr[g, idx, :] = r[:, :HEAD_DIM] / denom
                l_scr[g, idx, :] = m + jnp.log2(denom)
                prev, prev_bias = cur, bias_prev

        block_rows = SPAN * dil
        if dil == DILATIONS[-1]:
            def body(i, carry, chain=chain, load_kv=load_kv, kh_ref=kh_ref, vh_ref=vh_ref):
                for u in range(body_units):
                    r = i * body_units + u
                    chain([r], load_kv(kh_ref, vh_ref, r), bias_halo)
                return carry
            lax.fori_loop(0, dil // body_units, body, 0)
        elif dil > 1:
            assert DILATIONS[-1] // dil == chain_len

            def body(i, carry, chain=chain, load_kv=load_kv, kh_ref=kh_ref, vh_ref=vh_ref,
                     block_rows=block_rows):
                for u in range(body_units // chain_len):
                    r = i * (body_units // chain_len) + u
                    chain([r + b * block_rows for b in range(chain_len)],
                          load_kv(kh_ref, vh_ref, r), bias_halo)
                return carry
            lax.fori_loop(0, dil * chain_len // body_units, body, 0)
        else:
            for first in range(0, DILATIONS[-1], chain_len):
                prev = (load_kv(kh_ref, vh_ref, 0) if first == 0
                        else load_kv(k_ref, v_ref, (first - 1) * block_rows))
                chain([(first + b) * block_rows for b in range(chain_len)], prev,
                      bias_halo if first == 0 else bias_prev)

    merge_rows = 256

    def merge(ci, carry):
        rr = pl.ds(pl.multiple_of(ci * merge_rows, merge_rows), merge_rows)
        l0, l1, l2 = l_scr[0, rr, :], l_scr[1, rr, :], l_scr[2, rr, :]
        m = jnp.maximum(jnp.maximum(l0, l1), l2)
        e0, e1, e2 = jnp.exp2(l0 - m), jnp.exp2(l1 - m), jnp.exp2(l2 - m)
        inv = 1.0 / (e0 + e1 + e2)
        out_ref[0, rr, :] = (o_scr[0, rr, :] * (e0 * inv)).astype(out_ref.dtype)
        out_ref[1, rr, :] = (o_scr[1, rr, :] * (e1 * inv)).astype(out_ref.dtype)
        out_ref[2, rr, :] = (o_scr[2, rr, :] * (e2 * inv)).astype(out_ref.dtype)
        return carry

    lax.fori_loop(0, ATTN_BLOCK // merge_rows, merge, 0)


def _attention(qkv, *, batch):
    t = qkv.shape[0]
    tb = ATTN_BLOCK
    steps = t // batch // tb
    in_specs, operands = [], []
    for g, dil in enumerate(DILATIONS):
        halo = SPAN * dil
        ratio = tb // halo

        def cur(which, g=g):
            return pl.BlockSpec(
                (tb, HEAD_DIM),
                lambda b, i, j: (b * steps + i, which * ATTN_HEADS + g * HEADS_PER_GROUP + j))

        def prev(which, g=g, ratio=ratio, halo=halo):
            return pl.BlockSpec(
                (halo, HEAD_DIM),
                lambda b, i, j: (jnp.maximum((b * steps + i) * ratio - 1, 0),
                                 which * ATTN_HEADS + g * HEADS_PER_GROUP + j))

        in_specs += [cur(0), cur(1), cur(2), prev(1), prev(2)]
        operands += [qkv] * 5
    n_groups = len(DILATIONS)
    return pl.pallas_call(
        _attn_kernel,
        grid=(batch, steps, HEADS_PER_GROUP),
        in_specs=in_specs,
        out_specs=pl.BlockSpec((n_groups, tb, HEAD_DIM), lambda b, i, j: (0, b * steps + i, j)),
        out_shape=jax.ShapeDtypeStruct((n_groups, t, HEADS_PER_GROUP * HEAD_DIM), _BF16),
        scratch_shapes=[pltpu.VMEM((3, tb, HEAD_DIM), _F32), pltpu.VMEM((3, tb, HEAD_DIM), _F32)],
        compiler_params=_params("parallel", "arbitrary", "arbitrary"),
        name="dilated_attention",
    )(*operands)


def _rope_tables(seq_len):
    inv_freq = 1.0 / (ROPE_THETA ** (np.arange(0, HEAD_DIM, 2, dtype=np.float64) / HEAD_DIM))
    ang = np.arange(seq_len, dtype=np.float64)[:, None] * inv_freq[None, :]
    cos, sin = np.cos(ang), np.sin(ang)
    return (jnp.asarray(np.concatenate([cos, cos], axis=-1), _F32),
            jnp.asarray(np.concatenate([-sin, sin], axis=-1), _F32))


def kernel(x, norm_mix, norm_ffn, hgrn_w_in, hgrn_lb_logits, hgrn_out_norm, hgrn_w_out,
           attn_w_qkv, attn_w_out, ffn_w_in, ffn_w_down, final_norm):
    batch, seq, d = x.shape
    cos2, sin2 = _rope_tables(seq)
    h = x.reshape(batch * seq, d)

    gated = _hgrn_mixer(h, norm_mix[0], hgrn_w_in[0], hgrn_lb_logits, hgrn_out_norm[0],
                        batch=batch, layer=0)
    ffn_in, ffn_down = ffn_w_in, ffn_w_down
    h = _block_tail(gated[None], hgrn_w_out[0], h, norm_ffn[0], ffn_in, ffn_down, final_norm,
                    layer=0, final_norm=False)

    qkv = _qkv_projection(h, norm_mix[1], attn_w_qkv[0], cos2, sin2,
                          scale=HEAD_DIM ** -0.5 * LOG2_E)
    attn = _attention(qkv, batch=batch)
    h = _block_tail(attn, attn_w_out[0], h, norm_ffn[1], ffn_in, ffn_down, final_norm,
                    layer=1, final_norm=True)
    return h.reshape(batch, seq, d)
```

```python
import functools

import numpy as np
import jax
import jax.numpy as jnp
from jax import lax
from jax.experimental import pallas as pl
from jax.experimental.pallas import tpu as pltpu

D_MODEL = 1024
HEAD_DIM = 128
HGRN_HEADS = 8
HGRN_CHUNK = 64
HGRN_LEVELS = 6
HGRN_STEP_TOKENS = 512
ATTN_HEADS = 12
ATTN_WIDTH = ATTN_HEADS * HEAD_DIM
DILATIONS = (1, 4, 16)
SPAN = 128
HEADS_PER_GROUP = 4
ATTN_BLOCK = SPAN * DILATIONS[-1]
ROW_TILE = 512
COL_TILE = 512
ROPE_THETA = 10000.0
NORM_EPS = 1e-6
NEG_BIG = -1e30
LOG2_E = float(np.log2(np.e))
VMEM_LIMIT_BYTES = 56 * 1024 * 1024

_F32 = jnp.float32
_BF16 = jnp.bfloat16


def _dot(a, b):
    return lax.dot_general(a, b, (((1,), (0,)), ((), ())), preferred_element_type=_F32)


def _dot_nt(a, b):
    return lax.dot_general(a, b, (((1,), (1,)), ((), ())), preferred_element_type=_F32)


def _dot_tn(a, b):
    return lax.dot_general(a, b, (((0,), (0,)), ((), ())), preferred_element_type=_F32)


def _sigmoid(x):
    return 1.0 / (1.0 + jnp.exp2(x * (-LOG2_E)))


def _rms_scale(x):
    return lax.rsqrt(jnp.mean(x * x, axis=-1, keepdims=True) + NORM_EPS)


def _params(*sem):
    return pltpu.CompilerParams(dimension_semantics=sem, vmem_limit_bytes=VMEM_LIMIT_BYTES)


def _resident(shape, layer=None):
    if layer is None:
        index = (0,) * len(shape)
    else:
        index = (layer,) + (0,) * (len(shape) - 1)
        shape = (None,) + tuple(shape[1:])
    return pl.BlockSpec(shape, lambda *_: index, pipeline_mode=pl.Buffered(1))


def _qkv_kernel(h_ref, gain_ref, w_ref, cos_ref, sin_ref, o_ref, *, scale):
    x = h_ref[...]
    u = (x * _rms_scale(x) * gain_ref[...]).astype(_BF16)
    cos, sin = cos_ref[...], sin_ref[...]
    cos_q, sin_q = cos * scale, sin * scale
    for j in range(w_ref.shape[1] // COL_TILE):
        res = _dot(u, w_ref[:, j * COL_TILE:(j + 1) * COL_TILE])
        for hh in range(COL_TILE // HEAD_DIM):
            lo = j * COL_TILE + hh * HEAD_DIM
            xh = res[:, hh * HEAD_DIM:(hh + 1) * HEAD_DIM]
            if lo < ATTN_WIDTH:
                xh = xh * cos_q + pltpu.roll(xh, HEAD_DIM // 2, 1) * sin_q
            elif lo < 2 * ATTN_WIDTH:
                xh = xh * cos + pltpu.roll(xh, HEAD_DIM // 2, 1) * sin
            o_ref[:, lo:lo + HEAD_DIM] = xh


def _qkv_projection(h, gain, w, cos2, sin2, *, scale):
    t, d = h.shape
    n = w.shape[1]
    tm = ROW_TILE
    seq_tiles = cos2.shape[0] // tm
    return pl.pallas_call(
        functools.partial(_qkv_kernel, scale=scale),
        grid=(t // tm,),
        in_specs=[
            pl.BlockSpec((tm, d), lambda i: (i, 0)),
            _resident((1, d)),
            _resident((d, n)),
            pl.BlockSpec((tm, HEAD_DIM), lambda i: (i % seq_tiles, 0)),
            pl.BlockSpec((tm, HEAD_DIM), lambda i: (i % seq_tiles, 0)),
        ],
        out_specs=pl.BlockSpec((tm, n), lambda i: (i, 0)),
        out_shape=jax.ShapeDtypeStruct((t, n), _F32),
        compiler_params=_params("parallel"),
        name="qkv_projection",
    )(h, gain.reshape(1, d), w, cos2, sin2)


def _hgrn_tables():
    c = HGRN_CHUNK
    assert 1 << HGRN_LEVELS == c
    t = np.arange(c)
    col = t[None, :]
    row = t[:, None]
    sums = np.zeros((HGRN_LEVELS + 1, c, c), np.float32)
    sums[0] = col <= row
    sums[1] = col > row
    masks = np.zeros((HGRN_LEVELS + 1, c, c), np.float32)
    for level in range(HGRN_LEVELS):
        half = c >> (level + 1)
        block = t // (2 * half)
        mid = block * 2 * half + half
        is_query = t >= mid
        if level < HGRN_LEVELS - 1:
            q_rows = (col >= mid[:, None]) & (col <= row) & is_query[:, None]
            k_rows = (col > row) & (col < mid[:, None]) & (~is_query)[:, None]
            sums[2 + level] = q_rows | k_rows
        masks[level] = ((block[:, None] == block[None, :]) & is_query[:, None]
                        & (~is_query)[None, :])
    masks[HGRN_LEVELS] = np.eye(c)
    assert np.array_equal(masks.sum(0), np.tril(np.ones((c, c))))
    sums = sums.reshape((HGRN_LEVELS + 1) * c, c)
    return np.concatenate([sums, sums], axis=1), masks


def _hgrn_kernel(h_ref, ngain_ref, w_ref, lbl_ref, gain_ref, sums_ref, masks_ref, o_ref,
                 state_ref, proj_ref, a2_ref, b2_ref, v2_ref, decay_ref, s_ref, *, layer):
    c = HGRN_CHUNK
    width = HGRN_HEADS * HEAD_DIM

    @pl.when(pl.program_id(1) == 0)
    def _():
        state_ref[...] = jnp.zeros_like(state_ref)

    x = h_ref[...]
    u = (x * _rms_scale(x) * ngain_ref[...]).astype(_BF16)
    for j in range(w_ref.shape[1] // COL_TILE):
        cols = slice(j * COL_TILE, (j + 1) * COL_TILE)
        proj_ref[:, cols] = _dot(u, w_ref[:, cols])

    logits = lbl_ref[...]
    e = jnp.exp(logits - jnp.max(logits, axis=0, keepdims=True))
    lb = jnp.sum(e[:layer + 1], axis=0, keepdims=True) / jnp.sum(e, axis=0, keepdims=True)
    out_gain = gain_ref[...]

    n_chunks = h_ref.shape[0] // c

    def chunk_rows(ci):
        return pl.ds(pl.multiple_of(ci * c, c), c)

    def prepare(ci, slot):
        a_ref, b_ref, v_ref = a2_ref.at[slot], b2_ref.at[slot], v2_ref.at[slot]
        rows = chunk_rows(ci)
        q = proj_ref[rows, 0:width]
        forget = lb + (1.0 - lb) * _sigmoid(proj_ref[rows, width:2 * width])
        glog = jnp.log(forget) * LOG2_E
        kk = 1.0 - forget
        qq = q * _sigmoid(q)
        g_hi = glog.astype(_BF16)
        g_lo = (glog - g_hi.astype(_F32)).astype(_BF16)
        factors = jnp.exp2(_dot(sums_ref[...], jnp.concatenate([g_hi, g_lo], axis=0)))

        qq = qq.astype(_BF16)
        kk = kk.astype(_BF16)
        from_start = factors[0:c]
        a_ref[0] = qq * from_start.astype(_BF16)
        decay_ref[slot] = from_start[c - 1:c, :]
        b_ref[0] = kk * factors[c:2 * c].astype(_BF16)
        for level in range(HGRN_LEVELS - 1):
            fac = factors[(2 + level) * c:(3 + level) * c].astype(_BF16)
            a_ref[1 + level] = qq * fac
            b_ref[1 + level] = kk * fac
        a_ref[HGRN_LEVELS] = qq * forget.astype(_BF16)
        a_ref[HGRN_LEVELS + 1] = qq
        b_ref[HGRN_LEVELS] = kk
        v_ref[...] = proj_ref[rows, 2 * width:3 * width].astype(_BF16)

    def lanes(h):
        return slice(h * HEAD_DIM, (h + 1) * HEAD_DIM)

    def consume(ci, slot):
        a_ref, b_ref, v_ref = a2_ref.at[slot], b2_ref.at[slot], v2_ref.at[slot]
        rows = chunk_rows(ci)
        chunk_decay = decay_ref[slot]
        pairs = [(h, h + 1) for h in range(0, HGRN_HEADS, 2)]
        for pair in pairs:
            scores = [jnp.zeros((c, c), _F32) for _ in pair]
            for level in range(HGRN_LEVELS - 1):
                for i, h in enumerate(pair):
                    scores[i] += masks_ref[level] * _dot_nt(a_ref[1 + level, :, lanes(h)],
                                                            b_ref[1 + level, :, lanes(h)])
            for i, h in enumerate(pair):
                both = a_ref[HGRN_LEVELS:HGRN_LEVELS + 2, :, lanes(h)].reshape(2 * c, HEAD_DIM)
                prod = _dot_nt(both, b_ref[HGRN_LEVELS, :, lanes(h)])
                scores[i] += (masks_ref[HGRN_LEVELS - 1] * prod[:c]
                              + masks_ref[HGRN_LEVELS] * prod[c:])
                s_ref[h] = scores[i].astype(_BF16)
        for pair in pairs:
            states = [state_ref[h] for h in pair]
            intra = [_dot(s_ref[h], v_ref[:, lanes(h)]) for h in pair]
            inter = [_dot(a_ref[0, :, lanes(h)], states[i].astype(_BF16))
                     for i, h in enumerate(pair)]
            update = [_dot_tn(b_ref[0, :, lanes(h)], v_ref[:, lanes(h)]) for h in pair]
            for i, h in enumerate(pair):
                decay_col = jnp.transpose(
                    jnp.broadcast_to(chunk_decay[:, lanes(h)], (8, HEAD_DIM)))[:, 0:1]
                state_ref[h] = states[i] * decay_col + update[i]
                o = intra[i] + inter[i]
                o = o * _rms_scale(o) * out_gain
                gate = proj_ref[rows, 3 * width + h * HEAD_DIM:3 * width + (h + 1) * HEAD_DIM]
                o_ref[rows, lanes(h)] = (o * (gate * _sigmoid(gate))).astype(o_ref.dtype)

    def two_chunks(i, carry):
        first = 2 * i
        prepare(first + 1, 1)
        consume(first, 0)
        prepare(jnp.minimum(first + 2, n_chunks - 1), 0)
        consume(first + 1, 1)
        return carry

    prepare(0, 0)
    lax.fori_loop(0, n_chunks // 2, two_chunks, 0)


def _hgrn_mixer(h, norm_gain, w_in, lb_logits, out_gain, *, batch, layer):
    t, d = h.shape
    width = HGRN_HEADS * HEAD_DIM
    tc = HGRN_STEP_TOKENS
    steps = t // batch // tc
    sums, masks = _hgrn_tables()
    c = HGRN_CHUNK
    return pl.pallas_call(
        functools.partial(_hgrn_kernel, layer=layer),
        grid=(batch, steps),
        in_specs=[
            pl.BlockSpec((tc, d), lambda b, s: (b * steps + s, 0)),
            _resident((1, d)),
            _resident(w_in.shape),
            _resident(lb_logits.shape),
            _resident((1, HEAD_DIM)),
            _resident(sums.shape),
            _resident(masks.shape),
        ],
        out_specs=pl.BlockSpec((tc, width), lambda b, s: (b * steps + s, 0)),
        out_shape=jax.ShapeDtypeStruct((t, width), _BF16),
        scratch_shapes=[
            pltpu.VMEM((HGRN_HEADS, HEAD_DIM, HEAD_DIM), _F32),
            pltpu.VMEM((tc, 4 * width), _F32),
            pltpu.VMEM((2, HGRN_LEVELS + 2, c, width), _BF16),
            pltpu.VMEM((2, HGRN_LEVELS + 1, c, width), _BF16),
            pltpu.VMEM((2, c, width), _BF16),
            pltpu.VMEM((2, 1, width), _F32),
            pltpu.VMEM((HGRN_HEADS, c, c), _BF16),
        ],
        compiler_params=_params("parallel", "arbitrary"),
        name="hgrn_mixer",
    )(h, norm_gain.reshape(1, d), w_in, lb_logits, out_gain.reshape(1, HEAD_DIM),
      jnp.asarray(sums, _BF16), jnp.asarray(masks, _F32))


def _tail_kernel(a_ref, wo_hbm, h_ref, gain_ref, wi_hbm, wd_hbm, fgain_ref, o_ref,
                 u_ref, wo_ref, wi_ref, wd_ref, sem_ref, *, layer, final_norm, ff_tile):
    n_slabs, _, kw = a_ref.shape
    d_ff = wd_ref.shape[0]
    n_tiles = d_ff // ff_tile

    def weight_copies():
        copies = [pltpu.make_async_copy(wo_hbm, wo_ref, sem_ref.at[0])]
        for j in range(n_tiles):
            for half in range(2):
                cols = pl.ds(half * d_ff + j * ff_tile, ff_tile)
                copies.append(pltpu.make_async_copy(wi_hbm.at[layer, :, cols], wi_ref.at[:, cols],
                                                    sem_ref.at[len(copies)]))
            rows = pl.ds(j * ff_tile, ff_tile)
            copies.append(pltpu.make_async_copy(wd_hbm.at[layer, rows, :], wd_ref.at[rows, :],
                                                sem_ref.at[len(copies)]))
        return copies

    def body(copies):
        if copies:
            for cp in copies:
                cp.start()
            copies[0].wait()
        mixed = h_ref[...]
        for s in range(n_slabs):
            mixed += _dot(a_ref[s], wo_ref[s * kw:(s + 1) * kw, :])
        o_ref[...] = mixed
        u_ref[...] = (mixed * _rms_scale(mixed) * gain_ref[...]).astype(_BF16)
        for j in range(n_tiles):
            if copies:
                for cp in copies[1 + 3 * j:4 + 3 * j]:
                    cp.wait()
            u = u_ref[...]
            gate = _dot(u, wi_ref[:, j * ff_tile:(j + 1) * ff_tile])
            up = _dot(u, wi_ref[:, d_ff + j * ff_tile:d_ff + (j + 1) * ff_tile])
            act = (gate * _sigmoid(gate) * up).astype(_BF16)
            o_ref[...] += _dot(act, wd_ref[j * ff_tile:(j + 1) * ff_tile, :])
        if final_norm:
            y = o_ref[...]
            o_ref[...] = y * _rms_scale(y) * fgain_ref[...]

    first_step = pl.program_id(0) == 0

    @pl.when(first_step)
    def _():
        body(weight_copies())

    @pl.when(jnp.logical_not(first_step))
    def _():
        body(None)


def _block_tail(a, w_out, h, gain, w_in, w_down, final_gain, *, layer, final_norm, ff_tile=256):
    n_slabs, t, kw = a.shape
    d = h.shape[1]
    d_ff = w_down.shape[1]
    tm = ROW_TILE
    n_copies = 1 + 3 * (d_ff // ff_tile)
    in_hbm = pl.BlockSpec(memory_space=pl.ANY)
    return pl.pallas_call(
        functools.partial(_tail_kernel, layer=layer, final_norm=final_norm, ff_tile=ff_tile),
        grid=(t // tm,),
        in_specs=[
            pl.BlockSpec((n_slabs, tm, kw), lambda i: (0, i, 0)),
            in_hbm,
            pl.BlockSpec((tm, d), lambda i: (i, 0)),
            _resident((1, d)),
            in_hbm,
            in_hbm,
            _resident((1, d)),
        ],
        out_specs=pl.BlockSpec((tm, d), lambda i: (i, 0)),
        out_shape=jax.ShapeDtypeStruct((t, d), _F32),
        scratch_shapes=[
            pltpu.VMEM((tm, d), _BF16),
            pltpu.VMEM(w_out.shape, w_out.dtype),
            pltpu.VMEM(w_in.shape[1:], w_in.dtype),
            pltpu.VMEM(w_down.shape[1:], w_down.dtype),
            pltpu.SemaphoreType.DMA((n_copies,)),
        ],
        compiler_params=_params("arbitrary"),
        name="block_tail",
    )(a, w_out, h, gain.reshape(1, d), w_in, w_down, final_gain.reshape(1, d))


def _attn_kernel(*refs):
    ins = refs[:15]
    out_ref, o_scr, l_scr = refs[15:]
    first_block = pl.program_id(1) == 0
    row = lax.broadcasted_iota(jnp.int32, (SPAN, SPAN), 0)
    col = lax.broadcasted_iota(jnp.int32, (SPAN, SPAN), 1)
    bias_cur = jnp.where(col <= row, 0.0, NEG_BIG).astype(_F32)
    bias_prev = jnp.where(col >= row, 0.0, NEG_BIG).astype(_F32)
    bias_halo = bias_prev + jnp.where(first_block, NEG_BIG, 0.0).astype(_F32)
    ones = jnp.ones((SPAN, HEAD_DIM), _BF16)
    chain_len = 4
    body_units = 16

    for g, dil in enumerate(DILATIONS):
        q_ref, k_ref, v_ref, kh_ref, vh_ref = ins[5 * g:5 * g + 5]

        def load(ref, start, dil=dil):
            idx = pl.ds(start, SPAN) if dil == 1 else pl.ds(start, SPAN, stride=dil)
            return ref[idx, :].astype(_BF16)

        def load_kv(kref, vref, start, load=load):
            return load(kref, start), jnp.concatenate([load(vref, start), ones], axis=1)

        def chain(starts, prev, prev_bias, g=g, dil=dil, q_ref=q_ref, k_ref=k_ref, v_ref=v_ref,
                  load=load, load_kv=load_kv):
            for start in starts:
                kp, vp = prev
                kc, vc = cur = load_kv(k_ref, v_ref, start)
                s = _dot_nt(load(q_ref, start), jnp.concatenate([kp, kc], axis=0))
                s = s + jnp.concatenate([prev_bias, bias_cur], axis=1)
                m = jnp.max(jnp.maximum(s[:, :SPAN], s[:, SPAN:]), axis=-1, keepdims=True)
                p = jnp.exp2(s - m).astype(_BF16)
                r = _dot(p, jnp.concatenate([vp, vc], axis=0))
                denom = r[:, HEAD_DIM:]
                idx = pl.ds(start, SPAN) if dil == 1 else pl.ds(start, SPAN, stride=dil)
                o_scr[g, idx, :] = r[:, :HEAD_DIM] / denom
                l_scr[g, idx, :] = m + jnp.log2(denom)
                prev, prev_bias = cur, bias_prev

        block_rows = SPAN * dil
        if dil == DILATIONS[-1]:
            def body(i, carry, chain=chain, load_kv=load_kv, kh_ref=kh_ref, vh_ref=vh_ref):
                for u in range(body_units):
                    r = i * body_units + u
                    chain([r], load_kv(kh_ref, vh_ref, r), bias_halo)
                return carry
            lax.fori_loop(0, dil // body_units, body, 0)
        elif dil > 1:
            assert DILATIONS[-1] // dil == chain_len

            def body(i, carry, chain=chain, load_kv=load_kv, kh_ref=kh_ref, vh_ref=vh_ref,
                     block_rows=block_rows):
                for u in range(body_units // chain_len):
                    r = i * (body_units // chain_len) + u
                    chain([r + b * block_rows for b in range(chain_len)],
                          load_kv(kh_ref, vh_ref, r), bias_halo)
                return carry
            lax.fori_loop(0, dil * chain_len // body_units, body, 0)
        else:
            for first in range(0, DILATIONS[-1], chain_len):
                prev = (load_kv(kh_ref, vh_ref, 0) if first == 0
                        else load_kv(k_ref, v_ref, (first - 1) * block_rows))
                chain([(first + b) * block_rows for b in range(chain_len)], prev,
                      bias_halo if first == 0 else bias_prev)

    merge_rows = 256

    def merge(ci, carry):
        rr = pl.ds(pl.multiple_of(ci * merge_rows, merge_rows), merge_rows)
        l0, l1, l2 = l_scr[0, rr, :], l_scr[1, rr, :], l_scr[2, rr, :]
        m = jnp.maximum(jnp.maximum(l0, l1), l2)
        e0, e1, e2 = jnp.exp2(l0 - m), jnp.exp2(l1 - m), jnp.exp2(l2 - m)
        inv = 1.0 / (e0 + e1 + e2)
        out_ref[0, rr, :] = (o_scr[0, rr, :] * (e0 * inv)).astype(out_ref.dtype)
        out_ref[1, rr, :] = (o_scr[1, rr, :] * (e1 * inv)).astype(out_ref.dtype)
        out_ref[2, rr, :] = (o_scr[2, rr, :] * (e2 * inv)).astype(out_ref.dtype)
        return carry

    lax.fori_loop(0, ATTN_BLOCK // merge_rows, merge, 0)


def _attention(qkv, *, batch):
    t = qkv.shape[0]
    tb = ATTN_BLOCK
    steps = t // batch // tb
    in_specs, operands = [], []
    for g, dil in enumerate(DILATIONS):
        halo = SPAN * dil
        ratio = tb // halo

        def cur(which, g=g):
            return pl.BlockSpec(
                (tb, HEAD_DIM),
                lambda b, i, j: (b * steps + i, which * ATTN_HEADS + g * HEADS_PER_GROUP + j))

        def prev(which, g=g, ratio=ratio, halo=halo):
            return pl.BlockSpec(
                (halo, HEAD_DIM),
                lambda b, i, j: (jnp.maximum((b * steps + i) * ratio - 1, 0),
                                 which * ATTN_HEADS + g * HEADS_PER_GROUP + j))

        in_specs += [cur(0), cur(1), cur(2), prev(1), prev(2)]
        operands += [qkv] * 5
    n_groups = len(DILATIONS)
    return pl.pallas_call(
        _attn_kernel,
        grid=(batch, steps, HEADS_PER_GROUP),
        in_specs=in_specs,
        out_specs=pl.BlockSpec((n_groups, tb, HEAD_DIM), lambda b, i, j: (0, b * steps + i, j)),
        out_shape=jax.ShapeDtypeStruct((n_groups, t, HEADS_PER_GROUP * HEAD_DIM), _BF16),
        scratch_shapes=[pltpu.VMEM((3, tb, HEAD_DIM), _F32), pltpu.VMEM((3, tb, HEAD_DIM), _F32)],
        compiler_params=_params("parallel", "arbitrary", "arbitrary"),
        name="dilated_attention",
    )(*operands)


def _rope_tables(seq_len):
    inv_freq = 1.0 / (ROPE_THETA ** (np.arange(0, HEAD_DIM, 2, dtype=np.float64) / HEAD_DIM))
    ang = np.arange(seq_len, dtype=np.float64)[:, None] * inv_freq[None, :]
    cos, sin = np.cos(ang), np.sin(ang)
    return (jnp.asarray(np.concatenate([cos, cos], axis=-1), _F32),
            jnp.asarray(np.concatenate([-sin, sin], axis=-1), _F32))


def kernel(x, norm_mix, norm_ffn, hgrn_w_in, hgrn_lb_logits, hgrn_out_norm, hgrn_w_out,
           attn_w_qkv, attn_w_out, ffn_w_in, ffn_w_down, final_norm):
    batch, seq, d = x.shape
    cos2, sin2 = _rope_tables(seq)
    h = x.reshape(batch * seq, d)

    gated = _hgrn_mixer(h, norm_mix[0], hgrn_w_in[0], hgrn_lb_logits, hgrn_out_norm[0],
                        batch=batch, layer=0)
    ffn_in, ffn_down = ffn_w_in, ffn_w_down
    h = _block_tail(gated[None], hgrn_w_out[0], h, norm_ffn[0], ffn_in, ffn_down, final_norm,
                    layer=0, final_norm=False)

    qkv = _qkv_projection(h, norm_mix[1], attn_w_qkv[0], cos2, sin2,
                          scale=HEAD_DIM ** -0.5 * LOG2_E)
    attn = _attention(qkv, batch=batch)
    h = _block_tail(attn, attn_w_out[0], h, norm_ffn[1], ffn_in, ffn_down, final_norm,
                    layer=1, final_norm=True)
    return h.reshape(batch, seq, d)
```

```python
import functools

import numpy as np
import jax
import jax.numpy as jnp
from jax import lax
from jax.experimental import pallas as pl
from jax.experimental.pallas import tpu as pltpu

D_MODEL = 1024
HEAD_DIM = 128
HGRN_HEADS = 8
HGRN_CHUNK = 64
HGRN_LEVELS = 6
HGRN_STEP_TOKENS = 1024
ATTN_HEADS = 12
ATTN_WIDTH = ATTN_HEADS * HEAD_DIM
DILATIONS = (1, 4, 16)
SPAN = 128
HEADS_PER_GROUP = 4
ATTN_BLOCK = SPAN * DILATIONS[-1]
ROW_TILE = 512
COL_TILE = 512
ROPE_THETA = 10000.0
NORM_EPS = 1e-6
NEG_BIG = -1e30
LOG2_E = float(np.log2(np.e))
VMEM_LIMIT_BYTES = 56 * 1024 * 1024

_F32 = jnp.float32
_BF16 = jnp.bfloat16


def _dot(a, b):
    return lax.dot_general(a, b, (((1,), (0,)), ((), ())), preferred_element_type=_F32)


def _dot_nt(a, b):
    return lax.dot_general(a, b, (((1,), (1,)), ((), ())), preferred_element_type=_F32)


def _dot_tn(a, b):
    return lax.dot_general(a, b, (((0,), (0,)), ((), ())), preferred_element_type=_F32)


def _sigmoid(x):
    return 1.0 / (1.0 + jnp.exp2(x * (-LOG2_E)))


def _rms_scale(x):
    return lax.rsqrt(jnp.mean(x * x, axis=-1, keepdims=True) + NORM_EPS)


def _params(*sem):
    return pltpu.CompilerParams(dimension_semantics=sem, vmem_limit_bytes=VMEM_LIMIT_BYTES)


def _resident(shape, layer=None):
    if layer is None:
        index = (0,) * len(shape)
    else:
        index = (layer,) + (0,) * (len(shape) - 1)
        shape = (None,) + tuple(shape[1:])
    return pl.BlockSpec(shape, lambda *_: index, pipeline_mode=pl.Buffered(1))


def _qkv_kernel(h_ref, gain_ref, w_ref, cos_ref, sin_ref, o_ref, *, scale):
    x = h_ref[...]
    u = (x * _rms_scale(x) * gain_ref[...]).astype(_BF16)
    cos, sin = cos_ref[...], sin_ref[...]
    cos_q, sin_q = cos * scale, sin * scale
    for j in range(w_ref.shape[1] // COL_TILE):
        res = _dot(u, w_ref[:, j * COL_TILE:(j + 1) * COL_TILE])
        for hh in range(COL_TILE // HEAD_DIM):
            lo = j * COL_TILE + hh * HEAD_DIM
            xh = res[:, hh * HEAD_DIM:(hh + 1) * HEAD_DIM]
            if lo < ATTN_WIDTH:
                xh = xh * cos_q + pltpu.roll(xh, HEAD_DIM // 2, 1) * sin_q
            elif lo < 2 * ATTN_WIDTH:
                xh = xh * cos + pltpu.roll(xh, HEAD_DIM // 2, 1) * sin
            o_ref[:, lo:lo + HEAD_DIM] = xh


def _qkv_projection(h, gain, w, cos2, sin2, *, scale):
    t, d = h.shape
    n = w.shape[1]
    tm = ROW_TILE
    seq_tiles = cos2.shape[0] // tm
    return pl.pallas_call(
        functools.partial(_qkv_kernel, scale=scale),
        grid=(t // tm,),
        in_specs=[
            pl.BlockSpec((tm, d), lambda i: (i, 0)),
            _resident((1, d)),
            _resident((d, n)),
            pl.BlockSpec((tm, HEAD_DIM), lambda i: (i % seq_tiles, 0)),
            pl.BlockSpec((tm, HEAD_DIM), lambda i: (i % seq_tiles, 0)),
        ],
        out_specs=pl.BlockSpec((tm, n), lambda i: (i, 0)),
        out_shape=jax.ShapeDtypeStruct((t, n), _F32),
        compiler_params=_params("parallel"),
        name="qkv_projection",
    )(h, gain.reshape(1, d), w, cos2, sin2)


def _hgrn_tables():
    c = HGRN_CHUNK
    assert 1 << HGRN_LEVELS == c
    t = np.arange(c)
    col = t[None, :]
    row = t[:, None]
    sums = np.zeros((HGRN_LEVELS + 1, c, c), np.float32)
    sums[0] = col <= row
    sums[1] = col > row
    masks = np.zeros((HGRN_LEVELS + 1, c, c), np.float32)
    for level in range(HGRN_LEVELS):
        half = c >> (level + 1)
        block = t // (2 * half)
        mid = block * 2 * half + half
        is_query = t >= mid
        if level < HGRN_LEVELS - 1:
            q_rows = (col >= mid[:, None]) & (col <= row) & is_query[:, None]
            k_rows = (col > row) & (col < mid[:, None]) & (~is_query)[:, None]
            sums[2 + level] = q_rows | k_rows
        masks[level] = ((block[:, None] == block[None, :]) & is_query[:, None]
                        & (~is_query)[None, :])
    masks[HGRN_LEVELS] = np.eye(c)
    assert np.array_equal(masks.sum(0), np.tril(np.ones((c, c))))
    sums = sums.reshape((HGRN_LEVELS + 1) * c, c)
    return np.concatenate([sums, sums], axis=1), masks


def _hgrn_kernel(h_ref, ngain_ref, w_ref, lbl_ref, gain_ref, sums_ref, masks_ref, o_ref,
                 state_ref, proj_ref, a2_ref, b2_ref, v2_ref, decay_ref, s_ref, *, layer):
    c = HGRN_CHUNK
    width = HGRN_HEADS * HEAD_DIM

    @pl.when(pl.program_id(1) == 0)
    def _():
        state_ref[...] = jnp.zeros_like(state_ref)

    x = h_ref[...]
    u = (x * _rms_scale(x) * ngain_ref[...]).astype(_BF16)
    for j in range(w_ref.shape[1] // COL_TILE):
        cols = slice(j * COL_TILE, (j + 1) * COL_TILE)
        proj_ref[:, cols] = _dot(u, w_ref[:, cols])

    logits = lbl_ref[...]
    e = jnp.exp(logits - jnp.max(logits, axis=0, keepdims=True))
    lb = jnp.sum(e[:layer + 1], axis=0, keepdims=True) / jnp.sum(e, axis=0, keepdims=True)
    out_gain = gain_ref[...]

    n_chunks = h_ref.shape[0] // c

    def chunk_rows(ci):
        return pl.ds(pl.multiple_of(ci * c, c), c)

    def prepare(ci, slot):
        a_ref, b_ref, v_ref = a2_ref.at[slot], b2_ref.at[slot], v2_ref.at[slot]
        rows = chunk_rows(ci)
        q = proj_ref[rows, 0:width]
        forget = lb + (1.0 - lb) * _sigmoid(proj_ref[rows, width:2 * width])
        glog = jnp.log(forget) * LOG2_E
        kk = 1.0 - forget
        qq = q * _sigmoid(q)
        g_hi = glog.astype(_BF16)
        g_lo = (glog - g_hi.astype(_F32)).astype(_BF16)
        factors = jnp.exp2(_dot(sums_ref[...], jnp.concatenate([g_hi, g_lo], axis=0)))

        qq = qq.astype(_BF16)
        kk = kk.astype(_BF16)
        from_start = factors[0:c]
        a_ref[0] = qq * from_start.astype(_BF16)
        decay_ref[slot] = from_start[c - 1:c, :]
        b_ref[0] = kk * factors[c:2 * c].astype(_BF16)
        for level in range(HGRN_LEVELS - 1):
            fac = factors[(2 + level) * c:(3 + level) * c].astype(_BF16)
            a_ref[1 + level] = qq * fac
            b_ref[1 + level] = kk * fac
        a_ref[HGRN_LEVELS] = qq * forget.astype(_BF16)
        a_ref[HGRN_LEVELS + 1] = qq
        b_ref[HGRN_LEVELS] = kk
        v_ref[...] = proj_ref[rows, 2 * width:3 * width].astype(_BF16)

    def lanes(h):
        return slice(h * HEAD_DIM, (h + 1) * HEAD_DIM)

    def consume(ci, slot):
        a_ref, b_ref, v_ref = a2_ref.at[slot], b2_ref.at[slot], v2_ref.at[slot]
        rows = chunk_rows(ci)
        chunk_decay = decay_ref[slot]
        pairs = [(h, h + 1) for h in range(0, HGRN_HEADS, 2)]
        for pair in pairs:
            scores = [jnp.zeros((c, c), _F32) for _ in pair]
            for level in range(HGRN_LEVELS - 1):
                for i, h in enumerate(pair):
                    scores[i] += masks_ref[level] * _dot_nt(a_ref[1 + level, :, lanes(h)],
                                                            b_ref[1 + level, :, lanes(h)])
            for i, h in enumerate(pair):
                both = a_ref[HGRN_LEVELS:HGRN_LEVELS + 2, :, lanes(h)].reshape(2 * c, HEAD_DIM)
                prod = _dot_nt(both, b_ref[HGRN_LEVELS, :, lanes(h)])
                scores[i] += (masks_ref[HGRN_LEVELS - 1] * prod[:c]
                              + masks_ref[HGRN_LEVELS] * prod[c:])
                s_ref[h] = scores[i].astype(_BF16)
        for pair in pairs:
            states = [state_ref[h] for h in pair]
            intra = [_dot(s_ref[h], v_ref[:, lanes(h)]) for h in pair]
            inter = [_dot(a_ref[0, :, lanes(h)], states[i].astype(_BF16))
                     for i, h in enumerate(pair)]
            update = [_dot_tn(b_ref[0, :, lanes(h)], v_ref[:, lanes(h)]) for h in pair]
            for i, h in enumerate(pair):
                decay_col = jnp.transpose(
                    jnp.broadcast_to(chunk_decay[:, lanes(h)], (8, HEAD_DIM)))[:, 0:1]
                state_ref[h] = states[i] * decay_col + update[i]
                o = intra[i] + inter[i]
                o = o * _rms_scale(o) * out_gain
                gate = proj_ref[rows, 3 * width + h * HEAD_DIM:3 * width + (h + 1) * HEAD_DIM]
                o_ref[rows, lanes(h)] = (o * (gate * _sigmoid(gate))).astype(o_ref.dtype)

    def two_chunks(i, carry):
        first = 2 * i
        prepare(first + 1, 1)
        consume(first, 0)
        prepare(first + 2, 0)
        consume(first + 1, 1)
        return carry

    prepare(0, 0)
    lax.fori_loop(0, n_chunks // 2 - 1, two_chunks, 0)
    prepare(n_chunks - 1, 1)
    consume(n_chunks - 2, 0)
    consume(n_chunks - 1, 1)


def _hgrn_mixer(h, norm_gain, w_in, lb_logits, out_gain, *, batch, layer):
    t, d = h.shape
    width = HGRN_HEADS * HEAD_DIM
    tc = HGRN_STEP_TOKENS
    steps = t // batch // tc
    sums, masks = _hgrn_tables()
    c = HGRN_CHUNK
    return pl.pallas_call(
        functools.partial(_hgrn_kernel, layer=layer),
        grid=(batch, steps),
        in_specs=[
            pl.BlockSpec((tc, d), lambda b, s: (b * steps + s, 0)),
            _resident((1, d)),
            _resident(w_in.shape),
            _resident(lb_logits.shape),
            _resident((1, HEAD_DIM)),
            _resident(sums.shape),
            _resident(masks.shape),
        ],
        out_specs=pl.BlockSpec((tc, width), lambda b, s: (b * steps + s, 0)),
        out_shape=jax.ShapeDtypeStruct((t, width), _BF16),
        scratch_shapes=[
            pltpu.VMEM((HGRN_HEADS, HEAD_DIM, HEAD_DIM), _F32),
            pltpu.VMEM((tc, 4 * width), _F32),
            pltpu.VMEM((2, HGRN_LEVELS + 2, c, width), _BF16),
            pltpu.VMEM((2, HGRN_LEVELS + 1, c, width), _BF16),
            pltpu.VMEM((2, c, width), _BF16),
            pltpu.VMEM((2, 1, width), _F32),
            pltpu.VMEM((HGRN_HEADS, c, c), _BF16),
        ],
        compiler_params=_params("parallel", "arbitrary"),
        name="hgrn_mixer",
    )(h, norm_gain.reshape(1, d), w_in, lb_logits, out_gain.reshape(1, HEAD_DIM),
      jnp.asarray(sums, _BF16), jnp.asarray(masks, _F32))


def _tail_kernel(a_ref, wo_hbm, h_ref, gain_ref, wi_hbm, wd_hbm, fgain_ref, o_ref,
                 u_ref, wo_ref, wi_ref, wd_ref, sem_ref, *, layer, final_norm, ff_tile):
    n_slabs, _, kw = a_ref.shape
    d_ff = wd_ref.shape[0]
    n_tiles = d_ff // ff_tile

    def weight_copies():
        copies = [pltpu.make_async_copy(wo_hbm, wo_ref, sem_ref.at[0])]
        for j in range(n_tiles):
            for half in range(2):
                cols = pl.ds(half * d_ff + j * ff_tile, ff_tile)
                copies.append(pltpu.make_async_copy(wi_hbm.at[layer, :, cols], wi_ref.at[:, cols],
                                                    sem_ref.at[len(copies)]))
            rows = pl.ds(j * ff_tile, ff_tile)
            copies.append(pltpu.make_async_copy(wd_hbm.at[layer, rows, :], wd_ref.at[rows, :],
                                                sem_ref.at[len(copies)]))
        return copies

    def body(copies):
        if copies:
            for cp in copies:
                cp.start()
            copies[0].wait()
        mixed = h_ref[...]
        for s in range(n_slabs):
            mixed += _dot(a_ref[s], wo_ref[s * kw:(s + 1) * kw, :])
        o_ref[...] = mixed
        u_ref[...] = (mixed * _rms_scale(mixed) * gain_ref[...]).astype(_BF16)
        for j in range(n_tiles):
            if copies:
                for cp in copies[1 + 3 * j:4 + 3 * j]:
                    cp.wait()
            u = u_ref[...]
            gate = _dot(u, wi_ref[:, j * ff_tile:(j + 1) * ff_tile])
            up = _dot(u, wi_ref[:, d_ff + j * ff_tile:d_ff + (j + 1) * ff_tile])
            act = (gate * _sigmoid(gate) * up).astype(_BF16)
            o_ref[...] += _dot(act, wd_ref[j * ff_tile:(j + 1) * ff_tile, :])
        if final_norm:
            y = o_ref[...]
            o_ref[...] = y * _rms_scale(y) * fgain_ref[...]

    first_step = pl.program_id(0) == 0

    @pl.when(first_step)
    def _():
        body(weight_copies())

    @pl.when(jnp.logical_not(first_step))
    def _():
        body(None)


def _block_tail(a, w_out, h, gain, w_in, w_down, final_gain, *, layer, final_norm, ff_tile=256):
    n_slabs, t, kw = a.shape
    d = h.shape[1]
    d_ff = w_down.shape[1]
    tm = ROW_TILE
    n_copies = 1 + 3 * (d_ff // ff_tile)
    in_hbm = pl.BlockSpec(memory_space=pl.ANY)
    return pl.pallas_call(
        functools.partial(_tail_kernel, layer=layer, final_norm=final_norm, ff_tile=ff_tile),
        grid=(t // tm,),
        in_specs=[
            pl.BlockSpec((n_slabs, tm, kw), lambda i: (0, i, 0)),
            in_hbm,
            pl.BlockSpec((tm, d), lambda i: (i, 0)),
            _resident((1, d)),
            in_hbm,
            in_hbm,
            _resident((1, d)),
        ],
        out_specs=pl.BlockSpec((tm, d), lambda i: (i, 0)),
        out_shape=jax.ShapeDtypeStruct((t, d), _F32),
        scratch_shapes=[
            pltpu.VMEM((tm, d), _BF16),
            pltpu.VMEM(w_out.shape, w_out.dtype),
            pltpu.VMEM(w_in.shape[1:], w_in.dtype),
            pltpu.VMEM(w_down.shape[1:], w_down.dtype),
            pltpu.SemaphoreType.DMA((n_copies,)),
        ],
        compiler_params=_params("arbitrary"),
        name="block_tail",
    )(a, w_out, h, gain.reshape(1, d), w_in, w_down, final_gain.reshape(1, d))


def _attn_kernel(*refs):
    ins = refs[:15]
    out_ref, o_scr, l_scr = refs[15:]
    first_block = pl.program_id(1) == 0
    row = lax.broadcasted_iota(jnp.int32, (SPAN, SPAN), 0)
    col = lax.broadcasted_iota(jnp.int32, (SPAN, SPAN), 1)
    bias_cur = jnp.where(col <= row, 0.0, NEG_BIG).astype(_F32)
    bias_prev = jnp.where(col >= row, 0.0, NEG_BIG).astype(_F32)
    bias_halo = bias_prev + jnp.where(first_block, NEG_BIG, 0.0).astype(_F32)
    ones = jnp.ones((SPAN, HEAD_DIM), _BF16)
    chain_len = 4
    body_units = 16

    for g, dil in enumerate(DILATIONS):
        q_ref, k_ref, v_ref, kh_ref, vh_ref = ins[5 * g:5 * g + 5]

        def load(ref, start, dil=dil):
            idx = pl.ds(start, SPAN) if dil == 1 else pl.ds(start, SPAN, stride=dil)
            return ref[idx, :].astype(_BF16)

        def load_kv(kref, vref, start, load=load):
            return load(kref, start), jnp.concatenate([load(vref, start), ones], axis=1)

        def chain(starts, prev, prev_bias, g=g, dil=dil, q_ref=q_ref, k_ref=k_ref, v_ref=v_ref,
                  load=load, load_kv=load_kv):
            for start in starts:
                kp, vp = prev
                kc, vc = cur = load_kv(k_ref, v_ref, start)
                s = _dot_nt(load(q_ref, start), jnp.concatenate([kp, kc], axis=0))
                s = s + jnp.concatenate([prev_bias, bias_cur], axis=1)
                m = jnp.max(jnp.maximum(s[:, :SPAN], s[:, SPAN:]), axis=-1, keepdims=True)
                p = jnp.exp2(s - m).astype(_BF16)
                r = _dot(p, jnp.concatenate([vp, vc], axis=0))
                denom = r[:, HEAD_DIM:]
                idx = pl.ds(start, SPAN) if dil == 1 else pl.ds(start, SPAN, stride=dil)
                o_scr[g, idx, :] = r[:, :HEAD_DIM] / denom
                l_scr[g, idx, :] = m + jnp.log2(denom)
                prev, prev_bias = cur, bias_prev

        block_rows = SPAN * dil
        if dil == DILATIONS[-1]:
            def body(i, carry, chain=chain, load_kv=load_kv, kh_ref=kh_ref, vh_ref=vh_ref):
                for u in range(body_units):
                    r = i * body_units + u
                    chain([r], load_kv(kh_ref, vh_ref, r), bias_halo)
                return carry
            lax.fori_loop(0, dil // body_units, body, 0)
        elif dil > 1:
            assert DILATIONS[-1] // dil == chain_len

            def body(i, carry, chain=chain, load_kv=load_kv, kh_ref=kh_ref, vh_ref=vh_ref,
                     block_rows=block_rows):
                for u in range(body_units // chain_len):
                    r = i * (body_units // chain_len) + u
                    chain([r + b * block_rows for b in range(chain_len)],
                          load_kv(kh_ref, vh_ref, r), bias_halo)
                return carry
            lax.fori_loop(0, dil * chain_len // body_units, body, 0)
        else:
            for first in range(0, DILATIONS[-1], chain_len):
                prev = (load_kv(kh_ref, vh_ref, 0) if first == 0
                        else load_kv(k_ref, v_ref, (first - 1) * block_rows))
                chain([(first + b) * block_rows for b in range(chain_len)], prev,
                      bias_halo if first == 0 else bias_prev)

    merge_rows = 256

    def merge(ci, carry):
        rr = pl.ds(pl.multiple_of(ci * merge_rows, merge_rows), merge_rows)
        l0, l1, l2 = l_scr[0, rr, :], l_scr[1, rr, :], l_scr[2, rr, :]
        m = jnp.maximum(jnp.maximum(l0, l1), l2)
        e0, e1, e2 = jnp.exp2(l0 - m), jnp.exp2(l1 - m), jnp.exp2(l2 - m)
        inv = 1.0 / (e0 + e1 + e2)
        out_ref[0, rr, :] = (o_scr[0, rr, :] * (e0 * inv)).astype(out_ref.dtype)
        out_ref[1, rr, :] = (o_scr[1, rr, :] * (e1 * inv)).astype(out_ref.dtype)
        out_ref[2, rr, :] = (o_scr[2, rr, :] * (e2 * inv)).astype(out_ref.dtype)
        return carry

    lax.fori_loop(0, ATTN_BLOCK // merge_rows, merge, 0)


def _attention(qkv, *, batch):
    t = qkv.shape[0]
    tb = ATTN_BLOCK
    steps = t // batch // tb
    in_specs, operands = [], []
    for g, dil in enumerate(DILATIONS):
        halo = SPAN * dil
        ratio = tb // halo

        def cur(which, g=g):
            return pl.BlockSpec(
                (tb, HEAD_DIM),
                lambda b, i, j: (b * steps + i, which * ATTN_HEADS + g * HEADS_PER_GROUP + j))

        def prev(which, g=g, ratio=ratio, halo=halo):
            return pl.BlockSpec(
                (halo, HEAD_DIM),
                lambda b, i, j: (jnp.maximum((b * steps + i) * ratio - 1, 0),
                                 which * ATTN_HEADS + g * HEADS_PER_GROUP + j))

        in_specs += [cur(0), cur(1), cur(2), prev(1), prev(2)]
        operands += [qkv] * 5
    n_groups = len(DILATIONS)
    return pl.pallas_call(
        _attn_kernel,
        grid=(batch, steps, HEADS_PER_GROUP),
        in_specs=in_specs,
        out_specs=pl.BlockSpec((n_groups, tb, HEAD_DIM), lambda b, i, j: (0, b * steps + i, j)),
        out_shape=jax.ShapeDtypeStruct((n_groups, t, HEADS_PER_GROUP * HEAD_DIM), _BF16),
        scratch_shapes=[pltpu.VMEM((3, tb, HEAD_DIM), _F32), pltpu.VMEM((3, tb, HEAD_DIM), _F32)],
        compiler_params=_params("parallel", "arbitrary", "arbitrary"),
        name="dilated_attention",
    )(*operands)


def _rope_tables(seq_len):
    inv_freq = 1.0 / (ROPE_THETA ** (np.arange(0, HEAD_DIM, 2, dtype=np.float64) / HEAD_DIM))
    ang = np.arange(seq_len, dtype=np.float64)[:, None] * inv_freq[None, :]
    cos, sin = np.cos(ang), np.sin(ang)
    return (jnp.asarray(np.concatenate([cos, cos], axis=-1), _F32),
            jnp.asarray(np.concatenate([-sin, sin], axis=-1), _F32))


def kernel(x, norm_mix, norm_ffn, hgrn_w_in, hgrn_lb_logits, hgrn_out_norm, hgrn_w_out,
           attn_w_qkv, attn_w_out, ffn_w_in, ffn_w_down, final_norm):
    batch, seq, d = x.shape
    cos2, sin2 = _rope_tables(seq)
    h = x.reshape(batch * seq, d)

    gated = _hgrn_mixer(h, norm_mix[0], hgrn_w_in[0], hgrn_lb_logits, hgrn_out_norm[0],
                        batch=batch, layer=0)
    ffn_in, ffn_down = ffn_w_in, ffn_w_down
    h = _block_tail(gated[None], hgrn_w_out[0], h, norm_ffn[0], ffn_in, ffn_down, final_norm,
                    layer=0, final_norm=False)

    qkv = _qkv_projection(h, norm_mix[1], attn_w_qkv[0], cos2, sin2,
                          scale=HEAD_DIM ** -0.5 * LOG2_E)
    attn = _attention(qkv, batch=batch)
    h = _block_tail(attn, attn_w_out[0], h, norm_ffn[1], ffn_in, ffn_down, final_norm,
                    layer=1, final_norm=True)
    return h.reshape(batch, seq, d)
```

```python
import functools

import numpy as np
import jax
import jax.numpy as jnp
from jax import lax
from jax.experimental import pallas as pl
from jax.experimental.pallas import tpu as pltpu

D_MODEL = 1024
HEAD_DIM = 128
HGRN_HEADS = 8
HGRN_CHUNK = 64
HGRN_LEVELS = 6
HGRN_STEP_TOKENS = 1024
ATTN_HEADS = 12
ATTN_WIDTH = ATTN_HEADS * HEAD_DIM
DILATIONS = (1, 4, 16)
SPAN = 128
HEADS_PER_GROUP = 4
ATTN_BLOCK = SPAN * DILATIONS[-1]
ROW_TILE = 512
COL_TILE = 512
ROPE_THETA = 10000.0
NORM_EPS = 1e-6
NEG_BIG = -1e30
LOG2_E = float(np.log2(np.e))
VMEM_LIMIT_BYTES = 56 * 1024 * 1024

_F32 = jnp.float32
_BF16 = jnp.bfloat16


def _dot(a, b):
    return lax.dot_general(a, b, (((1,), (0,)), ((), ())), preferred_element_type=_F32)


def _dot_nt(a, b):
    return lax.dot_general(a, b, (((1,), (1,)), ((), ())), preferred_element_type=_F32)


def _dot_tn(a, b):
    return lax.dot_general(a, b, (((0,), (0,)), ((), ())), preferred_element_type=_F32)


def _sigmoid(x):
    return 1.0 / (1.0 + jnp.exp2(x * (-LOG2_E)))


def _rms_scale(x):
    return lax.rsqrt(jnp.mean(x * x, axis=-1, keepdims=True) + NORM_EPS)


def _params(*sem):
    return pltpu.CompilerParams(dimension_semantics=sem, vmem_limit_bytes=VMEM_LIMIT_BYTES)


def _resident(shape, layer=None):
    if layer is None:
        index = (0,) * len(shape)
    else:
        index = (layer,) + (0,) * (len(shape) - 1)
        shape = (None,) + tuple(shape[1:])
    return pl.BlockSpec(shape, lambda *_: index, pipeline_mode=pl.Buffered(1))


def _qkv_kernel(h_ref, gain_ref, w_ref, cos_ref, sin_ref, *refs, scale):
    out_refs, stage_ref = refs[:-1], refs[-1]
    group_width = HEADS_PER_GROUP * HEAD_DIM
    assert COL_TILE == group_width
    x = h_ref[...]
    u = (x * _rms_scale(x) * gain_ref[...]).astype(_BF16)
    cos, sin = cos_ref[...], sin_ref[...]
    cos_q, sin_q = cos * scale, sin * scale
    n_groups = len(DILATIONS)
    tiles = sorted(range(w_ref.shape[1] // COL_TILE), key=lambda j: -(j % n_groups))
    for j in tiles:
        which, g = divmod(j, n_groups)
        dil = DILATIONS[g]
        o_ref = out_refs[g]
        res = _dot(u, w_ref[:, j * COL_TILE:(j + 1) * COL_TILE])
        for hh in range(HEADS_PER_GROUP):
            lo = which * group_width + hh * HEAD_DIM
            xh = res[:, hh * HEAD_DIM:(hh + 1) * HEAD_DIM]
            if which == 0:
                xh = xh * cos_q + pltpu.roll(xh, HEAD_DIM // 2, 1) * sin_q
            elif which == 1:
                xh = xh * cos + pltpu.roll(xh, HEAD_DIM // 2, 1) * sin
            if dil == 1:
                o_ref[0, :, lo:lo + HEAD_DIM] = xh.astype(_BF16)
            else:
                slab = stage_ref.at[j * HEADS_PER_GROUP + hh]
                slab[...] = xh
                rows = x.shape[0] // dil
                for r in range(dil):
                    o_ref[r, :, lo:lo + HEAD_DIM] = (
                        slab[pl.ds(r, rows, stride=dil), :].astype(_BF16))


def _qkv_projection(h, gain, w, cos2, sin2, *, scale):
    t, d = h.shape
    n = w.shape[1]
    tm = ROW_TILE
    seq_tiles = cos2.shape[0] // tm
    group_cols = 3 * HEADS_PER_GROUP * HEAD_DIM
    return pl.pallas_call(
        functools.partial(_qkv_kernel, scale=scale),
        grid=(t // tm,),
        in_specs=[
            pl.BlockSpec((tm, d), lambda i: (i, 0)),
            _resident((1, d)),
            _resident((d, n)),
            pl.BlockSpec((tm, HEAD_DIM), lambda i: (i % seq_tiles, 0)),
            pl.BlockSpec((tm, HEAD_DIM), lambda i: (i % seq_tiles, 0)),
        ],
        out_specs=[pl.BlockSpec((dil, tm // dil, group_cols), lambda i: (0, i, 0))
                   for dil in DILATIONS],
        out_shape=[jax.ShapeDtypeStruct((dil, t // dil, group_cols), _BF16) for dil in DILATIONS],
        scratch_shapes=[pltpu.VMEM((n // HEAD_DIM, tm, HEAD_DIM), _F32)],
        compiler_params=_params("parallel"),
        name="qkv_projection",
    )(h, gain.reshape(1, d), w, cos2, sin2)


def _hgrn_tables():
    c = HGRN_CHUNK
    assert 1 << HGRN_LEVELS == c
    t = np.arange(c)
    col = t[None, :]
    row = t[:, None]
    sums = np.zeros((HGRN_LEVELS + 1, c, c), np.float32)
    sums[0] = col <= row
    sums[1] = col > row
    masks = np.zeros((HGRN_LEVELS + 1, c, c), np.float32)
    for level in range(HGRN_LEVELS):
        half = c >> (level + 1)
        block = t // (2 * half)
        mid = block * 2 * half + half
        is_query = t >= mid
        if level < HGRN_LEVELS - 1:
            q_rows = (col >= mid[:, None]) & (col <= row) & is_query[:, None]
            k_rows = (col > row) & (col < mid[:, None]) & (~is_query)[:, None]
            sums[2 + level] = q_rows | k_rows
        masks[level] = ((block[:, None] == block[None, :]) & is_query[:, None]
                        & (~is_query)[None, :])
    masks[HGRN_LEVELS] = np.eye(c)
    assert np.array_equal(masks.sum(0), np.tril(np.ones((c, c))))
    sums = sums.reshape((HGRN_LEVELS + 1) * c, c)
    return np.concatenate([sums, sums], axis=1), masks


def _hgrn_kernel(h_ref, ngain_ref, w_ref, lbl_ref, gain_ref, sums_ref, masks_ref, o_ref,
                 state_ref, proj_ref, a2_ref, b2_ref, v2_ref, decay_ref, s_ref, *, layer):
    c = HGRN_CHUNK
    width = HGRN_HEADS * HEAD_DIM

    @pl.when(pl.program_id(1) == 0)
    def _():
        state_ref[...] = jnp.zeros_like(state_ref)

    x = h_ref[...]
    u = (x * _rms_scale(x) * ngain_ref[...]).astype(_BF16)
    for j in range(w_ref.shape[1] // COL_TILE):
        cols = slice(j * COL_TILE, (j + 1) * COL_TILE)
        proj_ref[:, cols] = _dot(u, w_ref[:, cols])

    logits = lbl_ref[...]
    e = jnp.exp(logits - jnp.max(logits, axis=0, keepdims=True))
    lb = jnp.sum(e[:layer + 1], axis=0, keepdims=True) / jnp.sum(e, axis=0, keepdims=True)
    out_gain = gain_ref[...]

    n_chunks = h_ref.shape[0] // c

    def chunk_rows(ci):
        return pl.ds(pl.multiple_of(ci * c, c), c)

    def prepare(ci, slot):
        a_ref, b_ref, v_ref = a2_ref.at[slot], b2_ref.at[slot], v2_ref.at[slot]
        rows = chunk_rows(ci)
        q = proj_ref[rows, 0:width]
        forget = lb + (1.0 - lb) * _sigmoid(proj_ref[rows, width:2 * width])
        glog = jnp.log(forget) * LOG2_E
        kk = 1.0 - forget
        qq = q * _sigmoid(q)
        g_hi = glog.astype(_BF16)
        g_lo = (glog - g_hi.astype(_F32)).astype(_BF16)
        factors = jnp.exp2(_dot(sums_ref[...], jnp.concatenate([g_hi, g_lo], axis=0)))

        qq = qq.astype(_BF16)
        kk = kk.astype(_BF16)
        from_start = factors[0:c]
        a_ref[0] = qq * from_start.astype(_BF16)
        decay_ref[slot] = from_start[c - 1:c, :]
        b_ref[0] = kk * factors[c:2 * c].astype(_BF16)
        for level in range(HGRN_LEVELS - 1):
            fac = factors[(2 + level) * c:(3 + level) * c].astype(_BF16)
            a_ref[1 + level] = qq * fac
            b_ref[1 + level] = kk * fac
        a_ref[HGRN_LEVELS] = qq * forget.astype(_BF16)
        a_ref[HGRN_LEVELS + 1] = qq
        b_ref[HGRN_LEVELS] = kk
        v_ref[...] = proj_ref[rows, 2 * width:3 * width].astype(_BF16)

    def lanes(h):
        return slice(h * HEAD_DIM, (h + 1) * HEAD_DIM)

    def consume(ci, slot):
        a_ref, b_ref, v_ref = a2_ref.at[slot], b2_ref.at[slot], v2_ref.at[slot]
        rows = chunk_rows(ci)
        chunk_decay = decay_ref[slot]
        pairs = [(h, h + 1) for h in range(0, HGRN_HEADS, 2)]
        for pair in pairs:
            scores = [jnp.zeros((c, c), _F32) for _ in pair]
            for level in range(HGRN_LEVELS - 1):
                for i, h in enumerate(pair):
                    scores[i] += masks_ref[level] * _dot_nt(a_ref[1 + level, :, lanes(h)],
                                                            b_ref[1 + level, :, lanes(h)])
            for i, h in enumerate(pair):
                both = a_ref[HGRN_LEVELS:HGRN_LEVELS + 2, :, lanes(h)].reshape(2 * c, HEAD_DIM)
                prod = _dot_nt(both, b_ref[HGRN_LEVELS, :, lanes(h)])
                scores[i] += (masks_ref[HGRN_LEVELS - 1] * prod[:c]
                              + masks_ref[HGRN_LEVELS] * prod[c:])
                s_ref[h] = scores[i].astype(_BF16)
        for pair in pairs:
            states = [state_ref[h] for h in pair]
            intra = [_dot(s_ref[h], v_ref[:, lanes(h)]) for h in pair]
            inter = [_dot(a_ref[0, :, lanes(h)], states[i].astype(_BF16))
                     for i, h in enumerate(pair)]
            update = [_dot_tn(b_ref[0, :, lanes(h)], v_ref[:, lanes(h)]) for h in pair]
            for i, h in enumerate(pair):
                decay_col = jnp.transpose(
                    jnp.broadcast_to(chunk_decay[:, lanes(h)], (8, HEAD_DIM)))[:, 0:1]
                state_ref[h] = states[i] * decay_col + update[i]
                o = intra[i] + inter[i]
                o = o * _rms_scale(o) * out_gain
                gate = proj_ref[rows, 3 * width + h * HEAD_DIM:3 * width + (h + 1) * HEAD_DIM]
                o_ref[rows, lanes(h)] = (o * (gate * _sigmoid(gate))).astype(o_ref.dtype)

    def two_chunks(i, carry):
        first = 2 * i
        prepare(first + 1, 1)
        consume(first, 0)
        prepare(first + 2, 0)
        consume(first + 1, 1)
        return carry

    prepare(0, 0)
    lax.fori_loop(0, n_chunks // 2 - 1, two_chunks, 0)
    prepare(n_chunks - 1, 1)
    consume(n_chunks - 2, 0)
    consume(n_chunks - 1, 1)


def _hgrn_mixer(h, norm_gain, w_in, lb_logits, out_gain, *, batch, layer):
    t, d = h.shape
    width = HGRN_HEADS * HEAD_DIM
    tc = HGRN_STEP_TOKENS
    steps = t // batch // tc
    sums, masks = _hgrn_tables()
    c = HGRN_CHUNK
    return pl.pallas_call(
        functools.partial(_hgrn_kernel, layer=layer),
        grid=(batch, steps),
        in_specs=[
            pl.BlockSpec((tc, d), lambda b, s: (b * steps + s, 0)),
            _resident((1, d)),
            _resident(w_in.shape),
            _resident(lb_logits.shape),
            _resident((1, HEAD_DIM)),
            _resident(sums.shape),
            _resident(masks.shape),
        ],
        out_specs=pl.BlockSpec((tc, width), lambda b, s: (b * steps + s, 0)),
        out_shape=jax.ShapeDtypeStruct((t, width), _BF16),
        scratch_shapes=[
            pltpu.VMEM((HGRN_HEADS, HEAD_DIM, HEAD_DIM), _F32),
            pltpu.VMEM((tc, 4 * width), _F32),
            pltpu.VMEM((2, HGRN_LEVELS + 2, c, width), _BF16),
            pltpu.VMEM((2, HGRN_LEVELS + 1, c, width), _BF16),
            pltpu.VMEM((2, c, width), _BF16),
            pltpu.VMEM((2, 1, width), _F32),
            pltpu.VMEM((HGRN_HEADS, c, c), _BF16),
        ],
        compiler_params=_params("parallel", "arbitrary"),
        name="hgrn_mixer",
    )(h, norm_gain.reshape(1, d), w_in, lb_logits, out_gain.reshape(1, HEAD_DIM),
      jnp.asarray(sums, _BF16), jnp.asarray(masks, _F32))


def _tail_kernel(a_ref, wo_hbm, h_ref, gain_ref, wi_hbm, wd_hbm, fgain_ref, o_ref,
                 u_ref, wo_ref, wi_ref, wd_ref, sem_ref, *, layer, final_norm, ff_tile):
    n_slabs, _, kw = a_ref.shape
    d_ff = wd_ref.shape[0]
    n_tiles = d_ff // ff_tile

    def weight_copies():
        copies = [pltpu.make_async_copy(wo_hbm, wo_ref, sem_ref.at[0])]
        for j in range(n_tiles):
            for half in range(2):
                cols = pl.ds(half * d_ff + j * ff_tile, ff_tile)
                copies.append(pltpu.make_async_copy(wi_hbm.at[layer, :, cols], wi_ref.at[:, cols],
                                                    sem_ref.at[len(copies)]))
            rows = pl.ds(j * ff_tile, ff_tile)
            copies.append(pltpu.make_async_copy(wd_hbm.at[layer, rows, :], wd_ref.at[rows, :],
                                                sem_ref.at[len(copies)]))
        return copies

    def body(copies):
        if copies:
            for cp in copies:
                cp.start()
            copies[0].wait()
        mixed = h_ref[...]
        for s in range(n_slabs):
            mixed += _dot(a_ref[s], wo_ref[s * kw:(s + 1) * kw, :])
        o_ref[...] = mixed
        u_ref[...] = (mixed * _rms_scale(mixed) * gain_ref[...]).astype(_BF16)
        for j in range(n_tiles):
            if copies:
                for cp in copies[1 + 3 * j:4 + 3 * j]:
                    cp.wait()
            u = u_ref[...]
            gate = _dot(u, wi_ref[:, j * ff_tile:(j + 1) * ff_tile])
            up = _dot(u, wi_ref[:, d_ff + j * ff_tile:d_ff + (j + 1) * ff_tile])
            act = (gate * _sigmoid(gate) * up).astype(_BF16)
            o_ref[...] += _dot(act, wd_ref[j * ff_tile:(j + 1) * ff_tile, :])
        if final_norm:
            y = o_ref[...]
            o_ref[...] = y * _rms_scale(y) * fgain_ref[...]

    first_step = pl.program_id(0) == 0

    @pl.when(first_step)
    def _():
        body(weight_copies())

    @pl.when(jnp.logical_not(first_step))
    def _():
        body(None)


def _block_tail(a, w_out, h, gain, w_in, w_down, final_gain, *, layer, final_norm, ff_tile=256):
    n_slabs, t, kw = a.shape
    d = h.shape[1]
    d_ff = w_down.shape[1]
    tm = ROW_TILE
    n_copies = 1 + 3 * (d_ff // ff_tile)
    in_hbm = pl.BlockSpec(memory_space=pl.ANY)
    return pl.pallas_call(
        functools.partial(_tail_kernel, layer=layer, final_norm=final_norm, ff_tile=ff_tile),
        grid=(t // tm,),
        in_specs=[
            pl.BlockSpec((n_slabs, tm, kw), lambda i: (0, i, 0)),
            in_hbm,
            pl.BlockSpec((tm, d), lambda i: (i, 0)),
            _resident((1, d)),
            in_hbm,
            in_hbm,
            _resident((1, d)),
        ],
        out_specs=pl.BlockSpec((tm, d), lambda i: (i, 0)),
        out_shape=jax.ShapeDtypeStruct((t, d), _F32),
        scratch_shapes=[
            pltpu.VMEM((tm, d), _BF16),
            pltpu.VMEM(w_out.shape, w_out.dtype),
            pltpu.VMEM(w_in.shape[1:], w_in.dtype),
            pltpu.VMEM(w_down.shape[1:], w_down.dtype),
            pltpu.SemaphoreType.DMA((n_copies,)),
        ],
        compiler_params=_params("arbitrary"),
        name="block_tail",
    )(a, w_out, h, gain.reshape(1, d), w_in, w_down, final_gain.reshape(1, d))


def _attn_kernel(*refs):
    ins = refs[:15]
    out_ref, o_scr, l_scr = refs[15:]
    first_block = pl.program_id(1) == 0
    row = lax.broadcasted_iota(jnp.int32, (SPAN, SPAN), 0)
    col = lax.broadcasted_iota(jnp.int32, (SPAN, SPAN), 1)
    bias_cur = jnp.where(col <= row, 0.0, NEG_BIG).astype(_F32)
    bias_prev = jnp.where(col >= row, 0.0, NEG_BIG).astype(_F32)
    bias_halo = bias_prev + jnp.where(first_block, NEG_BIG, 0.0).astype(_F32)
    ones = jnp.ones((SPAN, HEAD_DIM), _BF16)

    for g, dil in enumerate(DILATIONS):
        q_ref, k_ref, v_ref, kh_ref, vh_ref = ins[5 * g:5 * g + 5]

        def with_ones(k, v):
            return k, jnp.concatenate([v, ones], axis=1)

        for r in range(dil):
            prev, prev_bias = with_ones(kh_ref[r], vh_ref[r]), bias_halo
            for blk in range(DILATIONS[-1] // dil):
                pos = slice(blk * SPAN, (blk + 1) * SPAN)
                kp, vp = prev
                kc, vc = cur = with_ones(k_ref[r, pos, :], v_ref[r, pos, :])
                s = _dot_nt(q_ref[r, pos, :], jnp.concatenate([kp, kc], axis=0))
                s = s + jnp.concatenate([prev_bias, bias_cur], axis=1)
                m = jnp.max(jnp.maximum(s[:, :SPAN], s[:, SPAN:]), axis=-1, keepdims=True)
                p = jnp.exp2(s - m).astype(_BF16)
                res = _dot(p, jnp.concatenate([vp, vc], axis=0))
                denom = res[:, HEAD_DIM:]
                start = r + blk * SPAN * dil
                idx = pl.ds(start, SPAN) if dil == 1 else pl.ds(start, SPAN, stride=dil)
                o_scr[g, idx, :] = res[:, :HEAD_DIM] / denom
                l_scr[g, idx, :] = m + jnp.log2(denom)
                prev, prev_bias = cur, bias_prev

    merge_rows = 256

    def merge(ci, carry):
        rr = pl.ds(pl.multiple_of(ci * merge_rows, merge_rows), merge_rows)
        l0, l1, l2 = l_scr[0, rr, :], l_scr[1, rr, :], l_scr[2, rr, :]
        m = jnp.maximum(jnp.maximum(l0, l1), l2)
        e0, e1, e2 = jnp.exp2(l0 - m), jnp.exp2(l1 - m), jnp.exp2(l2 - m)
        inv = 1.0 / (e0 + e1 + e2)
        out_ref[0, rr, :] = (o_scr[0, rr, :] * (e0 * inv)).astype(out_ref.dtype)
        out_ref[1, rr, :] = (o_scr[1, rr, :] * (e1 * inv)).astype(out_ref.dtype)
        out_ref[2, rr, :] = (o_scr[2, rr, :] * (e2 * inv)).astype(out_ref.dtype)
        return carry

    lax.fori_loop(0, ATTN_BLOCK // merge_rows, merge, 0)


def _attention(qkv_groups, *, batch):
    t = qkv_groups[0].shape[1]
    tb = ATTN_BLOCK
    steps = t // batch // tb
    in_specs, operands = [], []
    for g, dil in enumerate(DILATIONS):
        positions = tb // dil
        ratio = positions // SPAN

        def cur(which, dil=dil, positions=positions):
            return pl.BlockSpec(
                (dil, positions, HEAD_DIM),
                lambda b, i, j: (0, b * steps + i, which * HEADS_PER_GROUP + j))

        def prev(which, dil=dil, ratio=ratio):
            return pl.BlockSpec(
                (dil, SPAN, HEAD_DIM),
                lambda b, i, j: (0, jnp.maximum((b * steps + i) * ratio - 1, 0),
                                 which * HEADS_PER_GROUP + j))

        in_specs += [cur(0), cur(1), cur(2), prev(1), prev(2)]
        operands += [qkv_groups[g]] * 5
    n_groups = len(DILATIONS)
    return pl.pallas_call(
        _attn_kernel,
        grid=(batch, steps, HEADS_PER_GROUP),
        in_specs=in_specs,
        out_specs=pl.BlockSpec((n_groups, tb, HEAD_DIM), lambda b, i, j: (0, b * steps + i, j)),
        out_shape=jax.ShapeDtypeStruct((n_groups, t, HEADS_PER_GROUP * HEAD_DIM), _BF16),
        scratch_shapes=[pltpu.VMEM((3, tb, HEAD_DIM), _F32), pltpu.VMEM((3, tb, HEAD_DIM), _F32)],
        compiler_params=_params("parallel", "arbitrary", "arbitrary"),
        name="dilated_attention",
    )(*operands)


def _rope_tables(seq_len):
    inv_freq = 1.0 / (ROPE_THETA ** (np.arange(0, HEAD_DIM, 2, dtype=np.float64) / HEAD_DIM))
    ang = np.arange(seq_len, dtype=np.float64)[:, None] * inv_freq[None, :]
    cos, sin = np.cos(ang), np.sin(ang)
    return (jnp.asarray(np.concatenate([cos, cos], axis=-1), _F32),
            jnp.asarray(np.concatenate([-sin, sin], axis=-1), _F32))


def kernel(x, norm_mix, norm_ffn, hgrn_w_in, hgrn_lb_logits, hgrn_out_norm, hgrn_w_out,
           attn_w_qkv, attn_w_out, ffn_w_in, ffn_w_down, final_norm):
    batch, seq, d = x.shape
    cos2, sin2 = _rope_tables(seq)
    h = x.reshape(batch * seq, d)

    gated = _hgrn_mixer(h, norm_mix[0], hgrn_w_in[0], hgrn_lb_logits, hgrn_out_norm[0],
                        batch=batch, layer=0)
    ffn_in, ffn_down = ffn_w_in, ffn_w_down
    h = _block_tail(gated[None], hgrn_w_out[0], h, norm_ffn[0], ffn_in, ffn_down, final_norm,
                    layer=0, final_norm=False)

    qkv = _qkv_projection(h, norm_mix[1], attn_w_qkv[0], cos2, sin2,
                          scale=HEAD_DIM ** -0.5 * LOG2_E)
    attn = _attention(qkv, batch=batch)
    h = _block_tail(attn, attn_w_out[0], h, norm_ffn[1], ffn_in, ffn_down, final_norm,
                    layer=1, final_norm=True)
    return h.reshape(batch, seq, d)
```

```python
import functools

import numpy as np
import jax
import jax.numpy as jnp
from jax import lax
from jax.experimental import pallas as pl
from jax.experimental.pallas import tpu as pltpu

D_MODEL = 1024
HEAD_DIM = 128
HGRN_HEADS = 8
HGRN_CHUNK = 64
HGRN_LEVELS = 6
HGRN_STEP_TOKENS = 1024
ATTN_HEADS = 12
ATTN_WIDTH = ATTN_HEADS * HEAD_DIM
DILATIONS = (1, 4, 16)
SPAN = 128
HEADS_PER_GROUP = 4
ATTN_BLOCK = SPAN * DILATIONS[-1]
ROW_TILE = 512
COL_TILE = 512
ROPE_THETA = 10000.0
NORM_EPS = 1e-6
NEG_BIG = -1e30
LOG2_E = float(np.log2(np.e))
VMEM_LIMIT_BYTES = 56 * 1024 * 1024

_F32 = jnp.float32
_BF16 = jnp.bfloat16


def _dot(a, b):
    return lax.dot_general(a, b, (((1,), (0,)), ((), ())), preferred_element_type=_F32)


def _dot_nt(a, b):
    return lax.dot_general(a, b, (((1,), (1,)), ((), ())), preferred_element_type=_F32)


def _dot_tn(a, b):
    return lax.dot_general(a, b, (((0,), (0,)), ((), ())), preferred_element_type=_F32)


def _sigmoid(x):
    return 1.0 / (1.0 + jnp.exp2(x * (-LOG2_E)))


def _rms_scale(x):
    return lax.rsqrt(jnp.mean(x * x, axis=-1, keepdims=True) + NORM_EPS)


def _params(*sem):
    return pltpu.CompilerParams(dimension_semantics=sem, vmem_limit_bytes=VMEM_LIMIT_BYTES)


def _resident(shape, layer=None):
    if layer is None:
        index = (0,) * len(shape)
    else:
        index = (layer,) + (0,) * (len(shape) - 1)
        shape = (None,) + tuple(shape[1:])
    return pl.BlockSpec(shape, lambda *_: index, pipeline_mode=pl.Buffered(1))


def _qkv_kernel(h_ref, gain_ref, w_ref, cos_ref, sin_ref, o_ref, *, scale):
    x = h_ref[...]
    u = (x * _rms_scale(x) * gain_ref[...]).astype(_BF16)
    cos, sin = cos_ref[...], sin_ref[...]
    cos_q, sin_q = cos * scale, sin * scale
    for j in range(w_ref.shape[1] // COL_TILE):
        res = _dot(u, w_ref[:, j * COL_TILE:(j + 1) * COL_TILE])
        for hh in range(COL_TILE // HEAD_DIM):
            lo = j * COL_TILE + hh * HEAD_DIM
            xh = res[:, hh * HEAD_DIM:(hh + 1) * HEAD_DIM]
            if lo < ATTN_WIDTH:
                xh = xh * cos_q + pltpu.roll(xh, HEAD_DIM // 2, 1) * sin_q
            elif lo < 2 * ATTN_WIDTH:
                xh = xh * cos + pltpu.roll(xh, HEAD_DIM // 2, 1) * sin
            o_ref[:, lo:lo + HEAD_DIM] = xh


def _qkv_projection(h, gain, w, cos2, sin2, *, scale):
    t, d = h.shape
    n = w.shape[1]
    tm = ROW_TILE
    seq_tiles = cos2.shape[0] // tm
    return pl.pallas_call(
        functools.partial(_qkv_kernel, scale=scale),
        grid=(t // tm,),
        in_specs=[
            pl.BlockSpec((tm, d), lambda i: (i, 0)),
            _resident((1, d)),
            _resident((d, n)),
            pl.BlockSpec((tm, HEAD_DIM), lambda i: (i % seq_tiles, 0)),
            pl.BlockSpec((tm, HEAD_DIM), lambda i: (i % seq_tiles, 0)),
        ],
        out_specs=pl.BlockSpec((tm, n), lambda i: (i, 0)),
        out_shape=jax.ShapeDtypeStruct((t, n), _F32),
        compiler_params=_params("parallel"),
        name="qkv_projection",
    )(h, gain.reshape(1, d), w, cos2, sin2)


def _hgrn_tables():
    c = HGRN_CHUNK
    assert 1 << HGRN_LEVELS == c
    t = np.arange(c)
    col = t[None, :]
    row = t[:, None]
    sums = np.zeros((HGRN_LEVELS + 1, c, c), np.float32)
    sums[0] = col <= row
    sums[1] = col > row
    masks = np.zeros((HGRN_LEVELS + 1, c, c), np.float32)
    for level in range(HGRN_LEVELS):
        half = c >> (level + 1)
        block = t // (2 * half)
        mid = block * 2 * half + half
        is_query = t >= mid
        if level < HGRN_LEVELS - 1:
            q_rows = (col >= mid[:, None]) & (col <= row) & is_query[:, None]
            k_rows = (col > row) & (col < mid[:, None]) & (~is_query)[:, None]
            sums[2 + level] = q_rows | k_rows
        masks[level] = ((block[:, None] == block[None, :]) & is_query[:, None]
                        & (~is_query)[None, :])
    masks[HGRN_LEVELS] = np.eye(c)
    assert np.array_equal(masks.sum(0), np.tril(np.ones((c, c))))
    sums = sums.reshape((HGRN_LEVELS + 1) * c, c)
    return np.concatenate([sums, sums], axis=1), masks


def _hgrn_kernel(h_ref, ngain_ref, w_ref, lbl_ref, gain_ref, sums_ref, masks_ref, o_ref,
                 state_ref, proj_ref, a2_ref, b2_ref, v2_ref, decay_ref, s_ref, *, layer):
    c = HGRN_CHUNK
    width = HGRN_HEADS * HEAD_DIM

    @pl.when(pl.program_id(1) == 0)
    def _():
        state_ref[...] = jnp.zeros_like(state_ref)

    x = h_ref[...]
    u = (x * _rms_scale(x) * ngain_ref[...]).astype(_BF16)
    for j in range(w_ref.shape[1] // COL_TILE):
        cols = slice(j * COL_TILE, (j + 1) * COL_TILE)
        proj_ref[:, cols] = _dot(u, w_ref[:, cols])

    logits = lbl_ref[...]
    e = jnp.exp(logits - jnp.max(logits, axis=0, keepdims=True))
    lb = jnp.sum(e[:layer + 1], axis=0, keepdims=True) / jnp.sum(e, axis=0, keepdims=True)
    out_gain = gain_ref[...]

    n_chunks = h_ref.shape[0] // c

    def chunk_rows(ci):
        return pl.ds(pl.multiple_of(ci * c, c), c)

    def prepare(ci, slot):
        a_ref, b_ref, v_ref = a2_ref.at[slot], b2_ref.at[slot], v2_ref.at[slot]
        rows = chunk_rows(ci)
        q = proj_ref[rows, 0:width]
        forget = lb + (1.0 - lb) * _sigmoid(proj_ref[rows, width:2 * width])
        glog = jnp.log(forget) * LOG2_E
        kk = 1.0 - forget
        qq = q * _sigmoid(q)
        g_hi = glog.astype(_BF16)
        g_lo = (glog - g_hi.astype(_F32)).astype(_BF16)
        factors = jnp.exp2(_dot(sums_ref[...], jnp.concatenate([g_hi, g_lo], axis=0)))

        qq = qq.astype(_BF16)
        kk = kk.astype(_BF16)
        from_start = factors[0:c]
        a_ref[0] = qq * from_start.astype(_BF16)
        decay_ref[slot] = from_start[c - 1:c, :]
        b_ref[0] = kk * factors[c:2 * c].astype(_BF16)
        for level in range(HGRN_LEVELS - 1):
            fac = factors[(2 + level) * c:(3 + level) * c].astype(_BF16)
            a_ref[1 + level] = qq * fac
            b_ref[1 + level] = kk * fac
        a_ref[HGRN_LEVELS] = qq * forget.astype(_BF16)
        a_ref[HGRN_LEVELS + 1] = qq
        b_ref[HGRN_LEVELS] = kk
        v_ref[...] = proj_ref[rows, 2 * width:3 * width].astype(_BF16)

    def lanes(h):
        return slice(h * HEAD_DIM, (h + 1) * HEAD_DIM)

    def consume(ci, slot):
        a_ref, b_ref, v_ref = a2_ref.at[slot], b2_ref.at[slot], v2_ref.at[slot]
        rows = chunk_rows(ci)
        chunk_decay = decay_ref[slot]
        pairs = [(h, h + 1) for h in range(0, HGRN_HEADS, 2)]
        for pair in pairs:
            scores = [jnp.zeros((c, c), _F32) for _ in pair]
            for level in range(HGRN_LEVELS - 1):
                for i, h in enumerate(pair):
                    scores[i] += masks_ref[level] * _dot_nt(a_ref[1 + level, :, lanes(h)],
                                                            b_ref[1 + level, :, lanes(h)])
            for i, h in enumerate(pair):
                both = a_ref[HGRN_LEVELS:HGRN_LEVELS + 2, :, lanes(h)].reshape(2 * c, HEAD_DIM)
                prod = _dot_nt(both, b_ref[HGRN_LEVELS, :, lanes(h)])
                scores[i] += (masks_ref[HGRN_LEVELS - 1] * prod[:c]
                              + masks_ref[HGRN_LEVELS] * prod[c:])
                s_ref[h] = scores[i].astype(_BF16)
        for pair in pairs:
            states = [state_ref[h] for h in pair]
            intra = [_dot(s_ref[h], v_ref[:, lanes(h)]) for h in pair]
            inter = [_dot(a_ref[0, :, lanes(h)], states[i].astype(_BF16))
                     for i, h in enumerate(pair)]
            update = [_dot_tn(b_ref[0, :, lanes(h)], v_ref[:, lanes(h)]) for h in pair]
            for i, h in enumerate(pair):
                decay_col = jnp.transpose(
                    jnp.broadcast_to(chunk_decay[:, lanes(h)], (8, HEAD_DIM)))[:, 0:1]
                state_ref[h] = states[i] * decay_col + update[i]
                o = intra[i] + inter[i]
                o = o * _rms_scale(o) * out_gain
                gate = proj_ref[rows, 3 * width + h * HEAD_DIM:3 * width + (h + 1) * HEAD_DIM]
                o_ref[rows, lanes(h)] = (o * (gate * _sigmoid(gate))).astype(o_ref.dtype)

    def two_chunks(i, carry):
        first = 2 * i
        prepare(first + 1, 1)
        consume(first, 0)
        prepare(first + 2, 0)
        consume(first + 1, 1)
        return carry

    prepare(0, 0)
    lax.fori_loop(0, n_chunks // 2 - 1, two_chunks, 0)
    prepare(n_chunks - 1, 1)
    consume(n_chunks - 2, 0)
    consume(n_chunks - 1, 1)


def _hgrn_mixer(h, norm_gain, w_in, lb_logits, out_gain, *, batch, layer):
    t, d = h.shape
    width = HGRN_HEADS * HEAD_DIM
    tc = HGRN_STEP_TOKENS
    steps = t // batch // tc
    sums, masks = _hgrn_tables()
    c = HGRN_CHUNK
    return pl.pallas_call(
        functools.partial(_hgrn_kernel, layer=layer),
        grid=(batch, steps),
        in_specs=[
            pl.BlockSpec((tc, d), lambda b, s: (b * steps + s, 0)),
            _resident((1, d)),
            _resident(w_in.shape),
            _resident(lb_logits.shape),
            _resident((1, HEAD_DIM)),
            _resident(sums.shape),
            _resident(masks.shape),
        ],
        out_specs=pl.BlockSpec((tc, width), lambda b, s: (b * steps + s, 0)),
        out_shape=jax.ShapeDtypeStruct((t, width), _BF16),
        scratch_shapes=[
            pltpu.VMEM((HGRN_HEADS, HEAD_DIM, HEAD_DIM), _F32),
            pltpu.VMEM((tc, 4 * width), _F32),
            pltpu.VMEM((2, HGRN_LEVELS + 2, c, width), _BF16),
            pltpu.VMEM((2, HGRN_LEVELS + 1, c, width), _BF16),
            pltpu.VMEM((2, c, width), _BF16),
            pltpu.VMEM((2, 1, width), _F32),
            pltpu.VMEM((HGRN_HEADS, c, c), _BF16),
        ],
        compiler_params=_params("parallel", "arbitrary"),
        name="hgrn_mixer",
    )(h, norm_gain.reshape(1, d), w_in, lb_logits, out_gain.reshape(1, HEAD_DIM),
      jnp.asarray(sums, _BF16), jnp.asarray(masks, _F32))


def _tail_kernel(a_ref, wo_hbm, h_ref, gain_ref, wi_hbm, wd_hbm, fgain_ref, o_ref,
                 u_ref, wo_ref, wi_ref, wd_ref, sem_ref, *, layer, final_norm, ff_tile):
    n_slabs, _, kw = a_ref.shape
    d_ff = wd_ref.shape[0]
    n_tiles = d_ff // ff_tile

    def weight_copies():
        copies = [pltpu.make_async_copy(wo_hbm, wo_ref, sem_ref.at[0])]
        for j in range(n_tiles):
            for half in range(2):
                cols = pl.ds(half * d_ff + j * ff_tile, ff_tile)
                copies.append(pltpu.make_async_copy(wi_hbm.at[layer, :, cols], wi_ref.at[:, cols],
                                                    sem_ref.at[len(copies)]))
            rows = pl.ds(j * ff_tile, ff_tile)
            copies.append(pltpu.make_async_copy(wd_hbm.at[layer, rows, :], wd_ref.at[rows, :],
                                                sem_ref.at[len(copies)]))
        return copies

    def body(copies):
        if copies:
            for cp in copies:
                cp.start()
            copies[0].wait()
        mixed = h_ref[...]
        for s in range(n_slabs):
            mixed += _dot(a_ref[s], wo_ref[s * kw:(s + 1) * kw, :])
        o_ref[...] = mixed
        u_ref[...] = (mixed * _rms_scale(mixed) * gain_ref[...]).astype(_BF16)
        for j in range(n_tiles):
            if copies:
                for cp in copies[1 + 3 * j:4 + 3 * j]:
                    cp.wait()
            u = u_ref[...]
            gate = _dot(u, wi_ref[:, j * ff_tile:(j + 1) * ff_tile])
            up = _dot(u, wi_ref[:, d_ff + j * ff_tile:d_ff + (j + 1) * ff_tile])
            act = (gate * _sigmoid(gate) * up).astype(_BF16)
            o_ref[...] += _dot(act, wd_ref[j * ff_tile:(j + 1) * ff_tile, :])
        if final_norm:
            y = o_ref[...]
            o_ref[...] = y * _rms_scale(y) * fgain_ref[...]

    first_step = pl.program_id(0) == 0

    @pl.when(first_step)
    def _():
        body(weight_copies())

    @pl.when(jnp.logical_not(first_step))
    def _():
        body(None)


def _block_tail(a, w_out, h, gain, w_in, w_down, final_gain, *, layer, final_norm, ff_tile=256):
    n_slabs, t, kw = a.shape
    d = h.shape[1]
    d_ff = w_down.shape[1]
    tm = ROW_TILE
    n_copies = 1 + 3 * (d_ff // ff_tile)
    in_hbm = pl.BlockSpec(memory_space=pl.ANY)
    return pl.pallas_call(
        functools.partial(_tail_kernel, layer=layer, final_norm=final_norm, ff_tile=ff_tile),
        grid=(t // tm,),
        in_specs=[
            pl.BlockSpec((n_slabs, tm, kw), lambda i: (0, i, 0)),
            in_hbm,
            pl.BlockSpec((tm, d), lambda i: (i, 0)),
            _resident((1, d)),
            in_hbm,
            in_hbm,
            _resident((1, d)),
        ],
        out_specs=pl.BlockSpec((tm, d), lambda i: (i, 0)),
        out_shape=jax.ShapeDtypeStruct((t, d), _F32),
        scratch_shapes=[
            pltpu.VMEM((tm, d), _BF16),
            pltpu.VMEM(w_out.shape, w_out.dtype),
            pltpu.VMEM(w_in.shape[1:], w_in.dtype),
            pltpu.VMEM(w_down.shape[1:], w_down.dtype),
            pltpu.SemaphoreType.DMA((n_copies,)),
        ],
        compiler_params=_params("arbitrary"),
        name="block_tail",
    )(a, w_out, h, gain.reshape(1, d), w_in, w_down, final_gain.reshape(1, d))


def _attn_kernel(*refs):
    ins = refs[:15]
    out_ref, o_scr, l_scr = refs[15:]
    first_block = pl.program_id(1) == 0
    row = lax.broadcasted_iota(jnp.int32, (SPAN, SPAN), 0)
    col = lax.broadcasted_iota(jnp.int32, (SPAN, SPAN), 1)
    bias_cur = jnp.where(col <= row, 0.0, NEG_BIG).astype(_F32)
    bias_prev = jnp.where(col >= row, 0.0, NEG_BIG).astype(_F32)
    bias_halo = bias_prev + jnp.where(first_block, NEG_BIG, 0.0).astype(_F32)
    ones = jnp.ones((SPAN, HEAD_DIM), _BF16)
    chain_len = 4
    body_units = 16

    merge = functools.partial(_attn_merge, o_scr, l_scr, out_ref)

    assert DILATIONS[0] == 1
    for g, dil in reversed(list(enumerate(DILATIONS))):
        q_ref, k_ref, v_ref, kh_ref, vh_ref = ins[5 * g:5 * g + 5]

        def load(ref, start, dil=dil):
            idx = pl.ds(start, SPAN) if dil == 1 else pl.ds(start, SPAN, stride=dil)
            return ref[idx, :].astype(_BF16)

        def load_kv(kref, vref, start, load=load):
            return load(kref, start), jnp.concatenate([load(vref, start), ones], axis=1)

        def chain(starts, prev, prev_bias, g=g, dil=dil, q_ref=q_ref, k_ref=k_ref, v_ref=v_ref,
                  load=load, load_kv=load_kv):
            for start in starts:
                kp, vp = prev
                kc, vc = cur = load_kv(k_ref, v_ref, start)
                s = _dot_nt(load(q_ref, start), jnp.concatenate([kp, kc], axis=0))
                s = s + jnp.concatenate([prev_bias, bias_cur], axis=1)
                m = jnp.max(jnp.maximum(s[:, :SPAN], s[:, SPAN:]), axis=-1, keepdims=True)
                p = jnp.exp2(s - m).astype(_BF16)
                r = _dot(p, jnp.concatenate([vp, vc], axis=0))
                denom = r[:, HEAD_DIM:]
                idx = pl.ds(start, SPAN) if dil == 1 else pl.ds(start, SPAN, stride=dil)
                o_scr[g, idx, :] = r[:, :HEAD_DIM] / denom
                l_scr[g, idx, :] = m + jnp.log2(denom)
                prev, prev_bias = cur, bias_prev

        block_rows = SPAN * dil
        if dil == DILATIONS[-1]:
            def body(i, carry, chain=chain, load_kv=load_kv, kh_ref=kh_ref, vh_ref=vh_ref):
                for u in range(body_units):
                    r = i * body_units + u
                    chain([r], load_kv(kh_ref, vh_ref, r), bias_halo)
                return carry
            lax.fori_loop(0, dil // body_units, body, 0)
        elif dil > 1:
            assert DILATIONS[-1] // dil == chain_len

            def body(i, carry, chain=chain, load_kv=load_kv, kh_ref=kh_ref, vh_ref=vh_ref,
                     block_rows=block_rows):
                for u in range(body_units // chain_len):
                    r = i * (body_units // chain_len) + u
                    chain([r + b * block_rows for b in range(chain_len)],
                          load_kv(kh_ref, vh_ref, r), bias_halo)
                return carry
            lax.fori_loop(0, dil * chain_len // body_units, body, 0)
        else:
            for first in range(0, DILATIONS[-1], chain_len):
                prev = (load_kv(kh_ref, vh_ref, 0) if first == 0
                        else load_kv(k_ref, v_ref, (first - 1) * block_rows))
                chain([(first + b) * block_rows for b in range(chain_len)], prev,
                      bias_halo if first == 0 else bias_prev)
                merge(first * block_rows, chain_len * block_rows)


def _attn_merge(o_scr, l_scr, out_ref, start, rows):
    chunk = 256
    for lo in range(start, start + rows, chunk):
        rr = slice(lo, lo + chunk)
        l0, l1, l2 = l_scr[0, rr, :], l_scr[1, rr, :], l_scr[2, rr, :]
        m = jnp.maximum(jnp.maximum(l0, l1), l2)
        e0, e1, e2 = jnp.exp2(l0 - m), jnp.exp2(l1 - m), jnp.exp2(l2 - m)
        inv = 1.0 / (e0 + e1 + e2)
        out_ref[0, rr, :] = (o_scr[0, rr, :] * (e0 * inv)).astype(out_ref.dtype)
        out_ref[1, rr, :] = (o_scr[1, rr, :] * (e1 * inv)).astype(out_ref.dtype)
        out_ref[2, rr, :] = (o_scr[2, rr, :] * (e2 * inv)).astype(out_ref.dtype)


def _attention(qkv, *, batch):
    t = qkv.shape[0]
    tb = ATTN_BLOCK
    steps = t // batch // tb
    in_specs, operands = [], []
    for g, dil in enumerate(DILATIONS):
        halo = SPAN * dil
        ratio = tb // halo

        def cur(which, g=g):
            return pl.BlockSpec(
                (tb, HEAD_DIM),
                lambda b, i, j: (b * steps + i, which * ATTN_HEADS + g * HEADS_PER_GROUP + j))

        def prev(which, g=g, ratio=ratio, halo=halo):
            return pl.BlockSpec(
                (halo, HEAD_DIM),
                lambda b, i, j: (jnp.maximum((b * steps + i) * ratio - 1, 0),
                                 which * ATTN_HEADS + g * HEADS_PER_GROUP + j))

        in_specs += [cur(0), cur(1), cur(2), prev(1), prev(2)]
        operands += [qkv] * 5
    n_groups = len(DILATIONS)
    return pl.pallas_call(
        _attn_kernel,
        grid=(batch, steps, HEADS_PER_GROUP),
        in_specs=in_specs,
        out_specs=pl.BlockSpec((n_groups, tb, HEAD_DIM), lambda b, i, j: (0, b * steps + i, j)),
        out_shape=jax.ShapeDtypeStruct((n_groups, t, HEADS_PER_GROUP * HEAD_DIM), _BF16),
        scratch_shapes=[pltpu.VMEM((3, tb, HEAD_DIM), _F32), pltpu.VMEM((3, tb, HEAD_DIM), _F32)],
        compiler_params=_params("parallel", "arbitrary", "arbitrary"),
        name="dilated_attention",
    )(*operands)


def _rope_tables(seq_len):
    inv_freq = 1.0 / (ROPE_THETA ** (np.arange(0, HEAD_DIM, 2, dtype=np.float64) / HEAD_DIM))
    ang = np.arange(seq_len, dtype=np.float64)[:, None] * inv_freq[None, :]
    cos, sin = np.cos(ang), np.sin(ang)
    return (jnp.asarray(np.concatenate([cos, cos], axis=-1), _F32),
            jnp.asarray(np.concatenate([-sin, sin], axis=-1), _F32))


def kernel(x, norm_mix, norm_ffn, hgrn_w_in, hgrn_lb_logits, hgrn_out_norm, hgrn_w_out,
           attn_w_qkv, attn_w_out, ffn_w_in, ffn_w_down, final_norm):
    batch, seq, d = x.shape
    cos2, sin2 = _rope_tables(seq)
    h = x.reshape(batch * seq, d)

    gated = _hgrn_mixer(h, norm_mix[0], hgrn_w_in[0], hgrn_lb_logits, hgrn_out_norm[0],
                        batch=batch, layer=0)
    ffn_in, ffn_down = ffn_w_in, ffn_w_down
    h = _block_tail(gated[None], hgrn_w_out[0], h, norm_ffn[0], ffn_in, ffn_down, final_norm,
                    layer=0, final_norm=False)

    qkv = _qkv_projection(h, norm_mix[1], attn_w_qkv[0], cos2, sin2,
                          scale=HEAD_DIM ** -0.5 * LOG2_E)
    attn = _attention(qkv, batch=batch)
    h = _block_tail(attn, attn_w_out[0], h, norm_ffn[1], ffn_in, ffn_down, final_norm,
                    layer=1, final_norm=True)
    return h.reshape(batch, seq, d)
```

```python
import functools

import numpy as np
import jax
import jax.numpy as jnp
from jax import lax
from jax.experimental import pallas as pl
from jax.experimental.pallas import tpu as pltpu

D_MODEL = 1024
HEAD_DIM = 128
HGRN_HEADS = 8
HGRN_CHUNK = 64
HGRN_LEVELS = 6
HGRN_STEP_TOKENS = 1024
ATTN_HEADS = 12
ATTN_WIDTH = ATTN_HEADS * HEAD_DIM
DILATIONS = (1, 4, 16)
SPAN = 128
HEADS_PER_GROUP = 4
ATTN_BLOCK = SPAN * DILATIONS[-1]
ROW_TILE = 512
COL_TILE = 512
ROPE_THETA = 10000.0
NORM_EPS = 1e-6
NEG_BIG = -1e30
LOG2_E = float(np.log2(np.e))
VMEM_LIMIT_BYTES = 56 * 1024 * 1024

_F32 = jnp.float32
_BF16 = jnp.bfloat16


def _dot(a, b):
    return lax.dot_general(a, b, (((1,), (0,)), ((), ())), preferred_element_type=_F32)


def _dot_nt(a, b):
    return lax.dot_general(a, b, (((1,), (1,)), ((), ())), preferred_element_type=_F32)


def _dot_tn(a, b):
    return lax.dot_general(a, b, (((0,), (0,)), ((), ())), preferred_element_type=_F32)


def _sigmoid(x):
    return 1.0 / (1.0 + jnp.exp2(x * (-LOG2_E)))


def _rms_scale(x):
    return lax.rsqrt(jnp.mean(x * x, axis=-1, keepdims=True) + NORM_EPS)


def _params(*sem):
    return pltpu.CompilerParams(dimension_semantics=sem, vmem_limit_bytes=VMEM_LIMIT_BYTES)


def _resident(shape, layer=None):
    if layer is None:
        index = (0,) * len(shape)
    else:
        index = (layer,) + (0,) * (len(shape) - 1)
        shape = (None,) + tuple(shape[1:])
    return pl.BlockSpec(shape, lambda *_: index, pipeline_mode=pl.Buffered(1))


def _qkv_kernel(h_ref, gain_ref, w_ref, cos_ref, sin_ref, o_ref, *, scale):
    x = h_ref[...]
    u = (x * _rms_scale(x) * gain_ref[...]).astype(_BF16)
    cos, sin = cos_ref[...], sin_ref[...]
    cos_q, sin_q = cos * scale, sin * scale
    for j in range(w_ref.shape[1] // COL_TILE):
        res = _dot(u, w_ref[:, j * COL_TILE:(j + 1) * COL_TILE])
        for hh in range(COL_TILE // HEAD_DIM):
            lo = j * COL_TILE + hh * HEAD_DIM
            xh = res[:, hh * HEAD_DIM:(hh + 1) * HEAD_DIM]
            if lo < ATTN_WIDTH:
                xh = xh * cos_q + pltpu.roll(xh, HEAD_DIM // 2, 1) * sin_q
            elif lo < 2 * ATTN_WIDTH:
                xh = xh * cos + pltpu.roll(xh, HEAD_DIM // 2, 1) * sin
            o_ref[:, lo:lo + HEAD_DIM] = xh


def _qkv_projection(h, gain, w, cos2, sin2, *, scale):
    t, d = h.shape
    n = w.shape[1]
    tm = ROW_TILE
    seq_tiles = cos2.shape[0] // tm
    return pl.pallas_call(
        functools.partial(_qkv_kernel, scale=scale),
        grid=(t // tm,),
        in_specs=[
            pl.BlockSpec((tm, d), lambda i: (i, 0)),
            _resident((1, d)),
            _resident((d, n)),
            pl.BlockSpec((tm, HEAD_DIM), lambda i: (i % seq_tiles, 0)),
            pl.BlockSpec((tm, HEAD_DIM), lambda i: (i % seq_tiles, 0)),
        ],
        out_specs=pl.BlockSpec((tm, n), lambda i: (i, 0)),
        out_shape=jax.ShapeDtypeStruct((t, n), _F32),
        compiler_params=_params("parallel"),
        name="qkv_projection",
    )(h, gain.reshape(1, d), w, cos2, sin2)


def _hgrn_tables():
    c = HGRN_CHUNK
    assert 1 << HGRN_LEVELS == c
    t = np.arange(c)
    col = t[None, :]
    row = t[:, None]
    sums = np.zeros((HGRN_LEVELS + 1, c, c), np.float32)
    sums[0] = col <= row
    sums[1] = col > row
    masks = np.zeros((HGRN_LEVELS + 1, c, c), np.float32)
    for level in range(HGRN_LEVELS):
        half = c >> (level + 1)
        block = t // (2 * half)
        mid = block * 2 * half + half
        is_query = t >= mid
        if level < HGRN_LEVELS - 1:
            q_rows = (col >= mid[:, None]) & (col <= row) & is_query[:, None]
            k_rows = (col > row) & (col < mid[:, None]) & (~is_query)[:, None]
            sums[2 + level] = q_rows | k_rows
        masks[level] = ((block[:, None] == block[None, :]) & is_query[:, None]
                        & (~is_query)[None, :])
    masks[HGRN_LEVELS] = np.eye(c)
    assert np.array_equal(masks.sum(0), np.tril(np.ones((c, c))))
    sums = sums.reshape((HGRN_LEVELS + 1) * c, c)
    return np.concatenate([sums, sums], axis=1), masks


def _hgrn_kernel(h_ref, ngain_ref, w_ref, lbl_ref, gain_ref, sums_ref, masks_ref, o_ref,
                 state_ref, proj_ref, a2_ref, b2_ref, v2_ref, decay_ref, s_ref, *, layer):
    c = HGRN_CHUNK
    width = HGRN_HEADS * HEAD_DIM

    @pl.when(pl.program_id(1) == 0)
    def _():
        state_ref[...] = jnp.zeros_like(state_ref)

    x = h_ref[...]
    u = (x * _rms_scale(x) * ngain_ref[...]).astype(_BF16)
    for j in range(w_ref.shape[1] // COL_TILE):
        cols = slice(j * COL_TILE, (j + 1) * COL_TILE)
        proj_ref[:, cols] = _dot(u, w_ref[:, cols])

    logits = lbl_ref[...]
    e = jnp.exp(logits - jnp.max(logits, axis=0, keepdims=True))
    lb = jnp.sum(e[:layer + 1], axis=0, keepdims=True) / jnp.sum(e, axis=0, keepdims=True)
    out_gain = gain_ref[...]

    n_chunks = h_ref.shape[0] // c

    def chunk_rows(ci):
        return pl.ds(pl.multiple_of(ci * c, c), c)

    def prepare(ci, slot):
        a_ref, b_ref, v_ref = a2_ref.at[slot], b2_ref.at[slot], v2_ref.at[slot]
        rows = chunk_rows(ci)
        q = proj_ref[rows, 0:width]
        forget = lb + (1.0 - lb) * _sigmoid(proj_ref[rows, width:2 * width])
        glog = jnp.log(forget) * LOG2_E
        kk = 1.0 - forget
        qq = q * _sigmoid(q)
        g_hi = glog.astype(_BF16)
        g_lo = (glog - g_hi.astype(_F32)).astype(_BF16)
        factors = jnp.exp2(_dot(sums_ref[...], jnp.concatenate([g_hi, g_lo], axis=0)))

        qq = qq.astype(_BF16)
        kk = kk.astype(_BF16)
        from_start = factors[0:c]
        a_ref[0] = qq * from_start.astype(_BF16)
        decay_ref[slot] = from_start[c - 1:c, :]
        b_ref[0] = kk * factors[c:2 * c].astype(_BF16)
        for level in range(HGRN_LEVELS - 1):
            fac = factors[(2 + level) * c:(3 + level) * c].astype(_BF16)
            a_ref[1 + level] = qq * fac
            b_ref[1 + level] = kk * fac
        a_ref[HGRN_LEVELS] = qq * forget.astype(_BF16)
        a_ref[HGRN_LEVELS + 1] = qq
        b_ref[HGRN_LEVELS] = kk
        v_ref[...] = proj_ref[rows, 2 * width:3 * width].astype(_BF16)

    def lanes(h):
        return slice(h * HEAD_DIM, (h + 1) * HEAD_DIM)

    def consume(ci, slot):
        a_ref, b_ref, v_ref = a2_ref.at[slot], b2_ref.at[slot], v2_ref.at[slot]
        rows = chunk_rows(ci)
        chunk_decay = decay_ref[slot]
        pairs = [(h, h + 1) for h in range(0, HGRN_HEADS, 2)]
        for pair in pairs:
            scores = [jnp.zeros((c, c), _F32) for _ in pair]
            for level in range(HGRN_LEVELS - 1):
                for i, h in enumerate(pair):
                    scores[i] += masks_ref[level] * _dot_nt(a_ref[1 + level, :, lanes(h)],
                                                            b_ref[1 + level, :, lanes(h)])
            for i, h in enumerate(pair):
                both = a_ref[HGRN_LEVELS:HGRN_LEVELS + 2, :, lanes(h)].reshape(2 * c, HEAD_DIM)
                prod = _dot_nt(both, b_ref[HGRN_LEVELS, :, lanes(h)])
                scores[i] += (masks_ref[HGRN_LEVELS - 1] * prod[:c]
                              + masks_ref[HGRN_LEVELS] * prod[c:])
                s_ref[h] = scores[i].astype(_BF16)
        for pair in pairs:
            states = [state_ref[h] for h in pair]
            intra = [_dot(s_ref[h], v_ref[:, lanes(h)]) for h in pair]
            inter = [_dot(a_ref[0, :, lanes(h)], states[i].astype(_BF16))
                     for i, h in enumerate(pair)]
            update = [_dot_tn(b_ref[0, :, lanes(h)], v_ref[:, lanes(h)]) for h in pair]
            for i, h in enumerate(pair):
                decay_col = jnp.transpose(
                    jnp.broadcast_to(chunk_decay[:, lanes(h)], (8, HEAD_DIM)))[:, 0:1]
                state_ref[h] = states[i] * decay_col + update[i]
                o = intra[i] + inter[i]
                o = o * _rms_scale(o) * out_gain
                gate = proj_ref[rows, 3 * width + h * HEAD_DIM:3 * width + (h + 1) * HEAD_DIM]
                o_ref[rows, lanes(h)] = (o * (gate * _sigmoid(gate))).astype(o_ref.dtype)

    unroll = 4

    def some_chunks(i, carry):
        first = unroll * i
        for k in range(unroll):
            prepare(first + k + 1, (k + 1) % 2)
            consume(first + k, k % 2)
        return carry

    prepare(0, 0)
    lax.fori_loop(0, n_chunks // unroll - 1, some_chunks, 0)
    for ci in range(n_chunks - unroll, n_chunks):
        if ci + 1 < n_chunks:
            prepare(ci + 1, (ci + 1) % 2)
        consume(ci, ci % 2)


def _hgrn_mixer(h, norm_gain, w_in, lb_logits, out_gain, *, batch, layer):
    t, d = h.shape
    width = HGRN_HEADS * HEAD_DIM
    tc = HGRN_STEP_TOKENS
    steps = t // batch // tc
    sums, masks = _hgrn_tables()
    c = HGRN_CHUNK
    return pl.pallas_call(
        functools.partial(_hgrn_kernel, layer=layer),
        grid=(batch, steps),
        in_specs=[
            pl.BlockSpec((tc, d), lambda b, s: (b * steps + s, 0)),
            _resident((1, d)),
            _resident(w_in.shape),
            _resident(lb_logits.shape),
            _resident((1, HEAD_DIM)),
            _resident(sums.shape),
            _resident(masks.shape),
        ],
        out_specs=pl.BlockSpec((tc, width), lambda b, s: (b * steps + s, 0)),
        out_shape=jax.ShapeDtypeStruct((t, width), _BF16),
        scratch_shapes=[
            pltpu.VMEM((HGRN_HEADS, HEAD_DIM, HEAD_DIM), _F32),
            pltpu.VMEM((tc, 4 * width), _F32),
            pltpu.VMEM((2, HGRN_LEVELS + 2, c, width), _BF16),
            pltpu.VMEM((2, HGRN_LEVELS + 1, c, width), _BF16),
            pltpu.VMEM((2, c, width), _BF16),
            pltpu.VMEM((2, 1, width), _F32),
            pltpu.VMEM((HGRN_HEADS, c, c), _BF16),
        ],
        compiler_params=_params("parallel", "arbitrary"),
        name="hgrn_mixer",
    )(h, norm_gain.reshape(1, d), w_in, lb_logits, out_gain.reshape(1, HEAD_DIM),
      jnp.asarray(sums, _BF16), jnp.asarray(masks, _F32))


def _tail_kernel(a_ref, wo_hbm, h_ref, gain_ref, wi_hbm, wd_hbm, fgain_ref, o_ref,
                 u_ref, wo_ref, wi_ref, wd_ref, sem_ref, *, layer, final_norm, ff_tile):
    n_slabs, _, kw = a_ref.shape
    d_ff = wd_ref.shape[0]
    n_tiles = d_ff // ff_tile

    def weight_copies():
        copies = [pltpu.make_async_copy(wo_hbm, wo_ref, sem_ref.at[0])]
        for j in range(n_tiles):
            for half in range(2):
                cols = pl.ds(half * d_ff + j * ff_tile, ff_tile)
                copies.append(pltpu.make_async_copy(wi_hbm.at[layer, :, cols], wi_ref.at[:, cols],
                                                    sem_ref.at[len(copies)]))
            rows = pl.ds(j * ff_tile, ff_tile)
            copies.append(pltpu.make_async_copy(wd_hbm.at[layer, rows, :], wd_ref.at[rows, :],
                                                sem_ref.at[len(copies)]))
        return copies

    def body(copies):
        if copies:
            for cp in copies:
                cp.start()
            copies[0].wait()
        mixed = h_ref[...]
        for s in range(n_slabs):
            mixed += _dot(a_ref[s], wo_ref[s * kw:(s + 1) * kw, :])
        o_ref[...] = mixed
        u_ref[...] = (mixed * _rms_scale(mixed) * gain_ref[...]).astype(_BF16)
        for j in range(n_tiles):
            if copies:
                for cp in copies[1 + 3 * j:4 + 3 * j]:
                    cp.wait()
            u = u_ref[...]
            gate = _dot(u, wi_ref[:, j * ff_tile:(j + 1) * ff_tile])
            up = _dot(u, wi_ref[:, d_ff + j * ff_tile:d_ff + (j + 1) * ff_tile])
            act = (gate * _sigmoid(gate) * up).astype(_BF16)
            o_ref[...] += _dot(act, wd_ref[j * ff_tile:(j + 1) * ff_tile, :])
        if final_norm:
            y = o_ref[...]
            o_ref[...] = y * _rms_scale(y) * fgain_ref[...]

    first_step = pl.program_id(0) == 0

    @pl.when(first_step)
    def _():
        body(weight_copies())

    @pl.when(jnp.logical_not(first_step))
    def _():
        body(None)


def _block_tail(a, w_out, h, gain, w_in, w_down, final_gain, *, layer, final_norm, ff_tile=256):
    n_slabs, t, kw = a.shape
    d = h.shape[1]
    d_ff = w_down.shape[1]
    tm = ROW_TILE
    n_copies = 1 + 3 * (d_ff // ff_tile)
    in_hbm = pl.BlockSpec(memory_space=pl.ANY)
    return pl.pallas_call(
        functools.partial(_tail_kernel, layer=layer, final_norm=final_norm, ff_tile=ff_tile),
        grid=(t // tm,),
        in_specs=[
            pl.BlockSpec((n_slabs, tm, kw), lambda i: (0, i, 0)),
            in_hbm,
            pl.BlockSpec((tm, d), lambda i: (i, 0)),
            _resident((1, d)),
            in_hbm,
            in_hbm,
            _resident((1, d)),
        ],
        out_specs=pl.BlockSpec((tm, d), lambda i: (i, 0)),
        out_shape=jax.ShapeDtypeStruct((t, d), _F32),
        scratch_shapes=[
            pltpu.VMEM((tm, d), _BF16),
            pltpu.VMEM(w_out.shape, w_out.dtype),
            pltpu.VMEM(w_in.shape[1:], w_in.dtype),
            pltpu.VMEM(w_down.shape[1:], w_down.dtype),
            pltpu.SemaphoreType.DMA((n_copies,)),
        ],
        compiler_params=_params("arbitrary"),
        name="block_tail",
    )(a, w_out, h, gain.reshape(1, d), w_in, w_down, final_gain.reshape(1, d))


def _attn_kernel(*refs):
    ins = refs[:15]
    out_ref, o_scr, l_scr = refs[15:]
    first_block = pl.program_id(1) == 0
    row = lax.broadcasted_iota(jnp.int32, (SPAN, SPAN), 0)
    col = lax.broadcasted_iota(jnp.int32, (SPAN, SPAN), 1)
    bias_cur = jnp.where(col <= row, 0.0, NEG_BIG).astype(_F32)
    bias_prev = jnp.where(col >= row, 0.0, NEG_BIG).astype(_F32)
    bias_halo = bias_prev + jnp.where(first_block, NEG_BIG, 0.0).astype(_F32)
    ones = jnp.ones((SPAN, HEAD_DIM), _BF16)
    chain_len = 4
    body_units = 16

    for g, dil in enumerate(DILATIONS):
        q_ref, k_ref, v_ref, kh_ref, vh_ref = ins[5 * g:5 * g + 5]

        def load(ref, start, dil=dil):
            idx = pl.ds(start, SPAN) if dil == 1 else pl.ds(start, SPAN, stride=dil)
            return ref[idx, :].astype(_BF16)

        def load_kv(kref, vref, start, load=load):
            return load(kref, start), jnp.concatenate([load(vref, start), ones], axis=1)

        def chain(starts, prev, prev_bias, g=g, dil=dil, q_ref=q_ref, k_ref=k_ref, v_ref=v_ref,
                  load=load, load_kv=load_kv):
            for start in starts:
                kp, vp = prev
                kc, vc = cur = load_kv(k_ref, v_ref, start)
                s = _dot_nt(load(q_ref, start), jnp.concatenate([kp, kc], axis=0))
                s = s + jnp.concatenate([prev_bias, bias_cur], axis=1)
                m = jnp.max(jnp.maximum(s[:, :SPAN], s[:, SPAN:]), axis=-1, keepdims=True)
                p = jnp.exp2(s - m).astype(_BF16)
                r = _dot(p, jnp.concatenate([vp, vc], axis=0))
                denom = r[:, HEAD_DIM:]
                idx = pl.ds(start, SPAN) if dil == 1 else pl.ds(start, SPAN, stride=dil)
                o_scr[g, idx, :] = r[:, :HEAD_DIM] / denom
                l_scr[g, idx, :] = m + jnp.log2(denom)
                prev, prev_bias = cur, bias_prev

        block_rows = SPAN * dil
        if dil == DILATIONS[-1]:
            def body(i, carry, chain=chain, load_kv=load_kv, kh_ref=kh_ref, vh_ref=vh_ref):
                for u in range(body_units):
                    r = i * body_units + u
                    chain([r], load_kv(kh_ref, vh_ref, r), bias_halo)
                return carry
            lax.fori_loop(0, dil // body_units, body, 0)
        elif dil > 1:
            assert DILATIONS[-1] // dil == chain_len

            def body(i, carry, chain=chain, load_kv=load_kv, kh_ref=kh_ref, vh_ref=vh_ref,
                     block_rows=block_rows):
                for u in range(body_units // chain_len):
                    r = i * (body_units // chain_len) + u
                    chain([r + b * block_rows for b in range(chain_len)],
                          load_kv(kh_ref, vh_ref, r), bias_halo)
                return carry
            lax.fori_loop(0, dil * chain_len // body_units, body, 0)
        else:
            for first in range(0, DILATIONS[-1], chain_len):
                prev = (load_kv(kh_ref, vh_ref, 0) if first == 0
                        else load_kv(k_ref, v_ref, (first - 1) * block_rows))
                chain([(first + b) * block_rows for b in range(chain_len)], prev,
                      bias_halo if first == 0 else bias_prev)

    merge_rows = 256

    def merge(ci, carry):
        rr = pl.ds(pl.multiple_of(ci * merge_rows, merge_rows), merge_rows)
        l0, l1, l2 = l_scr[0, rr, :], l_scr[1, rr, :], l_scr[2, rr, :]
        m = jnp.maximum(jnp.maximum(l0, l1), l2)
        e0, e1, e2 = jnp.exp2(l0 - m), jnp.exp2(l1 - m), jnp.exp2(l2 - m)
        inv = 1.0 / (e0 + e1 + e2)
        out_ref[0, rr, :] = (o_scr[0, rr, :] * (e0 * inv)).astype(out_ref.dtype)
        out_ref[1, rr, :] = (o_scr[1, rr, :] * (e1 * inv)).astype(out_ref.dtype)
        out_ref[2, rr, :] = (o_scr[2, rr, :] * (e2 * inv)).astype(out_ref.dtype)
        return carry

    lax.fori_loop(0, ATTN_BLOCK // merge_rows, merge, 0)


def _attention(qkv, *, batch):
    t = qkv.shape[0]
    tb = ATTN_BLOCK
    steps = t // batch // tb
    in_specs, operands = [], []
    for g, dil in enumerate(DILATIONS):
        halo = SPAN * dil
        ratio = tb // halo

        def cur(which, g=g):
            return pl.BlockSpec(
                (tb, HEAD_DIM),
                lambda b, i, j: (b * steps + i, which * ATTN_HEADS + g * HEADS_PER_GROUP + j))

        def prev(which, g=g, ratio=ratio, halo=halo):
            return pl.BlockSpec(
                (halo, HEAD_DIM),
                lambda b, i, j: (jnp.maximum((b * steps + i) * ratio - 1, 0),
                                 which * ATTN_HEADS + g * HEADS_PER_GROUP + j))

        in_specs += [cur(0), cur(1), cur(2), prev(1), prev(2)]
        operands += [qkv] * 5
    n_groups = len(DILATIONS)
    return pl.pallas_call(
        _attn_kernel,
        grid=(batch, steps, HEADS_PER_GROUP),
        in_specs=in_specs,
        out_specs=pl.BlockSpec((n_groups, tb, HEAD_DIM), lambda b, i, j: (0, b * steps + i, j)),
        out_shape=jax.ShapeDtypeStruct((n_groups, t, HEADS_PER_GROUP * HEAD_DIM), _BF16),
        scratch_shapes=[pltpu.VMEM((3, tb, HEAD_DIM), _F32), pltpu.VMEM((3, tb, HEAD_DIM), _F32)],
        compiler_params=_params("parallel", "arbitrary", "arbitrary"),
        name="dilated_attention",
    )(*operands)


def _rope_tables(seq_len):
    inv_freq = 1.0 / (ROPE_THETA ** (np.arange(0, HEAD_DIM, 2, dtype=np.float64) / HEAD_DIM))
    ang = np.arange(seq_len, dtype=np.float64)[:, None] * inv_freq[None, :]
    cos, sin = np.cos(ang), np.sin(ang)
    return (jnp.asarray(np.concatenate([cos, cos], axis=-1), _F32),
            jnp.asarray(np.concatenate([-sin, sin], axis=-1), _F32))


def kernel(x, norm_mix, norm_ffn, hgrn_w_in, hgrn_lb_logits, hgrn_out_norm, hgrn_w_out,
           attn_w_qkv, attn_w_out, ffn_w_in, ffn_w_down, final_norm):
    batch, seq, d = x.shape
    cos2, sin2 = _rope_tables(seq)
    h = x.reshape(batch * seq, d)

    gated = _hgrn_mixer(h, norm_mix[0], hgrn_w_in[0], hgrn_lb_logits, hgrn_out_norm[0],
                        batch=batch, layer=0)
    ffn_in, ffn_down = ffn_w_in, ffn_w_down
    h = _block_tail(gated[None], hgrn_w_out[0], h, norm_ffn[0], ffn_in, ffn_down, final_norm,
                    layer=0, final_norm=False)

    qkv = _qkv_projection(h, norm_mix[1], attn_w_qkv[0], cos2, sin2,
                          scale=HEAD_DIM ** -0.5 * LOG2_E)
    attn = _attention(qkv, batch=batch)
    h = _block_tail(attn, attn_w_out[0], h, norm_ffn[1], ffn_in, ffn_down, final_norm,
                    layer=1, final_norm=True)
    return h.reshape(batch, seq, d)
```

```python
import functools

import numpy as np
import jax
import jax.numpy as jnp
from jax import lax
from jax.experimental import pallas as pl
from jax.experimental.pallas import tpu as pltpu

D_MODEL = 1024
HEAD_DIM = 128
HGRN_HEADS = 8
HGRN_CHUNK = 64
HGRN_LEVELS = 6
HGRN_STEP_TOKENS = 512
ATTN_HEADS = 12
ATTN_WIDTH = ATTN_HEADS * HEAD_DIM
DILATIONS = (1, 4, 16)
SPAN = 128
HEADS_PER_GROUP = 4
ATTN_BLOCK = SPAN * DILATIONS[-1]
ROW_TILE = 512
COL_TILE = 512
ROPE_THETA = 10000.0
NORM_EPS = 1e-6
NEG_BIG = -1e30
LOG2_E = float(np.log2(np.e))
VMEM_LIMIT_BYTES = 56 * 1024 * 1024

_F32 = jnp.float32
_BF16 = jnp.bfloat16


def _dot(a, b):
    return lax.dot_general(a, b, (((1,), (0,)), ((), ())), preferred_element_type=_F32)


def _dot_nt(a, b):
    return lax.dot_general(a, b, (((1,), (1,)), ((), ())), preferred_element_type=_F32)


def _dot_tn(a, b):
    return lax.dot_general(a, b, (((0,), (0,)), ((), ())), preferred_element_type=_F32)


def _sigmoid(x):
    return 1.0 / (1.0 + jnp.exp2(x * (-LOG2_E)))


def _rms_scale(x):
    return lax.rsqrt(jnp.mean(x * x, axis=-1, keepdims=True) + NORM_EPS)


def _params(*sem):
    return pltpu.CompilerParams(dimension_semantics=sem, vmem_limit_bytes=VMEM_LIMIT_BYTES)


def _resident(shape, layer=None):
    if layer is None:
        index = (0,) * len(shape)
    else:
        index = (layer,) + (0,) * (len(shape) - 1)
        shape = (None,) + tuple(shape[1:])
    return pl.BlockSpec(shape, lambda *_: index, pipeline_mode=pl.Buffered(1))


def _qkv_kernel(h_ref, gain_ref, w_ref, cos_ref, sin_ref, o_ref, *, scale):
    x = h_ref[...]
    u = (x * _rms_scale(x) * gain_ref[...]).astype(_BF16)
    cos, sin = cos_ref[...], sin_ref[...]
    cos_q, sin_q = cos * scale, sin * scale
    for j in range(w_ref.shape[1] // COL_TILE):
        res = _dot(u, w_ref[:, j * COL_TILE:(j + 1) * COL_TILE])
        for hh in range(COL_TILE // HEAD_DIM):
            lo = j * COL_TILE + hh * HEAD_DIM
            xh = res[:, hh * HEAD_DIM:(hh + 1) * HEAD_DIM]
            if lo < ATTN_WIDTH:
                xh = xh * cos_q + pltpu.roll(xh, HEAD_DIM // 2, 1) * sin_q
            elif lo < 2 * ATTN_WIDTH:
                xh = xh * cos + pltpu.roll(xh, HEAD_DIM // 2, 1) * sin
            o_ref[:, lo:lo + HEAD_DIM] = xh


def _qkv_projection(h, gain, w, cos2, sin2, *, scale):
    t, d = h.shape
    n = w.shape[1]
    tm = ROW_TILE
    seq_tiles = cos2.shape[0] // tm
    return pl.pallas_call(
        functools.partial(_qkv_kernel, scale=scale),
        grid=(t // tm,),
        in_specs=[
            pl.BlockSpec((tm, d), lambda i: (i, 0)),
            _resident((1, d)),
            _resident((d, n)),
            pl.BlockSpec((tm, HEAD_DIM), lambda i: (i % seq_tiles, 0)),
            pl.BlockSpec((tm, HEAD_DIM), lambda i: (i % seq_tiles, 0)),
        ],
        out_specs=pl.BlockSpec((tm, n), lambda i: (i, 0)),
        out_shape=jax.ShapeDtypeStruct((t, n), _F32),
        compiler_params=_params("parallel"),
        name="qkv_projection",
    )(h, gain.reshape(1, d), w, cos2, sin2)


def _hgrn_tables():
    c = HGRN_CHUNK
    assert 1 << HGRN_LEVELS == c
    t = np.arange(c)
    col = t[None, :]
    row = t[:, None]
    sums = np.zeros((HGRN_LEVELS + 1, c, c), np.float32)
    sums[0] = col <= row
    sums[1] = col > row
    masks = np.zeros((HGRN_LEVELS + 1, c, c), np.float32)
    for level in range(HGRN_LEVELS):
        half = c >> (level + 1)
        block = t // (2 * half)
        mid = block * 2 * half + half
        is_query = t >= mid
        if level < HGRN_LEVELS - 1:
            q_rows = (col >= mid[:, None]) & (col <= row) & is_query[:, None]
            k_rows = (col > row) & (col < mid[:, None]) & (~is_query)[:, None]
            sums[2 + level] = q_rows | k_rows
        masks[level] = ((block[:, None] == block[None, :]) & is_query[:, None]
                        & (~is_query)[None, :])
    masks[HGRN_LEVELS] = np.eye(c)
    assert np.array_equal(masks.sum(0), np.tril(np.ones((c, c))))
    sums = sums.reshape((HGRN_LEVELS + 1) * c, c)
    return np.concatenate([sums, sums], axis=1), masks


def _hgrn_kernel(h_ref, ngain_ref, w_ref, lbl_ref, gain_ref, sums_ref, masks_ref, o_ref,
                 state_ref, proj_ref, a2_ref, b2_ref, v2_ref, decay_ref, s_ref, *, layer):
    c = HGRN_CHUNK
    width = HGRN_HEADS * HEAD_DIM

    @pl.when(pl.program_id(1) == 0)
    def _():
        state_ref[...] = jnp.zeros_like(state_ref)

    x = h_ref[...]
    u = (x * _rms_scale(x) * ngain_ref[...]).astype(_BF16)
    for j in range(w_ref.shape[1] // COL_TILE):
        cols = slice(j * COL_TILE, (j + 1) * COL_TILE)
        proj_ref[:, cols] = _dot(u, w_ref[:, cols])

    logits = lbl_ref[...]
    e = jnp.exp(logits - jnp.max(logits, axis=0, keepdims=True))
    lb = jnp.sum(e[:layer + 1], axis=0, keepdims=True) / jnp.sum(e, axis=0, keepdims=True)
    out_gain = gain_ref[...]

    n_chunks = h_ref.shape[0] // c

    def chunk_rows(ci):
        return pl.ds(pl.multiple_of(ci * c, c), c)

    def prepare(ci, slot):
        a_ref, b_ref, v_ref = a2_ref.at[slot], b2_ref.at[slot], v2_ref.at[slot]
        rows = chunk_rows(ci)
        q = proj_ref[rows, 0:width]
        forget = lb + (1.0 - lb) * _sigmoid(proj_ref[rows, width:2 * width])
        glog = jnp.log(forget) * LOG2_E
        kk = 1.0 - forget
        qq = q * _sigmoid(q)
        g_hi = glog.astype(_BF16)
        g_lo = (glog - g_hi.astype(_F32)).astype(_BF16)
        factors = jnp.exp2(_dot(sums_ref[...], jnp.concatenate([g_hi, g_lo], axis=0)))

        qq = qq.astype(_BF16)
        kk = kk.astype(_BF16)
        from_start = factors[0:c]
        a_ref[0] = qq * from_start.astype(_BF16)
        decay_ref[slot] = from_start[c - 1:c, :]
        b_ref[0] = kk * factors[c:2 * c].astype(_BF16)
        for level in range(HGRN_LEVELS - 1):
            fac = factors[(2 + level) * c:(3 + level) * c].astype(_BF16)
            a_ref[1 + level] = qq * fac
            b_ref[1 + level] = kk * fac
        a_ref[HGRN_LEVELS] = qq * forget.astype(_BF16)
        a_ref[HGRN_LEVELS + 1] = qq
        b_ref[HGRN_LEVELS] = kk
        v_ref[...] = proj_ref[rows, 2 * width:3 * width].astype(_BF16)

    def lanes(h):
        return slice(h * HEAD_DIM, (h + 1) * HEAD_DIM)

    def consume(ci, slot):
        a_ref, b_ref, v_ref = a2_ref.at[slot], b2_ref.at[slot], v2_ref.at[slot]
        rows = chunk_rows(ci)
        chunk_decay = decay_ref[slot]
        pairs = [(h, h + 1) for h in range(0, HGRN_HEADS, 2)]
        for pair in pairs:
            scores = [jnp.zeros((c, c), _F32) for _ in pair]
            for level in range(HGRN_LEVELS - 1):
                for i, h in enumerate(pair):
                    scores[i] += masks_ref[level] * _dot_nt(a_ref[1 + level, :, lanes(h)],
                                                            b_ref[1 + level, :, lanes(h)])
            for i, h in enumerate(pair):
                both = a_ref[HGRN_LEVELS:HGRN_LEVELS + 2, :, lanes(h)].reshape(2 * c, HEAD_DIM)
                prod = _dot_nt(both, b_ref[HGRN_LEVELS, :, lanes(h)])
                scores[i] += (masks_ref[HGRN_LEVELS - 1] * prod[:c]
                              + masks_ref[HGRN_LEVELS] * prod[c:])
                s_ref[h] = scores[i].astype(_BF16)
        for pair in pairs:
            states = [state_ref[h] for h in pair]
            intra = [_dot(s_ref[h], v_ref[:, lanes(h)]) for h in pair]
            inter = [_dot(a_ref[0, :, lanes(h)], states[i].astype(_BF16))
                     for i, h in enumerate(pair)]
            update = [_dot_tn(b_ref[0, :, lanes(h)], v_ref[:, lanes(h)]) for h in pair]
            for i, h in enumerate(pair):
                decay_col = jnp.transpose(
                    jnp.broadcast_to(chunk_decay[:, lanes(h)], (8, HEAD_DIM)))[:, 0:1]
                state_ref[h] = states[i] * decay_col + update[i]
                o = intra[i] + inter[i]
                o = o * _rms_scale(o) * out_gain
                gate = proj_ref[rows, 3 * width + h * HEAD_DIM:3 * width + (h + 1) * HEAD_DIM]
                o_ref[rows, lanes(h)] = (o * (gate * _sigmoid(gate))).astype(o_ref.dtype)

    unroll = 8

    def some_chunks(i, carry):
        first = unroll * i
        for k in range(unroll):
            prepare(first + k + 1, (k + 1) % 2)
            consume(first + k, k % 2)
        return carry

    prepare(0, 0)
    lax.fori_loop(0, n_chunks // unroll - 1, some_chunks, 0)
    for ci in range(n_chunks - unroll, n_chunks):
        if ci + 1 < n_chunks:
            prepare(ci + 1, (ci + 1) % 2)
        consume(ci, ci % 2)


def _hgrn_mixer(h, norm_gain, w_in, lb_logits, out_gain, *, batch, layer):
    t, d = h.shape
    width = HGRN_HEADS * HEAD_DIM
    tc = HGRN_STEP_TOKENS
    steps = t // batch // tc
    sums, masks = _hgrn_tables()
    c = HGRN_CHUNK
    return pl.pallas_call(
        functools.partial(_hgrn_kernel, layer=layer),
        grid=(batch, steps),
        in_specs=[
            pl.BlockSpec((tc, d), lambda b, s: (b * steps + s, 0)),
            _resident((1, d)),
            _resident(w_in.shape),
            _resident(lb_logits.shape),
            _resident((1, HEAD_DIM)),
            _resident(sums.shape),
            _resident(masks.shape),
        ],
        out_specs=pl.BlockSpec((tc, width), lambda b, s: (b * steps + s, 0)),
        out_shape=jax.ShapeDtypeStruct((t, width), _BF16),
        scratch_shapes=[
            pltpu.VMEM((HGRN_HEADS, HEAD_DIM, HEAD_DIM), _F32),
            pltpu.VMEM((tc, 4 * width), _F32),
            pltpu.VMEM((2, HGRN_LEVELS + 2, c, width), _BF16),
            pltpu.VMEM((2, HGRN_LEVELS + 1, c, width), _BF16),
            pltpu.VMEM((2, c, width), _BF16),
            pltpu.VMEM((2, 1, width), _F32),
            pltpu.VMEM((HGRN_HEADS, c, c), _BF16),
        ],
        compiler_params=_params("parallel", "arbitrary"),
        name="hgrn_mixer",
    )(h, norm_gain.reshape(1, d), w_in, lb_logits, out_gain.reshape(1, HEAD_DIM),
      jnp.asarray(sums, _BF16), jnp.asarray(masks, _F32))


def _tail_kernel(a_ref, wo_hbm, h_ref, gain_ref, wi_hbm, wd_hbm, fgain_ref, o_ref,
                 u_ref, wo_ref, wi_ref, wd_ref, sem_ref, *, layer, final_norm, ff_tile):
    n_slabs, _, kw = a_ref.shape
    d_ff = wd_ref.shape[0]
    n_tiles = d_ff // ff_tile

    def weight_copies():
        copies = [pltpu.make_async_copy(wo_hbm, wo_ref, sem_ref.at[0])]
        for j in range(n_tiles):
            for half in range(2):
                cols = pl.ds(half * d_ff + j * ff_tile, ff_tile)
                copies.append(pltpu.make_async_copy(wi_hbm.at[layer, :, cols], wi_ref.at[:, cols],
                                                    sem_ref.at[len(copies)]))
            rows = pl.ds(j * ff_tile, ff_tile)
            copies.append(pltpu.make_async_copy(wd_hbm.at[layer, rows, :], wd_ref.at[rows, :],
                                                sem_ref.at[len(copies)]))
        return copies

    def body(copies):
        if copies:
            for cp in copies:
                cp.start()
            copies[0].wait()
        mixed = h_ref[...]
        for s in range(n_slabs):
            mixed += _dot(a_ref[s], wo_ref[s * kw:(s + 1) * kw, :])
        o_ref[...] = mixed
        u_ref[...] = (mixed * _rms_scale(mixed) * gain_ref[...]).astype(_BF16)
        for j in range(n_tiles):
            if copies:
                for cp in copies[1 + 3 * j:4 + 3 * j]:
                    cp.wait()
            u = u_ref[...]
            gate = _dot(u, wi_ref[:, j * ff_tile:(j + 1) * ff_tile])
            up = _dot(u, wi_ref[:, d_ff + j * ff_tile:d_ff + (j + 1) * ff_tile])
            act = (gate * _sigmoid(gate) * up).astype(_BF16)
            o_ref[...] += _dot(act, wd_ref[j * ff_tile:(j + 1) * ff_tile, :])
        if final_norm:
            y = o_ref[...]
            o_ref[...] = y * _rms_scale(y) * fgain_ref[...]

    first_step = pl.program_id(0) == 0

    @pl.when(first_step)
    def _():
        body(weight_copies())

    @pl.when(jnp.logical_not(first_step))
    def _():
        body(None)


def _block_tail(a, w_out, h, gain, w_in, w_down, final_gain, *, layer, final_norm, ff_tile=256):
    n_slabs, t, kw = a.shape
    d = h.shape[1]
    d_ff = w_down.shape[1]
    tm = ROW_TILE
    n_copies = 1 + 3 * (d_ff // ff_tile)
    in_hbm = pl.BlockSpec(memory_space=pl.ANY)
    return pl.pallas_call(
        functools.partial(_tail_kernel, layer=layer, final_norm=final_norm, ff_tile=ff_tile),
        grid=(t // tm,),
        in_specs=[
            pl.BlockSpec((n_slabs, tm, kw), lambda i: (0, i, 0)),
            in_hbm,
            pl.BlockSpec((tm, d), lambda i: (i, 0)),
            _resident((1, d)),
            in_hbm,
            in_hbm,
            _resident((1, d)),
        ],
        out_specs=pl.BlockSpec((tm, d), lambda i: (i, 0)),
        out_shape=jax.ShapeDtypeStruct((t, d), _F32),
        scratch_shapes=[
            pltpu.VMEM((tm, d), _BF16),
            pltpu.VMEM(w_out.shape, w_out.dtype),
            pltpu.VMEM(w_in.shape[1:], w_in.dtype),
            pltpu.VMEM(w_down.shape[1:], w_down.dtype),
            pltpu.SemaphoreType.DMA((n_copies,)),
        ],
        compiler_params=_params("arbitrary"),
        name="block_tail",
    )(a, w_out, h, gain.reshape(1, d), w_in, w_down, final_gain.reshape(1, d))


def _attn_kernel(*refs):
    ins = refs[:15]
    out_ref, o_scr, l_scr = refs[15:]
    first_block = pl.program_id(1) == 0
    row = lax.broadcasted_iota(jnp.int32, (SPAN, SPAN), 0)
    col = lax.broadcasted_iota(jnp.int32, (SPAN, SPAN), 1)
    bias_cur = jnp.where(col <= row, 0.0, NEG_BIG).astype(_F32)
    bias_prev = jnp.where(col >= row, 0.0, NEG_BIG).astype(_F32)
    bias_halo = bias_prev + jnp.where(first_block, NEG_BIG, 0.0).astype(_F32)
    ones = jnp.ones((SPAN, HEAD_DIM), _BF16)
    chain_len = 4
    body_units = 16

    for g, dil in enumerate(DILATIONS):
        q_ref, k_ref, v_ref, kh_ref, vh_ref = ins[5 * g:5 * g + 5]

        def load(ref, start, dil=dil):
            idx = pl.ds(start, SPAN) if dil == 1 else pl.ds(start, SPAN, stride=dil)
            return ref[idx, :].astype(_BF16)

        def load_kv(kref, vref, start, load=load):
            return load(kref, start), jnp.concatenate([load(vref, start), ones], axis=1)

        def chain(starts, prev, prev_bias, g=g, dil=dil, q_ref=q_ref, k_ref=k_ref, v_ref=v_ref,
                  load=load, load_kv=load_kv):
            for start in starts:
                kp, vp = prev
                kc, vc = cur = load_kv(k_ref, v_ref, start)
                s = _dot_nt(load(q_ref, start), jnp.concatenate([kp, kc], axis=0))
                s = s + jnp.concatenate([prev_bias, bias_cur], axis=1)
                m = jnp.max(jnp.maximum(s[:, :SPAN], s[:, SPAN:]), axis=-1, keepdims=True)
                p = jnp.exp2(s - m).astype(_BF16)
                r = _dot(p, jnp.concatenate([vp, vc], axis=0))
                denom = r[:, HEAD_DIM:]
                idx = pl.ds(start, SPAN) if dil == 1 else pl.ds(start, SPAN, stride=dil)
                o_scr[g, idx, :] = r[:, :HEAD_DIM] / denom
                l_scr[g, idx, :] = m + jnp.log2(denom)
                prev, prev_bias = cur, bias_prev

        block_rows = SPAN * dil
        if dil == DILATIONS[-1]:
            def body(i, carry, chain=chain, load_kv=load_kv, kh_ref=kh_ref, vh_ref=vh_ref):
                for u in range(body_units):
                    r = i * body_units + u
                    chain([r], load_kv(kh_ref, vh_ref, r), bias_halo)
                return carry
            lax.fori_loop(0, dil // body_units, body, 0)
        elif dil > 1:
            assert DILATIONS[-1] // dil == chain_len

            def body(i, carry, chain=chain, load_kv=load_kv, kh_ref=kh_ref, vh_ref=vh_ref,
                     block_rows=block_rows):
                for u in range(body_units // chain_len):
                    r = i * (body_units // chain_len) + u
                    chain([r + b * block_rows for b in range(chain_len)],
                          load_kv(kh_ref, vh_ref, r), bias_halo)
                return carry
            lax.fori_loop(0, dil * chain_len // body_units, body, 0)
        else:
            for first in range(0, DILATIONS[-1], chain_len):
                prev = (load_kv(kh_ref, vh_ref, 0) if first == 0
                        else load_kv(k_ref, v_ref, (first - 1) * block_rows))
                chain([(first + b) * block_rows for b in range(chain_len)], prev,
                      bias_halo if first == 0 else bias_prev)

    merge_rows = 256

    def merge(ci, carry):
        rr = pl.ds(pl.multiple_of(ci * merge_rows, merge_rows), merge_rows)
        l0, l1, l2 = l_scr[0, rr, :], l_scr[1, rr, :], l_scr[2, rr, :]
        m = jnp.maximum(jnp.maximum(l0, l1), l2)
        e0, e1, e2 = jnp.exp2(l0 - m), jnp.exp2(l1 - m), jnp.exp2(l2 - m)
        inv = 1.0 / (e0 + e1 + e2)
        out_ref[0, rr, :] = (o_scr[0, rr, :] * (e0 * inv)).astype(out_ref.dtype)
        out_ref[1, rr, :] = (o_scr[1, rr, :] * (e1 * inv)).astype(out_ref.dtype)
        out_ref[2, rr, :] = (o_scr[2, rr, :] * (e2 * inv)).astype(out_ref.dtype)
        return carry

    lax.fori_loop(0, ATTN_BLOCK // merge_rows, merge, 0)


def _attention(qkv, *, batch):
    t = qkv.shape[0]
    tb = ATTN_BLOCK
    steps = t // batch // tb
    in_specs, operands = [], []
    for g, dil in enumerate(DILATIONS):
        halo = SPAN * dil
        ratio = tb // halo

        def cur(which, g=g):
            return pl.BlockSpec(
                (tb, HEAD_DIM),
                lambda b, i, j: (b * steps + i, which * ATTN_HEADS + g * HEADS_PER_GROUP + j))

        def prev(which, g=g, ratio=ratio, halo=halo):
            return pl.BlockSpec(
                (halo, HEAD_DIM),
                lambda b, i, j: (jnp.maximum((b * steps + i) * ratio - 1, 0),
                                 which * ATTN_HEADS + g * HEADS_PER_GROUP + j))

        in_specs += [cur(0), cur(1), cur(2), prev(1), prev(2)]
        operands += [qkv] * 5
    n_groups = len(DILATIONS)
    return pl.pallas_call(
        _attn_kernel,
        grid=(batch, steps, HEADS_PER_GROUP),
        in_specs=in_specs,
        out_specs=pl.BlockSpec((n_groups, tb, HEAD_DIM), lambda b, i, j: (0, b * steps + i, j)),
        out_shape=jax.ShapeDtypeStruct((n_groups, t, HEADS_PER_GROUP * HEAD_DIM), _BF16),
        scratch_shapes=[pltpu.VMEM((3, tb, HEAD_DIM), _F32), pltpu.VMEM((3, tb, HEAD_DIM), _F32)],
        compiler_params=_params("parallel", "arbitrary", "arbitrary"),
        name="dilated_attention",
    )(*operands)


def _rope_tables(seq_len):
    inv_freq = 1.0 / (ROPE_THETA ** (np.arange(0, HEAD_DIM, 2, dtype=np.float64) / HEAD_DIM))
    ang = np.arange(seq_len, dtype=np.float64)[:, None] * inv_freq[None, :]
    cos, sin = np.cos(ang), np.sin(ang)
    return (jnp.asarray(np.concatenate([cos, cos], axis=-1), _F32),
            jnp.asarray(np.concatenate([-sin, sin], axis=-1), _F32))


def kernel(x, norm_mix, norm_ffn, hgrn_w_in, hgrn_lb_logits, hgrn_out_norm, hgrn_w_out,
           attn_w_qkv, attn_w_out, ffn_w_in, ffn_w_down, final_norm):
    batch, seq, d = x.shape
    cos2, sin2 = _rope_tables(seq)
    h = x.reshape(batch * seq, d)

    gated = _hgrn_mixer(h, norm_mix[0], hgrn_w_in[0], hgrn_lb_logits, hgrn_out_norm[0],
                        batch=batch, layer=0)
    ffn_in, ffn_down = ffn_w_in, ffn_w_down
    h = _block_tail(gated[None], hgrn_w_out[0], h, norm_ffn[0], ffn_in, ffn_down, final_norm,
                    layer=0, final_norm=False)

    qkv = _qkv_projection(h, norm_mix[1], attn_w_qkv[0], cos2, sin2,
                          scale=HEAD_DIM ** -0.5 * LOG2_E)
    attn = _attention(qkv, batch=batch)
    h = _block_tail(attn, attn_w_out[0], h, norm_ffn[1], ffn_in, ffn_down, final_norm,
                    layer=1, final_norm=True)
    return h.reshape(batch, seq, d)
```

```python
import functools

import numpy as np
import jax
import jax.numpy as jnp
from jax import lax
from jax.experimental import pallas as pl
from jax.experimental.pallas import tpu as pltpu

D_MODEL = 1024
HEAD_DIM = 128
HGRN_HEADS = 8
HGRN_CHUNK = 64
HGRN_LEVELS = 6
HGRN_STEP_TOKENS = 512
ATTN_HEADS = 12
ATTN_WIDTH = ATTN_HEADS * HEAD_DIM
DILATIONS = (1, 4, 16)
SPAN = 128
HEADS_PER_GROUP = 4
ATTN_BLOCK = SPAN * DILATIONS[-1]
ROW_TILE = 512
COL_TILE = 512
ROPE_THETA = 10000.0
NORM_EPS = 1e-6
NEG_BIG = -1e30
LOG2_E = float(np.log2(np.e))
VMEM_LIMIT_BYTES = 56 * 1024 * 1024

_F32 = jnp.float32
_BF16 = jnp.bfloat16


def _dot(a, b):
    return lax.dot_general(a, b, (((1,), (0,)), ((), ())), preferred_element_type=_F32)


def _dot_nt(a, b):
    return lax.dot_general(a, b, (((1,), (1,)), ((), ())), preferred_element_type=_F32)


def _dot_tn(a, b):
    return lax.dot_general(a, b, (((0,), (0,)), ((), ())), preferred_element_type=_F32)


def _sigmoid(x):
    return 1.0 / (1.0 + jnp.exp2(x * (-LOG2_E)))


def _rms_scale(x):
    return lax.rsqrt(jnp.mean(x * x, axis=-1, keepdims=True) + NORM_EPS)


def _params(*sem):
    return pltpu.CompilerParams(dimension_semantics=sem, vmem_limit_bytes=VMEM_LIMIT_BYTES)


def _resident(shape, layer=None):
    if layer is None:
        index = (0,) * len(shape)
    else:
        index = (layer,) + (0,) * (len(shape) - 1)
        shape = (None,) + tuple(shape[1:])
    return pl.BlockSpec(shape, lambda *_: index, pipeline_mode=pl.Buffered(1))


def _qkv_kernel(h_ref, gain_ref, w_ref, cos_ref, sin_ref, o_ref, *, scale):
    x = h_ref[...]
    u = (x * _rms_scale(x) * gain_ref[...]).astype(_BF16)
    cos, sin = cos_ref[...], sin_ref[...]
    cos_q, sin_q = cos * scale, sin * scale
    for j in range(w_ref.shape[1] // COL_TILE):
        res = _dot(u, w_ref[:, j * COL_TILE:(j + 1) * COL_TILE])
        for hh in range(COL_TILE // HEAD_DIM):
            lo = j * COL_TILE + hh * HEAD_DIM
            xh = res[:, hh * HEAD_DIM:(hh + 1) * HEAD_DIM]
            if lo < ATTN_WIDTH:
                xh = xh * cos_q + pltpu.roll(xh, HEAD_DIM // 2, 1) * sin_q
            elif lo < 2 * ATTN_WIDTH:
                xh = xh * cos + pltpu.roll(xh, HEAD_DIM // 2, 1) * sin
            o_ref[:, lo:lo + HEAD_DIM] = xh


def _qkv_projection(h, gain, w, cos2, sin2, *, scale):
    t, d = h.shape
    n = w.shape[1]
    tm = ROW_TILE
    seq_tiles = cos2.shape[0] // tm
    return pl.pallas_call(
        functools.partial(_qkv_kernel, scale=scale),
        grid=(t // tm,),
        in_specs=[
            pl.BlockSpec((tm, d), lambda i: (i, 0)),
            _resident((1, d)),
            _resident((d, n)),
            pl.BlockSpec((tm, HEAD_DIM), lambda i: (i % seq_tiles, 0)),
            pl.BlockSpec((tm, HEAD_DIM), lambda i: (i % seq_tiles, 0)),
        ],
        out_specs=pl.BlockSpec((tm, n), lambda i: (i, 0)),
        out_shape=jax.ShapeDtypeStruct((t, n), _F32),
        compiler_params=_params("parallel"),
        name="qkv_projection",
    )(h, gain.reshape(1, d), w, cos2, sin2)


def _hgrn_tables():
    c = HGRN_CHUNK
    assert 1 << HGRN_LEVELS == c
    t = np.arange(c)
    col = t[None, :]
    row = t[:, None]
    sums = np.zeros((HGRN_LEVELS + 1, c, c), np.float32)
    sums[0] = col <= row
    sums[1] = col > row
    masks = np.zeros((HGRN_LEVELS + 1, c, c), np.float32)
    for level in range(HGRN_LEVELS):
        half = c >> (level + 1)
        block = t // (2 * half)
        mid = block * 2 * half + half
        is_query = t >= mid
        if level < HGRN_LEVELS - 1:
            q_rows = (col >= mid[:, None]) & (col <= row) & is_query[:, None]
            k_rows = (col > row) & (col < mid[:, None]) & (~is_query)[:, None]
            sums[2 + level] = q_rows | k_rows
        masks[level] = ((block[:, None] == block[None, :]) & is_query[:, None]
                        & (~is_query)[None, :])
    masks[HGRN_LEVELS] = np.eye(c)
    assert np.array_equal(masks.sum(0), np.tril(np.ones((c, c))))
    sums = sums.reshape((HGRN_LEVELS + 1) * c, c)
    return np.concatenate([sums, sums], axis=1), masks


def _hgrn_kernel(h_ref, ngain_ref, w_ref, lbl_ref, gain_ref, sums_ref, masks_ref, o_ref,
                 state_ref, proj_ref, a2_ref, b2_ref, v2_ref, decay_ref, s_ref, *, layer):
    c = HGRN_CHUNK
    width = HGRN_HEADS * HEAD_DIM

    @pl.when(pl.program_id(1) == 0)
    def _():
        state_ref[...] = jnp.zeros_like(state_ref)

    x = h_ref[...]
    u = (x * _rms_scale(x) * ngain_ref[...]).astype(_BF16)
    half_rows = x.shape[0] // 2
    for half in range(2):
        rows = slice(half * half_rows, (half + 1) * half_rows)
        for j in range(w_ref.shape[1] // COL_TILE):
            cols = slice(j * COL_TILE, (j + 1) * COL_TILE)
            proj_ref[rows, cols] = _dot(u[rows], w_ref[:, cols])

    logits = lbl_ref[...]
    e = jnp.exp(logits - jnp.max(logits, axis=0, keepdims=True))
    lb = jnp.sum(e[:layer + 1], axis=0, keepdims=True) / jnp.sum(e, axis=0, keepdims=True)
    out_gain = gain_ref[...]

    n_chunks = h_ref.shape[0] // c

    def chunk_rows(ci):
        return pl.ds(pl.multiple_of(ci * c, c), c)

    def prepare(ci, slot):
        a_ref, b_ref, v_ref = a2_ref.at[slot], b2_ref.at[slot], v2_ref.at[slot]
        rows = chunk_rows(ci)
        q = proj_ref[rows, 0:width]
        forget = lb + (1.0 - lb) * _sigmoid(proj_ref[rows, width:2 * width])
        glog = jnp.log(forget) * LOG2_E
        kk = 1.0 - forget
        qq = q * _sigmoid(q)
        g_hi = glog.astype(_BF16)
        g_lo = (glog - g_hi.astype(_F32)).astype(_BF16)
        factors = jnp.exp2(_dot(sums_ref[...], jnp.concatenate([g_hi, g_lo], axis=0)))

        qq = qq.astype(_BF16)
        kk = kk.astype(_BF16)
        from_start = factors[0:c]
        a_ref[0] = qq * from_start.astype(_BF16)
        decay_ref[slot] = from_start[c - 1:c, :]
        b_ref[0] = kk * factors[c:2 * c].astype(_BF16)
        for level in range(HGRN_LEVELS - 1):
            fac = factors[(2 + level) * c:(3 + level) * c].astype(_BF16)
            a_ref[1 + level] = qq * fac
            b_ref[1 + level] = kk * fac
        a_ref[HGRN_LEVELS] = qq * forget.astype(_BF16)
        a_ref[HGRN_LEVELS + 1] = qq
        b_ref[HGRN_LEVELS] = kk
        v_ref[...] = proj_ref[rows, 2 * width:3 * width].astype(_BF16)

    def lanes(h):
        return slice(h * HEAD_DIM, (h + 1) * HEAD_DIM)

    def consume(ci, slot):
        a_ref, b_ref, v_ref = a2_ref.at[slot], b2_ref.at[slot], v2_ref.at[slot]
        rows = chunk_rows(ci)
        chunk_decay = decay_ref[slot]
        pairs = [(h, h + 1) for h in range(0, HGRN_HEADS, 2)]
        for pair in pairs:
            scores = [jnp.zeros((c, c), _F32) for _ in pair]
            for level in range(HGRN_LEVELS - 1):
                for i, h in enumerate(pair):
                    scores[i] += masks_ref[level] * _dot_nt(a_ref[1 + level, :, lanes(h)],
                                                            b_ref[1 + level, :, lanes(h)])
            for i, h in enumerate(pair):
                both = a_ref[HGRN_LEVELS:HGRN_LEVELS + 2, :, lanes(h)].reshape(2 * c, HEAD_DIM)
                prod = _dot_nt(both, b_ref[HGRN_LEVELS, :, lanes(h)])
                scores[i] += (masks_ref[HGRN_LEVELS - 1] * prod[:c]
                              + masks_ref[HGRN_LEVELS] * prod[c:])
                s_ref[h] = scores[i].astype(_BF16)
        for pair in pairs:
            states = [state_ref[h] for h in pair]
            intra = [_dot(s_ref[h], v_ref[:, lanes(h)]) for h in pair]
            inter = [_dot(a_ref[0, :, lanes(h)], states[i].astype(_BF16))
                     for i, h in enumerate(pair)]
            update = [_dot_tn(b_ref[0, :, lanes(h)], v_ref[:, lanes(h)]) for h in pair]
            for i, h in enumerate(pair):
                decay_col = jnp.transpose(
                    jnp.broadcast_to(chunk_decay[:, lanes(h)], (8, HEAD_DIM)))[:, 0:1]
                state_ref[h] = states[i] * decay_col + update[i]
                o = intra[i] + inter[i]
                o = o * _rms_scale(o) * out_gain
                gate = proj_ref[rows, 3 * width + h * HEAD_DIM:3 * width + (h + 1) * HEAD_DIM]
                o_ref[rows, lanes(h)] = (o * (gate * _sigmoid(gate))).astype(o_ref.dtype)

    unroll = 8

    def some_chunks(i, carry):
        first = unroll * i
        for k in range(unroll):
            prepare(first + k + 1, (k + 1) % 2)
            consume(first + k, k % 2)
        return carry

    prepare(0, 0)
    lax.fori_loop(0, n_chunks // unroll - 1, some_chunks, 0)
    for ci in range(n_chunks - unroll, n_chunks):
        if ci + 1 < n_chunks:
            prepare(ci + 1, (ci + 1) % 2)
        consume(ci, ci % 2)


def _hgrn_mixer(h, norm_gain, w_in, lb_logits, out_gain, *, batch, layer):
    t, d = h.shape
    width = HGRN_HEADS * HEAD_DIM
    tc = HGRN_STEP_TOKENS
    steps = t // batch // tc
    sums, masks = _hgrn_tables()
    c = HGRN_CHUNK
    return pl.pallas_call(
        functools.partial(_hgrn_kernel, layer=layer),
        grid=(batch, steps),
        in_specs=[
            pl.BlockSpec((tc, d), lambda b, s: (b * steps + s, 0)),
            _resident((1, d)),
            _resident(w_in.shape),
            _resident(lb_logits.shape),
            _resident((1, HEAD_DIM)),
            _resident(sums.shape),
            _resident(masks.shape),
        ],
        out_specs=pl.BlockSpec((tc, width), lambda b, s: (b * steps + s, 0)),
        out_shape=jax.ShapeDtypeStruct((t, width), _BF16),
        scratch_shapes=[
            pltpu.VMEM((HGRN_HEADS, HEAD_DIM, HEAD_DIM), _F32),
            pltpu.VMEM((tc, 4 * width), _F32),
            pltpu.VMEM((2, HGRN_LEVELS + 2, c, width), _BF16),
            pltpu.VMEM((2, HGRN_LEVELS + 1, c, width), _BF16),
            pltpu.VMEM((2, c, width), _BF16),
            pltpu.VMEM((2, 1, width), _F32),
            pltpu.VMEM((HGRN_HEADS, c, c), _BF16),
        ],
        compiler_params=_params("parallel", "arbitrary"),
        name="hgrn_mixer",
    )(h, norm_gain.reshape(1, d), w_in, lb_logits, out_gain.reshape(1, HEAD_DIM),
      jnp.asarray(sums, _BF16), jnp.asarray(masks, _F32))


def _tail_kernel(a_ref, wo_hbm, h_ref, gain_ref, wi_hbm, wd_hbm, fgain_ref, o_ref,
                 u_ref, wo_ref, wi_ref, wd_ref, sem_ref, *, layer, final_norm, ff_tile):
    n_slabs, _, kw = a_ref.shape
    d_ff = wd_ref.shape[0]
    n_tiles = d_ff // ff_tile

    def weight_copies():
        copies = [pltpu.make_async_copy(wo_hbm, wo_ref, sem_ref.at[0])]
        for j in range(n_tiles):
            for half in range(2):
                cols = pl.ds(half * d_ff + j * ff_tile, ff_tile)
                copies.append(pltpu.make_async_copy(wi_hbm.at[layer, :, cols], wi_ref.at[:, cols],
                                                    sem_ref.at[len(copies)]))
            rows = pl.ds(j * ff_tile, ff_tile)
            copies.append(pltpu.make_async_copy(wd_hbm.at[layer, rows, :], wd_ref.at[rows, :],
                                                sem_ref.at[len(copies)]))
        return copies

    def body(copies):
        if copies:
            for cp in copies:
                cp.start()
            copies[0].wait()
        mixed = h_ref[...]
        for s in range(n_slabs):
            mixed += _dot(a_ref[s], wo_ref[s * kw:(s + 1) * kw, :])
        o_ref[...] = mixed
        u_ref[...] = (mixed * _rms_scale(mixed) * gain_ref[...]).astype(_BF16)
        for j in range(n_tiles):
            if copies:
                for cp in copies[1 + 3 * j:4 + 3 * j]:
                    cp.wait()
            u = u_ref[...]
            gate = _dot(u, wi_ref[:, j * ff_tile:(j + 1) * ff_tile])
            up = _dot(u, wi_ref[:, d_ff + j * ff_tile:d_ff + (j + 1) * ff_tile])
            act = (gate * _sigmoid(gate) * up).astype(_BF16)
            o_ref[...] += _dot(act, wd_ref[j * ff_tile:(j + 1) * ff_tile, :])
        if final_norm:
            y = o_ref[...]
            o_ref[...] = y * _rms_scale(y) * fgain_ref[...]

    first_step = pl.program_id(0) == 0

    @pl.when(first_step)
    def _():
        body(weight_copies())

    @pl.when(jnp.logical_not(first_step))
    def _():
        body(None)


def _block_tail(a, w_out, h, gain, w_in, w_down, final_gain, *, layer, final_norm, ff_tile=256):
    n_slabs, t, kw = a.shape
    d = h.shape[1]
    d_ff = w_down.shape[1]
    tm = ROW_TILE
    n_copies = 1 + 3 * (d_ff // ff_tile)
    in_hbm = pl.BlockSpec(memory_space=pl.ANY)
    return pl.pallas_call(
        functools.partial(_tail_kernel, layer=layer, final_norm=final_norm, ff_tile=ff_tile),
        grid=(t // tm,),
        in_specs=[
            pl.BlockSpec((n_slabs, tm, kw), lambda i: (0, i, 0)),
            in_hbm,
            pl.BlockSpec((tm, d), lambda i: (i, 0)),
            _resident((1, d)),
            in_hbm,
            in_hbm,
            _resident((1, d)),
        ],
        out_specs=pl.BlockSpec((tm, d), lambda i: (i, 0)),
        out_shape=jax.ShapeDtypeStruct((t, d), _F32),
        scratch_shapes=[
            pltpu.VMEM((tm, d), _BF16),
            pltpu.VMEM(w_out.shape, w_out.dtype),
            pltpu.VMEM(w_in.shape[1:], w_in.dtype),
            pltpu.VMEM(w_down.shape[1:], w_down.dtype),
            pltpu.SemaphoreType.DMA((n_copies,)),
        ],
        compiler_params=_params("arbitrary"),
        name="block_tail",
    )(a, w_out, h, gain.reshape(1, d), w_in, w_down, final_gain.reshape(1, d))


def _attn_kernel(*refs):
    ins = refs[:15]
    out_ref, o_scr, l_scr = refs[15:]
    first_block = pl.program_id(1) == 0
    row = lax.broadcasted_iota(jnp.int32, (SPAN, SPAN), 0)
    col = lax.broadcasted_iota(jnp.int32, (SPAN, SPAN), 1)
    bias_cur = jnp.where(col <= row, 0.0, NEG_BIG).astype(_F32)
    bias_prev = jnp.where(col >= row, 0.0, NEG_BIG).astype(_F32)
    bias_halo = bias_prev + jnp.where(first_block, NEG_BIG, 0.0).astype(_F32)
    ones = jnp.ones((SPAN, HEAD_DIM), _BF16)
    chain_len = 4
    body_units = 16

    for g, dil in enumerate(DILATIONS):
        q_ref, k_ref, v_ref, kh_ref, vh_ref = ins[5 * g:5 * g + 5]

        def load(ref, start, dil=dil):
            idx = pl.ds(start, SPAN) if dil == 1 else pl.ds(start, SPAN, stride=dil)
            return ref[idx, :].astype(_BF16)

        def load_kv(kref, vref, start, load=load):
            return load(kref, start), jnp.concatenate([load(vref, start), ones], axis=1)

        def chain(starts, prev, prev_bias, g=g, dil=dil, q_ref=q_ref, k_ref=k_ref, v_ref=v_ref,
                  load=load, load_kv=load_kv):
            for start in starts:
                kp, vp = prev
                kc, vc = cur = load_kv(k_ref, v_ref, start)
                s = _dot_nt(load(q_ref, start), jnp.concatenate([kp, kc], axis=0))
                s = s + jnp.concatenate([prev_bias, bias_cur], axis=1)
                m = jnp.max(jnp.maximum(s[:, :SPAN], s[:, SPAN:]), axis=-1, keepdims=True)
                p = jnp.exp2(s - m).astype(_BF16)
                r = _dot(p, jnp.concatenate([vp, vc], axis=0))
                denom = r[:, HEAD_DIM:]
                idx = pl.ds(start, SPAN) if dil == 1 else pl.ds(start, SPAN, stride=dil)
                o_scr[g, idx, :] = r[:, :HEAD_DIM] / denom
                l_scr[g, idx, :] = m + jnp.log2(denom)
                prev, prev_bias = cur, bias_prev

        block_rows = SPAN * dil
        if dil == DILATIONS[-1]:
            def body(i, carry, chain=chain, load_kv=load_kv, kh_ref=kh_ref, vh_ref=vh_ref):
                for u in range(body_units):
                    r = i * body_units + u
                    chain([r], load_kv(kh_ref, vh_ref, r), bias_halo)
                return carry
            lax.fori_loop(0, dil // body_units, body, 0)
        elif dil > 1:
            assert DILATIONS[-1] // dil == chain_len

            def body(i, carry, chain=chain, load_kv=load_kv, kh_ref=kh_ref, vh_ref=vh_ref,
                     block_rows=block_rows):
                for u in range(body_units // chain_len):
                    r = i * (body_units // chain_len) + u
                    chain([r + b * block_rows for b in range(chain_len)],
                          load_kv(kh_ref, vh_ref, r), bias_halo)
                return carry
            lax.fori_loop(0, dil * chain_len // body_units, body, 0)
        else:
            for first in range(0, DILATIONS[-1], chain_len):
                prev = (load_kv(kh_ref, vh_ref, 0) if first == 0
                        else load_kv(k_ref, v_ref, (first - 1) * block_rows))
                chain([(first + b) * block_rows for b in range(chain_len)], prev,
                      bias_halo if first == 0 else bias_prev)

    merge_rows = 256

    def merge(ci, carry):
        rr = pl.ds(pl.multiple_of(ci * merge_rows, merge_rows), merge_rows)
        l0, l1, l2 = l_scr[0, rr, :], l_scr[1, rr, :], l_scr[2, rr, :]
        m = jnp.maximum(jnp.maximum(l0, l1), l2)
        e0, e1, e2 = jnp.exp2(l0 - m), jnp.exp2(l1 - m), jnp.exp2(l2 - m)
        inv = 1.0 / (e0 + e1 + e2)
        out_ref[0, rr, :] = (o_scr[0, rr, :] * (e0 * inv)).astype(out_ref.dtype)
        out_ref[1, rr, :] = (o_scr[1, rr, :] * (e1 * inv)).astype(out_ref.dtype)
        out_ref[2, rr, :] = (o_scr[2, rr, :] * (e2 * inv)).astype(out_ref.dtype)
        return carry

    lax.fori_loop(0, ATTN_BLOCK // merge_rows, merge, 0)


def _attention(qkv, *, batch):
    t = qkv.shape[0]
    tb = ATTN_BLOCK
    steps = t // batch // tb
    in_specs, operands = [], []
    for g, dil in enumerate(DILATIONS):
        halo = SPAN * dil
        ratio = tb // halo

        def cur(which, g=g):
            return pl.BlockSpec(
                (tb, HEAD_DIM),
                lambda b, i, j: (b * steps + i, which * ATTN_HEADS + g * HEADS_PER_GROUP + j))

        def prev(which, g=g, ratio=ratio, halo=halo):
            return pl.BlockSpec(
                (halo, HEAD_DIM),
                lambda b, i, j: (jnp.maximum((b * steps + i) * ratio - 1, 0),
                                 which * ATTN_HEADS + g * HEADS_PER_GROUP + j))

        in_specs += [cur(0), cur(1), cur(2), prev(1), prev(2)]
        operands += [qkv] * 5
    n_groups = len(DILATIONS)
    return pl.pallas_call(
        _attn_kernel,
        grid=(batch, steps, HEADS_PER_GROUP),
        in_specs=in_specs,
        out_specs=pl.BlockSpec((n_groups, tb, HEAD_DIM), lambda b, i, j: (0, b * steps + i, j)),
        out_shape=jax.ShapeDtypeStruct((n_groups, t, HEADS_PER_GROUP * HEAD_DIM), _BF16),
        scratch_shapes=[pltpu.VMEM((3, tb, HEAD_DIM), _F32), pltpu.VMEM((3, tb, HEAD_DIM), _F32)],
        compiler_params=_params("parallel", "arbitrary", "arbitrary"),
        name="dilated_attention",
    )(*operands)


def _rope_tables(seq_len):
    inv_freq = 1.0 / (ROPE_THETA ** (np.arange(0, HEAD_DIM, 2, dtype=np.float64) / HEAD_DIM))
    ang = np.arange(seq_len, dtype=np.float64)[:, None] * inv_freq[None, :]
    cos, sin = np.cos(ang), np.sin(ang)
    return (jnp.asarray(np.concatenate([cos, cos], axis=-1), _F32),
            jnp.asarray(np.concatenate([-sin, sin], axis=-1), _F32))


def kernel(x, norm_mix, norm_ffn, hgrn_w_in, hgrn_lb_logits, hgrn_out_norm, hgrn_w_out,
           attn_w_qkv, attn_w_out, ffn_w_in, ffn_w_down, final_norm):
    batch, seq, d = x.shape
    cos2, sin2 = _rope_tables(seq)
    h = x.reshape(batch * seq, d)

    gated = _hgrn_mixer(h, norm_mix[0], hgrn_w_in[0], hgrn_lb_logits, hgrn_out_norm[0],
                        batch=batch, layer=0)
    ffn_in, ffn_down = ffn_w_in, ffn_w_down
    h = _block_tail(gated[None], hgrn_w_out[0], h, norm_ffn[0], ffn_in, ffn_down, final_norm,
                    layer=0, final_norm=False)

    qkv = _qkv_projection(h, norm_mix[1], attn_w_qkv[0], cos2, sin2,
                          scale=HEAD_DIM ** -0.5 * LOG2_E)
    attn = _attention(qkv, batch=batch)
    h = _block_tail(attn, attn_w_out[0], h, norm_ffn[1], ffn_in, ffn_down, final_norm,
                    layer=1, final_norm=True)
    return h.reshape(batch, seq, d)
```

```python
import functools

import numpy as np
import jax
import jax.numpy as jnp
from jax import lax
from jax.experimental import pallas as pl
from jax.experimental.pallas import tpu as pltpu

HEAD_DIM = 128
HGRN_HEADS = 8
HGRN_CHUNK = 64
HGRN_LEVELS = 6
HGRN_STEP_TOKENS = 512
HGRN_SLOTS = 2
ATTN_HEADS = 12
ATTN_WIDTH = ATTN_HEADS * HEAD_DIM
DILATIONS = (1, 4, 16)
SPAN = 128
HEADS_PER_GROUP = 4
ATTN_BLOCK = SPAN * DILATIONS[-1]
ROW_TILE = 512
COL_TILE = 512
ROPE_THETA = 10000.0
NORM_EPS = 1e-6
NEG_BIG = -1e30
LOG2_E = float(np.log2(np.e))
VMEM_LIMIT_BYTES = 56 * 1024 * 1024

_F32 = jnp.float32
_BF16 = jnp.bfloat16


def _dot(a, b):
    return lax.dot_general(a, b, (((1,), (0,)), ((), ())), preferred_element_type=_F32)


def _dot_nt(a, b):
    return lax.dot_general(a, b, (((1,), (1,)), ((), ())), preferred_element_type=_F32)


def _dot_tn(a, b):
    return lax.dot_general(a, b, (((0,), (0,)), ((), ())), preferred_element_type=_F32)


def _sigmoid(x):
    return 1.0 / (1.0 + jnp.exp2(x * (-LOG2_E)))


def _rms_scale(x):
    return lax.rsqrt(jnp.mean(x * x, axis=-1, keepdims=True) + NORM_EPS)


def _params(*sem):
    return pltpu.CompilerParams(dimension_semantics=sem, vmem_limit_bytes=VMEM_LIMIT_BYTES)


def _resident(shape):
    index = (0,) * len(shape)
    return pl.BlockSpec(shape, lambda *_: index, pipeline_mode=pl.Buffered(1))


def _qkv_kernel(h_ref, gain_ref, w_ref, cos_ref, sin_ref, o_ref, *, scale):
    x = h_ref[...]
    u = (x * _rms_scale(x) * gain_ref[...]).astype(_BF16)
    cos, sin = cos_ref[...], sin_ref[...]
    cos_q, sin_q = cos * scale, sin * scale
    for j in range(w_ref.shape[1] // COL_TILE):
        res = _dot(u, w_ref[:, j * COL_TILE:(j + 1) * COL_TILE])
        for hh in range(COL_TILE // HEAD_DIM):
            lo = j * COL_TILE + hh * HEAD_DIM
            xh = res[:, hh * HEAD_DIM:(hh + 1) * HEAD_DIM]
            if lo < ATTN_WIDTH:
                xh = xh * cos_q + pltpu.roll(xh, HEAD_DIM // 2, 1) * sin_q
            elif lo < 2 * ATTN_WIDTH:
                xh = xh * cos + pltpu.roll(xh, HEAD_DIM // 2, 1) * sin
            o_ref[:, lo:lo + HEAD_DIM] = xh


def _qkv_projection(h, gain, w, cos2, sin2, *, scale):
    t, d = h.shape
    n = w.shape[1]
    tm = ROW_TILE
    seq_tiles = cos2.shape[0] // tm
    return pl.pallas_call(
        functools.partial(_qkv_kernel, scale=scale),
        grid=(t // tm,),
        in_specs=[
            pl.BlockSpec((tm, d), lambda i: (i, 0)),
            _resident((1, d)),
            _resident((d, n)),
            pl.BlockSpec((tm, HEAD_DIM), lambda i: (i % seq_tiles, 0)),
            pl.BlockSpec((tm, HEAD_DIM), lambda i: (i % seq_tiles, 0)),
        ],
        out_specs=pl.BlockSpec((tm, n), lambda i: (i, 0)),
        out_shape=jax.ShapeDtypeStruct((t, n), _F32),
        compiler_params=_params("parallel"),
        name="qkv_projection",
    )(h, gain.reshape(1, d), w, cos2, sin2)


def _hgrn_tables():
    c = HGRN_CHUNK
    assert 1 << HGRN_LEVELS == c
    t = np.arange(c)
    col = t[None, :]
    row = t[:, None]
    sums = np.zeros((HGRN_LEVELS + 1, c, c), np.float32)
    sums[0] = col <= row
    sums[1] = col > row
    masks = np.zeros((HGRN_LEVELS + 1, c, c), np.float32)
    for level in range(HGRN_LEVELS):
        half = c >> (level + 1)
        block = t // (2 * half)
        mid = block * 2 * half + half
        is_query = t >= mid
        if level < HGRN_LEVELS - 1:
            q_rows = (col >= mid[:, None]) & (col <= row) & is_query[:, None]
            k_rows = (col > row) & (col < mid[:, None]) & (~is_query)[:, None]
            sums[2 + level] = q_rows | k_rows
        masks[level] = ((block[:, None] == block[None, :]) & is_query[:, None]
                        & (~is_query)[None, :])
    masks[HGRN_LEVELS] = np.eye(c)
    assert np.array_equal(masks.sum(0), np.tril(np.ones((c, c))))
    sums = sums.reshape((HGRN_LEVELS + 1) * c, c)
    return np.concatenate([sums, sums], axis=1), masks


def _hgrn_kernel(h_ref, ngain_ref, w_ref, lbl_ref, gain_ref, sums_ref, masks_ref, o_ref,
                 state_ref, proj_ref, a2_ref, b2_ref, v2_ref, decay_ref, s_ref, *, layer):
    c = HGRN_CHUNK
    width = HGRN_HEADS * HEAD_DIM

    @pl.when(pl.program_id(1) == 0)
    def _():
        state_ref[...] = jnp.zeros_like(state_ref)

    x = h_ref[...]
    u = (x * _rms_scale(x) * ngain_ref[...]).astype(_BF16)
    half_rows = x.shape[0] // 2
    for half in range(2):
        rows = slice(half * half_rows, (half + 1) * half_rows)
        for j in range(w_ref.shape[1] // COL_TILE):
            cols = slice(j * COL_TILE, (j + 1) * COL_TILE)
            proj_ref[rows, cols] = _dot(u[rows], w_ref[:, cols])

    logits = lbl_ref[...]
    e = jnp.exp(logits - jnp.max(logits, axis=0, keepdims=True))
    lb = jnp.sum(e[:layer + 1], axis=0, keepdims=True) / jnp.sum(e, axis=0, keepdims=True)
    out_gain = gain_ref[...]

    n_chunks = h_ref.shape[0] // c

    def chunk_rows(ci):
        return pl.ds(pl.multiple_of(ci * c, c), c)

    def prepare(ci, slot):
        a_ref, b_ref, v_ref = a2_ref.at[slot], b2_ref.at[slot], v2_ref.at[slot]
        rows = chunk_rows(ci)
        q = proj_ref[rows, 0:width]
        forget = lb + (1.0 - lb) * _sigmoid(proj_ref[rows, width:2 * width])
        glog = jnp.log(forget) * LOG2_E
        kk = 1.0 - forget
        qq = q * _sigmoid(q)
        g_hi = glog.astype(_BF16)
        g_lo = (glog - g_hi.astype(_F32)).astype(_BF16)
        factors = jnp.exp2(_dot(sums_ref[...], jnp.concatenate([g_hi, g_lo], axis=0)))

        qq = qq.astype(_BF16)
        kk = kk.astype(_BF16)
        from_start = factors[0:c]
        a_ref[0] = qq * from_start.astype(_BF16)
        decay_ref[slot] = from_start[c - 1:c, :]
        b_ref[0] = kk * factors[c:2 * c].astype(_BF16)
        for level in range(HGRN_LEVELS - 1):
            fac = factors[(2 + level) * c:(3 + level) * c].astype(_BF16)
            a_ref[1 + level] = qq * fac
            b_ref[1 + level] = kk * fac
        a_ref[HGRN_LEVELS] = qq * forget.astype(_BF16)
        a_ref[HGRN_LEVELS + 1] = qq
        b_ref[HGRN_LEVELS] = kk
        v_ref[...] = proj_ref[rows, 2 * width:3 * width].astype(_BF16)

    def lanes(h):
        return slice(h * HEAD_DIM, (h + 1) * HEAD_DIM)

    def consume(ci, slot):
        a_ref, b_ref, v_ref = a2_ref.at[slot], b2_ref.at[slot], v2_ref.at[slot]
        rows = chunk_rows(ci)
        chunk_decay = decay_ref[slot]
        pairs = [(h, h + 1) for h in range(0, HGRN_HEADS, 2)]
        for pair in pairs:
            scores = [jnp.zeros((c, c), _F32) for _ in pair]
            for level in range(HGRN_LEVELS - 1):
                for i, h in enumerate(pair):
                    scores[i] += masks_ref[level] * _dot_nt(a_ref[1 + level, :, lanes(h)],
                                                            b_ref[1 + level, :, lanes(h)])
            for i, h in enumerate(pair):
                both = a_ref[HGRN_LEVELS:HGRN_LEVELS + 2, :, lanes(h)].reshape(2 * c, HEAD_DIM)
                prod = _dot_nt(both, b_ref[HGRN_LEVELS, :, lanes(h)])
                scores[i] += (masks_ref[HGRN_LEVELS - 1] * prod[:c]
                              + masks_ref[HGRN_LEVELS] * prod[c:])
                s_ref[h] = scores[i].astype(_BF16)
        for pair in pairs:
            states = [state_ref[h] for h in pair]
            intra = [_dot(s_ref[h], v_ref[:, lanes(h)]) for h in pair]
            inter = [_dot(a_ref[0, :, lanes(h)], states[i].astype(_BF16))
                     for i, h in enumerate(pair)]
            update = [_dot_tn(b_ref[0, :, lanes(h)], v_ref[:, lanes(h)]) for h in pair]
            for i, h in enumerate(pair):
                decay_col = jnp.transpose(
                    jnp.broadcast_to(chunk_decay[:, lanes(h)], (8, HEAD_DIM)))[:, 0:1]
                state_ref[h] = states[i] * decay_col + update[i]
                o = intra[i] + inter[i]
                o = o * _rms_scale(o) * out_gain
                gate = proj_ref[rows, 3 * width + h * HEAD_DIM:3 * width + (h + 1) * HEAD_DIM]
                o_ref[rows, lanes(h)] = (o * (gate * _sigmoid(gate))).astype(o_ref.dtype)

    n_slots = a2_ref.shape[0]
    ahead = n_slots - 1
    for ci in range(min(ahead, n_chunks)):
        prepare(ci, ci % n_slots)
    for ci in range(n_chunks):
        if ci + ahead < n_chunks:
            prepare(ci + ahead, (ci + ahead) % n_slots)
        consume(ci, ci % n_slots)


def _hgrn_mixer(h, norm_gain, w_in, lb_logits, out_gain, *, batch, layer):
    t, d = h.shape
    width = HGRN_HEADS * HEAD_DIM
    tc = HGRN_STEP_TOKENS
    steps = t // batch // tc
    sums, masks = _hgrn_tables()
    c = HGRN_CHUNK
    return pl.pallas_call(
        functools.partial(_hgrn_kernel, layer=layer),
        grid=(batch, steps),
        in_specs=[
            pl.BlockSpec((tc, d), lambda b, s: (b * steps + s, 0)),
            _resident((1, d)),
            _resident(w_in.shape),
            _resident(lb_logits.shape),
            _resident((1, HEAD_DIM)),
            _resident(sums.shape),
            _resident(masks.shape),
        ],
        out_specs=pl.BlockSpec((tc, width), lambda b, s: (b * steps + s, 0)),
        out_shape=jax.ShapeDtypeStruct((t, width), _BF16),
        scratch_shapes=[
            pltpu.VMEM((HGRN_HEADS, HEAD_DIM, HEAD_DIM), _F32),
            pltpu.VMEM((tc, 4 * width), _F32),
            pltpu.VMEM((HGRN_SLOTS, HGRN_LEVELS + 2, c, width), _BF16),
            pltpu.VMEM((HGRN_SLOTS, HGRN_LEVELS + 1, c, width), _BF16),
            pltpu.VMEM((HGRN_SLOTS, c, width), _BF16),
            pltpu.VMEM((HGRN_SLOTS, 1, width), _F32),
            pltpu.VMEM((HGRN_HEADS, c, c), _BF16),
        ],
        compiler_params=_params("parallel", "arbitrary"),
        name="hgrn_mixer",
    )(h, norm_gain.reshape(1, d), w_in, lb_logits, out_gain.reshape(1, HEAD_DIM),
      jnp.asarray(sums, _BF16), jnp.asarray(masks, _F32))


def _tail_kernel(a_ref, wo_hbm, h_ref, gain_ref, wi_hbm, wd_hbm, fgain_ref, o_ref,
                 u_ref, wo_ref, wi_ref, wd_ref, sem_ref, *, layer, final_norm, ff_tile):
    n_slabs, _, kw = a_ref.shape
    d_ff = wd_ref.shape[0]
    n_tiles = d_ff // ff_tile

    def weight_copies():
        copies = [pltpu.make_async_copy(wo_hbm, wo_ref, sem_ref.at[0])]
        for j in range(n_tiles):
            for half in range(2):
                cols = pl.ds(half * d_ff + j * ff_tile, ff_tile)
                copies.append(pltpu.make_async_copy(wi_hbm.at[layer, :, cols], wi_ref.at[:, cols],
                                                    sem_ref.at[len(copies)]))
            rows = pl.ds(j * ff_tile, ff_tile)
            copies.append(pltpu.make_async_copy(wd_hbm.at[layer, rows, :], wd_ref.at[rows, :],
                                                sem_ref.at[len(copies)]))
        return copies

    def body(copies):
        if copies:
            for cp in copies:
                cp.start()
            copies[0].wait()
        mixed = h_ref[...]
        for s in range(n_slabs):
            mixed += _dot(a_ref[s], wo_ref[s * kw:(s + 1) * kw, :])
        o_ref[...] = mixed
        u_ref[...] = (mixed * _rms_scale(mixed) * gain_ref[...]).astype(_BF16)
        for j in range(n_tiles):
            if copies:
                for cp in copies[1 + 3 * j:4 + 3 * j]:
                    cp.wait()
            u = u_ref[...]
            gate = _dot(u, wi_ref[:, j * ff_tile:(j + 1) * ff_tile])
            up = _dot(u, wi_ref[:, d_ff + j * ff_tile:d_ff + (j + 1) * ff_tile])
            act = (gate * _sigmoid(gate) * up).astype(_BF16)
            o_ref[...] += _dot(act, wd_ref[j * ff_tile:(j + 1) * ff_tile, :])
        if final_norm:
            y = o_ref[...]
            o_ref[...] = y * _rms_scale(y) * fgain_ref[...]

    first_step = pl.program_id(0) == 0

    @pl.when(first_step)
    def _():
        body(weight_copies())

    @pl.when(jnp.logical_not(first_step))
    def _():
        body(None)


def _block_tail(a, w_out, h, gain, w_in, w_down, final_gain, *, layer, final_norm, ff_tile=256):
    n_slabs, t, kw = a.shape
    d = h.shape[1]
    d_ff = w_down.shape[1]
    tm = ROW_TILE
    n_copies = 1 + 3 * (d_ff // ff_tile)
    in_hbm = pl.BlockSpec(memory_space=pl.ANY)
    return pl.pallas_call(
        functools.partial(_tail_kernel, layer=layer, final_norm=final_norm, ff_tile=ff_tile),
        grid=(t // tm,),
        in_specs=[
            pl.BlockSpec((n_slabs, tm, kw), lambda i: (0, i, 0)),
            in_hbm,
            pl.BlockSpec((tm, d), lambda i: (i, 0)),
            _resident((1, d)),
            in_hbm,
            in_hbm,
            _resident((1, d)),
        ],
        out_specs=pl.BlockSpec((tm, d), lambda i: (i, 0)),
        out_shape=jax.ShapeDtypeStruct((t, d), _F32),
        scratch_shapes=[
            pltpu.VMEM((tm, d), _BF16),
            pltpu.VMEM(w_out.shape, w_out.dtype),
            pltpu.VMEM(w_in.shape[1:], w_in.dtype),
            pltpu.VMEM(w_down.shape[1:], w_down.dtype),
            pltpu.SemaphoreType.DMA((n_copies,)),
        ],
        compiler_params=_params("arbitrary"),
        name="block_tail",
    )(a, w_out, h, gain.reshape(1, d), w_in, w_down, final_gain.reshape(1, d))


def _attn_kernel(*refs):
    ins = refs[:15]
    out_ref, o_scr, l_scr = refs[15:]
    first_block = pl.program_id(1) == 0
    row = lax.broadcasted_iota(jnp.int32, (SPAN, SPAN), 0)
    col = lax.broadcasted_iota(jnp.int32, (SPAN, SPAN), 1)
    bias_cur = jnp.where(col <= row, 0.0, NEG_BIG).astype(_F32)
    bias_prev = jnp.where(col >= row, 0.0, NEG_BIG).astype(_F32)
    bias_halo = bias_prev + jnp.where(first_block, NEG_BIG, 0.0).astype(_F32)
    ones = jnp.ones((SPAN, HEAD_DIM), _BF16)
    chain_len = 4
    body_units = 16

    for g, dil in enumerate(DILATIONS):
        q_ref, k_ref, v_ref, kh_ref, vh_ref = ins[5 * g:5 * g + 5]

        def load(ref, start, dil=dil):
            idx = pl.ds(start, SPAN) if dil == 1 else pl.ds(start, SPAN, stride=dil)
            return ref[idx, :].astype(_BF16)

        def load_kv(kref, vref, start, load=load):
            return load(kref, start), jnp.concatenate([load(vref, start), ones], axis=1)

        def chain(starts, prev, prev_bias, g=g, dil=dil, q_ref=q_ref, k_ref=k_ref, v_ref=v_ref,
                  load=load, load_kv=load_kv):
            for start in starts:
                kp, vp = prev
                kc, vc = cur = load_kv(k_ref, v_ref, start)
                s = _dot_nt(load(q_ref, start), jnp.concatenate([kp, kc], axis=0))
                s = s + jnp.concatenate([prev_bias, bias_cur], axis=1)
                m = jnp.max(jnp.maximum(s[:, :SPAN], s[:, SPAN:]), axis=-1, keepdims=True)
                p = jnp.exp2(s - m).astype(_BF16)
                r = _dot(p, jnp.concatenate([vp, vc], axis=0))
                denom = r[:, HEAD_DIM:]
                idx = pl.ds(start, SPAN) if dil == 1 else pl.ds(start, SPAN, stride=dil)
                o_scr[g, idx, :] = r[:, :HEAD_DIM] / denom
                l_scr[g, idx, :] = m + jnp.log2(denom)
                prev, prev_bias = cur, bias_prev

        block_rows = SPAN * dil
        if dil == DILATIONS[-1]:
            def body(i, carry, chain=chain, load_kv=load_kv, kh_ref=kh_ref, vh_ref=vh_ref):
                for u in range(body_units):
                    r = i * body_units + u
                    chain([r], load_kv(kh_ref, vh_ref, r), bias_halo)
                return carry
            lax.fori_loop(0, dil // body_units, body, 0)
        elif dil > 1:
            assert DILATIONS[-1] // dil == chain_len

            def body(i, carry, chain=chain, load_kv=load_kv, kh_ref=kh_ref, vh_ref=vh_ref,
                     block_rows=block_rows):
                for u in range(body_units // chain_len):
                    r = i * (body_units // chain_len) + u
                    chain([r + b * block_rows for b in range(chain_len)],
                          load_kv(kh_ref, vh_ref, r), bias_halo)
                return carry
            lax.fori_loop(0, dil * chain_len // body_units, body, 0)
        else:
            for first in range(0, DILATIONS[-1], chain_len):
                prev = (load_kv(kh_ref, vh_ref, 0) if first == 0
                        else load_kv(k_ref, v_ref, (first - 1) * block_rows))
                chain([(first + b) * block_rows for b in range(chain_len)], prev,
                      bias_halo if first == 0 else bias_prev)

    merge_rows = 256

    def merge(ci, carry):
        rr = pl.ds(pl.multiple_of(ci * merge_rows, merge_rows), merge_rows)
        l0, l1, l2 = l_scr[0, rr, :], l_scr[1, rr, :], l_scr[2, rr, :]
        m = jnp.maximum(jnp.maximum(l0, l1), l2)
        e0, e1, e2 = jnp.exp2(l0 - m), jnp.exp2(l1 - m), jnp.exp2(l2 - m)
        inv = 1.0 / (e0 + e1 + e2)
        out_ref[0, rr, :] = (o_scr[0, rr, :] * (e0 * inv)).astype(out_ref.dtype)
        out_ref[1, rr, :] = (o_scr[1, rr, :] * (e1 * inv)).astype(out_ref.dtype)
        out_ref[2, rr, :] = (o_scr[2, rr, :] * (e2 * inv)).astype(out_ref.dtype)
        return carry

    lax.fori_loop(0, ATTN_BLOCK // merge_rows, merge, 0)


def _attention(qkv, *, batch):
    t = qkv.shape[0]
    tb = ATTN_BLOCK
    steps = t // batch // tb
    in_specs, operands = [], []
    for g, dil in enumerate(DILATIONS):
        halo = SPAN * dil
        ratio = tb // halo

        def cur(which, g=g):
            return pl.BlockSpec(
                (tb, HEAD_DIM),
                lambda b, i, j: (b * steps + i, which * ATTN_HEADS + g * HEADS_PER_GROUP + j))

        def prev(which, g=g, ratio=ratio, halo=halo):
            return pl.BlockSpec(
                (halo, HEAD_DIM),
                lambda b, i, j: (jnp.maximum((b * steps + i) * ratio - 1, 0),
                                 which * ATTN_HEADS + g * HEADS_PER_GROUP + j))

        in_specs += [cur(0), cur(1), cur(2), prev(1), prev(2)]
        operands += [qkv] * 5
    n_groups = len(DILATIONS)
    return pl.pallas_call(
        _attn_kernel,
        grid=(batch, steps, HEADS_PER_GROUP),
        in_specs=in_specs,
        out_specs=pl.BlockSpec((n_groups, tb, HEAD_DIM), lambda b, i, j: (0, b * steps + i, j)),
        out_shape=jax.ShapeDtypeStruct((n_groups, t, HEADS_PER_GROUP * HEAD_DIM), _BF16),
        scratch_shapes=[pltpu.VMEM((3, tb, HEAD_DIM), _F32), pltpu.VMEM((3, tb, HEAD_DIM), _F32)],
        compiler_params=_params("parallel", "arbitrary", "arbitrary"),
        name="dilated_attention",
    )(*operands)


def _rope_tables(seq_len):
    inv_freq = 1.0 / (ROPE_THETA ** (np.arange(0, HEAD_DIM, 2, dtype=np.float64) / HEAD_DIM))
    ang = np.arange(seq_len, dtype=np.float64)[:, None] * inv_freq[None, :]
    cos, sin = np.cos(ang), np.sin(ang)
    return (jnp.asarray(np.concatenate([cos, cos], axis=-1), _F32),
            jnp.asarray(np.concatenate([-sin, sin], axis=-1), _F32))


def kernel(x, norm_mix, norm_ffn, hgrn_w_in, hgrn_lb_logits, hgrn_out_norm, hgrn_w_out,
           attn_w_qkv, attn_w_out, ffn_w_in, ffn_w_down, final_norm):
    batch, seq, d = x.shape
    cos2, sin2 = _rope_tables(seq)
    h = x.reshape(batch * seq, d)

    gated = _hgrn_mixer(h, norm_mix[0], hgrn_w_in[0], hgrn_lb_logits, hgrn_out_norm[0],
                        batch=batch, layer=0)
    ffn_in, ffn_down = ffn_w_in, ffn_w_down
    h = _block_tail(gated[None], hgrn_w_out[0], h, norm_ffn[0], ffn_in, ffn_down, final_norm,
                    layer=0, final_norm=False)

    qkv = _qkv_projection(h, norm_mix[1], attn_w_qkv[0], cos2, sin2,
                          scale=HEAD_DIM ** -0.5 * LOG2_E)
    attn = _attention(qkv, batch=batch)
    h = _block_tail(attn, attn_w_out[0], h, norm_ffn[1], ffn_in, ffn_down, final_norm,
                    layer=1, final_norm=True)
    return h.reshape(batch, seq, d)
```

```python
import functools

import numpy as np
import jax
import jax.numpy as jnp
from jax import lax
from jax.experimental import pallas as pl
from jax.experimental.pallas import tpu as pltpu

HEAD_DIM = 128
HGRN_HEADS = 8
HGRN_CHUNK = 64
HGRN_LEVELS = 6
HGRN_STEP_TOKENS = 512
HGRN_SLOTS = 2
ATTN_HEADS = 12
ATTN_WIDTH = ATTN_HEADS * HEAD_DIM
DILATIONS = (1, 4, 16)
SPAN = 128
HEADS_PER_GROUP = 4
ATTN_BLOCK = SPAN * DILATIONS[-1]
ROW_TILE = 512
COL_TILE = 512
ROPE_THETA = 10000.0
NORM_EPS = 1e-6
NEG_BIG = -1e30
LOG2_E = float(np.log2(np.e))
VMEM_LIMIT_BYTES = 56 * 1024 * 1024

_F32 = jnp.float32
_BF16 = jnp.bfloat16


def _dot(a, b):
    return lax.dot_general(a, b, (((1,), (0,)), ((), ())), preferred_element_type=_F32)


def _dot_nt(a, b):
    return lax.dot_general(a, b, (((1,), (1,)), ((), ())), preferred_element_type=_F32)


def _dot_tn(a, b):
    return lax.dot_general(a, b, (((0,), (0,)), ((), ())), preferred_element_type=_F32)


def _sigmoid(x):
    return 1.0 / (1.0 + jnp.exp2(x * (-LOG2_E)))


def _rms_scale(x):
    return lax.rsqrt(jnp.mean(x * x, axis=-1, keepdims=True) + NORM_EPS)


def _params(*sem):
    return pltpu.CompilerParams(dimension_semantics=sem, vmem_limit_bytes=VMEM_LIMIT_BYTES)


def _resident(shape):
    index = (0,) * len(shape)
    return pl.BlockSpec(shape, lambda *_: index, pipeline_mode=pl.Buffered(1))


def _qkv_kernel(h_ref, gain_ref, w_ref, cos_ref, sin_ref, o_ref, *, scale):
    x = h_ref[...]
    u = (x * _rms_scale(x) * gain_ref[...]).astype(_BF16)
    cos, sin = cos_ref[...], sin_ref[...]
    cos_q, sin_q = cos * scale, sin * scale
    for j in range(w_ref.shape[1] // COL_TILE):
        res = _dot(u, w_ref[:, j * COL_TILE:(j + 1) * COL_TILE])
        for hh in range(COL_TILE // HEAD_DIM):
            lo = j * COL_TILE + hh * HEAD_DIM
            xh = res[:, hh * HEAD_DIM:(hh + 1) * HEAD_DIM]
            if lo < ATTN_WIDTH:
                xh = xh * cos_q + pltpu.roll(xh, HEAD_DIM // 2, 1) * sin_q
            elif lo < 2 * ATTN_WIDTH:
                xh = xh * cos + pltpu.roll(xh, HEAD_DIM // 2, 1) * sin
            o_ref[:, lo:lo + HEAD_DIM] = xh


def _qkv_projection(h, gain, w, cos2, sin2, *, scale):
    t, d = h.shape
    n = w.shape[1]
    tm = ROW_TILE
    seq_tiles = cos2.shape[0] // tm
    return pl.pallas_call(
        functools.partial(_qkv_kernel, scale=scale),
        grid=(t // tm,),
        in_specs=[
            pl.BlockSpec((tm, d), lambda i: (i, 0)),
            _resident((1, d)),
            _resident((d, n)),
            pl.BlockSpec((tm, HEAD_DIM), lambda i: (i % seq_tiles, 0)),
            pl.BlockSpec((tm, HEAD_DIM), lambda i: (i % seq_tiles, 0)),
        ],
        out_specs=pl.BlockSpec((tm, n), lambda i: (i, 0)),
        out_shape=jax.ShapeDtypeStruct((t, n), _F32),
        compiler_params=_params("parallel"),
        name="qkv_projection",
    )(h, gain.reshape(1, d), w, cos2, sin2)


def _hgrn_tables():
    c = HGRN_CHUNK
    assert 1 << HGRN_LEVELS == c
    t = np.arange(c)
    col = t[None, :]
    row = t[:, None]
    sums = np.zeros((HGRN_LEVELS + 1, c, c), np.float32)
    sums[0] = col <= row
    sums[1] = col > row
    masks = np.zeros((HGRN_LEVELS + 1, c, c), np.float32)
    for level in range(HGRN_LEVELS):
        half = c >> (level + 1)
        block = t // (2 * half)
        mid = block * 2 * half + half
        is_query = t >= mid
        if level < HGRN_LEVELS - 1:
            q_rows = (col >= mid[:, None]) & (col <= row) & is_query[:, None]
            k_rows = (col > row) & (col < mid[:, None]) & (~is_query)[:, None]
            sums[2 + level] = q_rows | k_rows
        masks[level] = ((block[:, None] == block[None, :]) & is_query[:, None]
                        & (~is_query)[None, :])
    masks[HGRN_LEVELS] = np.eye(c)
    assert np.array_equal(masks.sum(0), np.tril(np.ones((c, c))))
    sums = sums.reshape((HGRN_LEVELS + 1) * c, c)
    return np.concatenate([sums, sums], axis=1), masks


def _hgrn_kernel(h_ref, ngain_ref, w_ref, lbl_ref, gain_ref, sums_ref, masks_ref, o_ref,
                 state_ref, proj_ref, a2_ref, b2_ref, v2_ref, decay_ref, s_ref, *, layer):
    c = HGRN_CHUNK
    width = HGRN_HEADS * HEAD_DIM

    @pl.when(pl.program_id(1) == 0)
    def _():
        state_ref[...] = jnp.zeros_like(state_ref)

    half_rows = h_ref.shape[0] // 2
    for half in range(2):
        rows = slice(half * half_rows, (half + 1) * half_rows)
        x = h_ref[rows, :]
        u = (x * _rms_scale(x) * ngain_ref[...]).astype(_BF16)
        for j in range(w_ref.shape[1] // COL_TILE):
            cols = slice(j * COL_TILE, (j + 1) * COL_TILE)
            proj_ref[rows, cols] = _dot(u, w_ref[:, cols])

    logits = lbl_ref[...]
    e = jnp.exp(logits - jnp.max(logits, axis=0, keepdims=True))
    lb = jnp.sum(e[:layer + 1], axis=0, keepdims=True) / jnp.sum(e, axis=0, keepdims=True)
    out_gain = gain_ref[...]

    n_chunks = h_ref.shape[0] // c

    def chunk_rows(ci):
        return pl.ds(pl.multiple_of(ci * c, c), c)

    def prepare(ci, slot):
        a_ref, b_ref, v_ref = a2_ref.at[slot], b2_ref.at[slot], v2_ref.at[slot]
        rows = chunk_rows(ci)
        q = proj_ref[rows, 0:width]
        forget = lb + (1.0 - lb) * _sigmoid(proj_ref[rows, width:2 * width])
        glog = jnp.log(forget) * LOG2_E
        kk = 1.0 - forget
        qq = q * _sigmoid(q)
        g_hi = glog.astype(_BF16)
        g_lo = (glog - g_hi.astype(_F32)).astype(_BF16)
        factors = jnp.exp2(_dot(sums_ref[...], jnp.concatenate([g_hi, g_lo], axis=0)))

        qq = qq.astype(_BF16)
        kk = kk.astype(_BF16)
        from_start = factors[0:c]
        a_ref[0] = qq * from_start.astype(_BF16)
        decay_ref[slot] = from_start[c - 1:c, :]
        b_ref[0] = kk * factors[c:2 * c].astype(_BF16)
        for level in range(HGRN_LEVELS - 1):
            fac = factors[(2 + level) * c:(3 + level) * c].astype(_BF16)
            a_ref[1 + level] = qq * fac
            b_ref[1 + level] = kk * fac
        a_ref[HGRN_LEVELS] = qq * forget.astype(_BF16)
        a_ref[HGRN_LEVELS + 1] = qq
        b_ref[HGRN_LEVELS] = kk
        v_ref[...] = proj_ref[rows, 2 * width:3 * width].astype(_BF16)

    def lanes(h):
        return slice(h * HEAD_DIM, (h + 1) * HEAD_DIM)

    def consume(ci, slot):
        a_ref, b_ref, v_ref = a2_ref.at[slot], b2_ref.at[slot], v2_ref.at[slot]
        rows = chunk_rows(ci)
        chunk_decay = decay_ref[slot]
        pairs = [(h, h + 1) for h in range(0, HGRN_HEADS, 2)]
        for pair in pairs:
            scores = [jnp.zeros((c, c), _F32) for _ in pair]
            for level in range(HGRN_LEVELS - 1):
                for i, h in enumerate(pair):
                    scores[i] += masks_ref[level] * _dot_nt(a_ref[1 + level, :, lanes(h)],
                                                            b_ref[1 + level, :, lanes(h)])
            for i, h in enumerate(pair):
                both = a_ref[HGRN_LEVELS:HGRN_LEVELS + 2, :, lanes(h)].reshape(2 * c, HEAD_DIM)
                prod = _dot_nt(both, b_ref[HGRN_LEVELS, :, lanes(h)])
                scores[i] += (masks_ref[HGRN_LEVELS - 1] * prod[:c]
                              + masks_ref[HGRN_LEVELS] * prod[c:])
                s_ref[h] = scores[i].astype(_BF16)
        for pair in pairs:
            states = [state_ref[h] for h in pair]
            intra = [_dot(s_ref[h], v_ref[:, lanes(h)]) for h in pair]
            inter = [_dot(a_ref[0, :, lanes(h)], states[i].astype(_BF16))
                     for i, h in enumerate(pair)]
            update = [_dot_tn(b_ref[0, :, lanes(h)], v_ref[:, lanes(h)]) for h in pair]
            for i, h in enumerate(pair):
                decay_col = jnp.transpose(
                    jnp.broadcast_to(chunk_decay[:, lanes(h)], (8, HEAD_DIM)))[:, 0:1]
                state_ref[h] = states[i] * decay_col + update[i]
                o = intra[i] + inter[i]
                o = o * _rms_scale(o) * out_gain
                gate = proj_ref[rows, 3 * width + h * HEAD_DIM:3 * width + (h + 1) * HEAD_DIM]
                o_ref[rows, lanes(h)] = (o * (gate * _sigmoid(gate))).astype(o_ref.dtype)

    n_slots = a2_ref.shape[0]
    ahead = n_slots - 1
    for ci in range(min(ahead, n_chunks)):
        prepare(ci, ci % n_slots)
    for ci in range(n_chunks):
        if ci + ahead < n_chunks:
            prepare(ci + ahead, (ci + ahead) % n_slots)
        consume(ci, ci % n_slots)


def _hgrn_mixer(h, norm_gain, w_in, lb_logits, out_gain, *, batch, layer):
    t, d = h.shape
    width = HGRN_HEADS * HEAD_DIM
    tc = HGRN_STEP_TOKENS
    steps = t // batch // tc
    sums, masks = _hgrn_tables()
    c = HGRN_CHUNK
    return pl.pallas_call(
        functools.partial(_hgrn_kernel, layer=layer),
        grid=(batch, steps),
        in_specs=[
            pl.BlockSpec((tc, d), lambda b, s: (b * steps + s, 0)),
            _resident((1, d)),
            _resident(w_in.shape),
            _resident(lb_logits.shape),
            _resident((1, HEAD_DIM)),
            _resident(sums.shape),
            _resident(masks.shape),
        ],
        out_specs=pl.BlockSpec((tc, width), lambda b, s: (b * steps + s, 0)),
        out_shape=jax.ShapeDtypeStruct((t, width), _BF16),
        scratch_shapes=[
            pltpu.VMEM((HGRN_HEADS, HEAD_DIM, HEAD_DIM), _F32),
            pltpu.VMEM((tc, 4 * width), _F32),
            pltpu.VMEM((HGRN_SLOTS, HGRN_LEVELS + 2, c, width), _BF16),
            pltpu.VMEM((HGRN_SLOTS, HGRN_LEVELS + 1, c, width), _BF16),
            pltpu.VMEM((HGRN_SLOTS, c, width), _BF16),
            pltpu.VMEM((HGRN_SLOTS, 1, width), _F32),
            pltpu.VMEM((HGRN_HEADS, c, c), _BF16),
        ],
        compiler_params=_params("parallel", "arbitrary"),
        name="hgrn_mixer",
    )(h, norm_gain.reshape(1, d), w_in, lb_logits, out_gain.reshape(1, HEAD_DIM),
      jnp.asarray(sums, _BF16), jnp.asarray(masks, _F32))


def _tail_kernel(a_ref, wo_hbm, h_ref, gain_ref, wi_hbm, wd_hbm, fgain_ref, o_ref,
                 u_ref, wo_ref, wi_ref, wd_ref, sem_ref, *, layer, final_norm, ff_tile):
    n_slabs, _, kw = a_ref.shape
    d_ff = wd_ref.shape[0]
    n_tiles = d_ff // ff_tile

    def weight_copies():
        copies = [pltpu.make_async_copy(wo_hbm, wo_ref, sem_ref.at[0])]
        for j in range(n_tiles):
            for half in range(2):
                cols = pl.ds(half * d_ff + j * ff_tile, ff_tile)
                copies.append(pltpu.make_async_copy(wi_hbm.at[layer, :, cols], wi_ref.at[:, cols],
                                                    sem_ref.at[len(copies)]))
            rows = pl.ds(j * ff_tile, ff_tile)
            copies.append(pltpu.make_async_copy(wd_hbm.at[layer, rows, :], wd_ref.at[rows, :],
                                                sem_ref.at[len(copies)]))
        return copies

    def body(copies):
        if copies:
            for cp in copies:
                cp.start()
            copies[0].wait()
        mixed = h_ref[...]
        for s in range(n_slabs):
            mixed += _dot(a_ref[s], wo_ref[s * kw:(s + 1) * kw, :])
        o_ref[...] = mixed
        u_ref[...] = (mixed * _rms_scale(mixed) * gain_ref[...]).astype(_BF16)
        for j in range(n_tiles):
            if copies:
                for cp in copies[1 + 3 * j:4 + 3 * j]:
                    cp.wait()
            u = u_ref[...]
            gate = _dot(u, wi_ref[:, j * ff_tile:(j + 1) * ff_tile])
            up = _dot(u, wi_ref[:, d_ff + j * ff_tile:d_ff + (j + 1) * ff_tile])
            act = (gate * _sigmoid(gate) * up).astype(_BF16)
            o_ref[...] += _dot(act, wd_ref[j * ff_tile:(j + 1) * ff_tile, :])
        if final_norm:
            y = o_ref[...]
            o_ref[...] = y * _rms_scale(y) * fgain_ref[...]

    first_step = pl.program_id(0) == 0

    @pl.when(first_step)
    def _():
        body(weight_copies())

    @pl.when(jnp.logical_not(first_step))
    def _():
        body(None)


def _block_tail(a, w_out, h, gain, w_in, w_down, final_gain, *, layer, final_norm, ff_tile=256):
    n_slabs, t, kw = a.shape
    d = h.shape[1]
    d_ff = w_down.shape[1]
    tm = ROW_TILE
    n_copies = 1 + 3 * (d_ff // ff_tile)
    in_hbm = pl.BlockSpec(memory_space=pl.ANY)
    return pl.pallas_call(
        functools.partial(_tail_kernel, layer=layer, final_norm=final_norm, ff_tile=ff_tile),
        grid=(t // tm,),
        in_specs=[
            pl.BlockSpec((n_slabs, tm, kw), lambda i: (0, i, 0)),
            in_hbm,
            pl.BlockSpec((tm, d), lambda i: (i, 0)),
            _resident((1, d)),
            in_hbm,
            in_hbm,
            _resident((1, d)),
        ],
        out_specs=pl.BlockSpec((tm, d), lambda i: (i, 0)),
        out_shape=jax.ShapeDtypeStruct((t, d), _F32),
        scratch_shapes=[
            pltpu.VMEM((tm, d), _BF16),
            pltpu.VMEM(w_out.shape, w_out.dtype),
            pltpu.VMEM(w_in.shape[1:], w_in.dtype),
            pltpu.VMEM(w_down.shape[1:], w_down.dtype),
            pltpu.SemaphoreType.DMA((n_copies,)),
        ],
        compiler_params=_params("arbitrary"),
        name="block_tail",
    )(a, w_out, h, gain.reshape(1, d), w_in, w_down, final_gain.reshape(1, d))


def _attn_kernel(*refs):
    ins = refs[:15]
    out_ref, o_scr, l_scr = refs[15:]
    first_block = pl.program_id(1) == 0
    row = lax.broadcasted_iota(jnp.int32, (SPAN, SPAN), 0)
    col = lax.broadcasted_iota(jnp.int32, (SPAN, SPAN), 1)
    bias_cur = jnp.where(col <= row, 0.0, NEG_BIG).astype(_F32)
    bias_prev = jnp.where(col >= row, 0.0, NEG_BIG).astype(_F32)
    bias_halo = bias_prev + jnp.where(first_block, NEG_BIG, 0.0).astype(_F32)
    ones = jnp.ones((SPAN, HEAD_DIM), _BF16)
    chain_len = 4
    body_units = 16

    for g, dil in enumerate(DILATIONS):
        q_ref, k_ref, v_ref, kh_ref, vh_ref = ins[5 * g:5 * g + 5]

        def load(ref, start, dil=dil):
            idx = pl.ds(start, SPAN) if dil == 1 else pl.ds(start, SPAN, stride=dil)
            return ref[idx, :].astype(_BF16)

        def load_kv(kref, vref, start, load=load):
            return load(kref, start), jnp.concatenate([load(vref, start), ones], axis=1)

        def chain(starts, prev, prev_bias, g=g, dil=dil, q_ref=q_ref, k_ref=k_ref, v_ref=v_ref,
                  load=load, load_kv=load_kv):
            for start in starts:
                kp, vp = prev
                kc, vc = cur = load_kv(k_ref, v_ref, start)
                s = _dot_nt(load(q_ref, start), jnp.concatenate([kp, kc], axis=0))
                s = s + jnp.concatenate([prev_bias, bias_cur], axis=1)
                m = jnp.max(jnp.maximum(s[:, :SPAN], s[:, SPAN:]), axis=-1, keepdims=True)
                p = jnp.exp2(s - m).astype(_BF16)
                r = _dot(p, jnp.concatenate([vp, vc], axis=0))
                denom = r[:, HEAD_DIM:]
                idx = pl.ds(start, SPAN) if dil == 1 else pl.ds(start, SPAN, stride=dil)
                o_scr[g, idx, :] = r[:, :HEAD_DIM] / denom
                l_scr[g, idx, :] = m + jnp.log2(denom)
                prev, prev_bias = cur, bias_prev

        block_rows = SPAN * dil
        if dil == DILATIONS[-1]:
            def body(i, carry, chain=chain, load_kv=load_kv, kh_ref=kh_ref, vh_ref=vh_ref):
                for u in range(body_units):
                    r = i * body_units + u
                    chain([r], load_kv(kh_ref, vh_ref, r), bias_halo)
                return carry
            lax.fori_loop(0, dil // body_units, body, 0)
        elif dil > 1:
            assert DILATIONS[-1] // dil == chain_len

            def body(i, carry, chain=chain, load_kv=load_kv, kh_ref=kh_ref, vh_ref=vh_ref,
                     block_rows=block_rows):
                for u in range(body_units // chain_len):
                    r = i * (body_units // chain_len) + u
                    chain([r + b * block_rows for b in range(chain_len)],
                          load_kv(kh_ref, vh_ref, r), bias_halo)
                return carry
            lax.fori_loop(0, dil * chain_len // body_units, body, 0)
        else:
            for first in range(0, DILATIONS[-1], chain_len):
                prev = (load_kv(kh_ref, vh_ref, 0) if first == 0
                        else load_kv(k_ref, v_ref, (first - 1) * block_rows))
                chain([(first + b) * block_rows for b in range(chain_len)], prev,
                      bias_halo if first == 0 else bias_prev)

    merge_rows = 256

    def merge(ci, carry):
        rr = pl.ds(pl.multiple_of(ci * merge_rows, merge_rows), merge_rows)
        l0, l1, l2 = l_scr[0, rr, :], l_scr[1, rr, :], l_scr[2, rr, :]
        m = jnp.maximum(jnp.maximum(l0, l1), l2)
        e0, e1, e2 = jnp.exp2(l0 - m), jnp.exp2(l1 - m), jnp.exp2(l2 - m)
        inv = 1.0 / (e0 + e1 + e2)
        out_ref[0, rr, :] = (o_scr[0, rr, :] * (e0 * inv)).astype(out_ref.dtype)
        out_ref[1, rr, :] = (o_scr[1, rr, :] * (e1 * inv)).astype(out_ref.dtype)
        out_ref[2, rr, :] = (o_scr[2, rr, :] * (e2 * inv)).astype(out_ref.dtype)
        return carry

    lax.fori_loop(0, ATTN_BLOCK // merge_rows, merge, 0)


def _attention(qkv, *, batch):
    t = qkv.shape[0]
    tb = ATTN_BLOCK
    steps = t // batch // tb
    in_specs, operands = [], []
    for g, dil in enumerate(DILATIONS):
        halo = SPAN * dil
        ratio = tb // halo

        def cur(which, g=g):
            return pl.BlockSpec(
                (tb, HEAD_DIM),
                lambda b, i, j: (b * steps + i, which * ATTN_HEADS + g * HEADS_PER_GROUP + j))

        def prev(which, g=g, ratio=ratio, halo=halo):
            return pl.BlockSpec(
                (halo, HEAD_DIM),
                lambda b, i, j: (jnp.maximum((b * steps + i) * ratio - 1, 0),
                                 which * ATTN_HEADS + g * HEADS_PER_GROUP + j))

        in_specs += [cur(0), cur(1), cur(2), prev(1), prev(2)]
        operands += [qkv] * 5
    n_groups = len(DILATIONS)
    return pl.pallas_call(
        _attn_kernel,
        grid=(batch, steps, HEADS_PER_GROUP),
        in_specs=in_specs,
        out_specs=pl.BlockSpec((n_groups, tb, HEAD_DIM), lambda b, i, j: (0, b * steps + i, j)),
        out_shape=jax.ShapeDtypeStruct((n_groups, t, HEADS_PER_GROUP * HEAD_DIM), _BF16),
        scratch_shapes=[pltpu.VMEM((3, tb, HEAD_DIM), _F32), pltpu.VMEM((3, tb, HEAD_DIM), _F32)],
        compiler_params=_params("parallel", "arbitrary", "arbitrary"),
        name="dilated_attention",
    )(*operands)


def _rope_tables(seq_len):
    inv_freq = 1.0 / (ROPE_THETA ** (np.arange(0, HEAD_DIM, 2, dtype=np.float64) / HEAD_DIM))
    ang = np.arange(seq_len, dtype=np.float64)[:, None] * inv_freq[None, :]
    cos, sin = np.cos(ang), np.sin(ang)
    return (jnp.asarray(np.concatenate([cos, cos], axis=-1), _F32),
            jnp.asarray(np.concatenate([-sin, sin], axis=-1), _F32))


def kernel(x, norm_mix, norm_ffn, hgrn_w_in, hgrn_lb_logits, hgrn_out_norm, hgrn_w_out,
           attn_w_qkv, attn_w_out, ffn_w_in, ffn_w_down, final_norm):
    batch, seq, d = x.shape
    cos2, sin2 = _rope_tables(seq)
    h = x.reshape(batch * seq, d)

    gated = _hgrn_mixer(h, norm_mix[0], hgrn_w_in[0], hgrn_lb_logits, hgrn_out_norm[0],
                        batch=batch, layer=0)
    ffn_in, ffn_down = ffn_w_in, ffn_w_down
    h = _block_tail(gated[None], hgrn_w_out[0], h, norm_ffn[0], ffn_in, ffn_down, final_norm,
                    layer=0, final_norm=False)

    qkv = _qkv_projection(h, norm_mix[1], attn_w_qkv[0], cos2, sin2,
                          scale=HEAD_DIM ** -0.5 * LOG2_E)
    attn = _attention(qkv, batch=batch)
    h = _block_tail(attn, attn_w_out[0], h, norm_ffn[1], ffn_in, ffn_down, final_norm,
                    layer=1, final_norm=True)
    return h.reshape(batch, seq, d)
```

```python
import functools

import numpy as np
import jax
import jax.numpy as jnp
from jax import lax
from jax.experimental import pallas as pl
from jax.experimental.pallas import tpu as pltpu

HEAD_DIM = 128
HGRN_HEADS = 8
HGRN_CHUNK = 64
HGRN_LEVELS = 6
HGRN_STEP_TOKENS = 512
HGRN_SLOTS = 2
ATTN_HEADS = 12
ATTN_WIDTH = ATTN_HEADS * HEAD_DIM
DILATIONS = (1, 4, 16)
SPAN = 128
HEADS_PER_GROUP = 4
ATTN_BLOCK = SPAN * DILATIONS[-1]
ROW_TILE = 512
COL_TILE = 512
ROPE_THETA = 10000.0
NORM_EPS = 1e-6
NEG_BIG = -1e30
LOG2_E = float(np.log2(np.e))
VMEM_LIMIT_BYTES = 56 * 1024 * 1024

_F32 = jnp.float32
_BF16 = jnp.bfloat16


def _dot(a, b):
    return lax.dot_general(a, b, (((1,), (0,)), ((), ())), preferred_element_type=_F32)


def _dot_nt(a, b):
    return lax.dot_general(a, b, (((1,), (1,)), ((), ())), preferred_element_type=_F32)


def _dot_tn(a, b):
    return lax.dot_general(a, b, (((0,), (0,)), ((), ())), preferred_element_type=_F32)


def _sigmoid(x):
    return 1.0 / (1.0 + jnp.exp2(x * (-LOG2_E)))


def _rms_scale(x):
    return lax.rsqrt(jnp.mean(x * x, axis=-1, keepdims=True) + NORM_EPS)


def _params(*sem):
    return pltpu.CompilerParams(dimension_semantics=sem, vmem_limit_bytes=VMEM_LIMIT_BYTES)


def _resident(shape):
    index = (0,) * len(shape)
    return pl.BlockSpec(shape, lambda *_: index, pipeline_mode=pl.Buffered(1))


def _qkv_kernel(h_ref, gain_ref, w_ref, cos_ref, sin_ref, o_ref, *, scale):
    x = h_ref[...]
    u = (x * _rms_scale(x) * gain_ref[...]).astype(_BF16)
    cos, sin = cos_ref[...], sin_ref[...]
    cos_q, sin_q = cos * scale, sin * scale
    for j in range(w_ref.shape[1] // COL_TILE):
        res = _dot(u, w_ref[:, j * COL_TILE:(j + 1) * COL_TILE])
        for hh in range(COL_TILE // HEAD_DIM):
            lo = j * COL_TILE + hh * HEAD_DIM
            xh = res[:, hh * HEAD_DIM:(hh + 1) * HEAD_DIM]
            if lo < ATTN_WIDTH:
                xh = xh * cos_q + pltpu.roll(xh, HEAD_DIM // 2, 1) * sin_q
            elif lo < 2 * ATTN_WIDTH:
                xh = xh * cos + pltpu.roll(xh, HEAD_DIM // 2, 1) * sin
            o_ref[:, lo:lo + HEAD_DIM] = xh


def _qkv_projection(h, gain, w, cos2, sin2, *, scale):
    t, d = h.shape
    n = w.shape[1]
    tm = ROW_TILE
    seq_tiles = cos2.shape[0] // tm
    return pl.pallas_call(
        functools.partial(_qkv_kernel, scale=scale),
        grid=(t // tm,),
        in_specs=[
            pl.BlockSpec((tm, d), lambda i: (i, 0)),
            _resident((1, d)),
            _resident((d, n)),
            pl.BlockSpec((tm, HEAD_DIM), lambda i: (i % seq_tiles, 0)),
            pl.BlockSpec((tm, HEAD_DIM), lambda i: (i % seq_tiles, 0)),
        ],
        out_specs=pl.BlockSpec((tm, n), lambda i: (i, 0)),
        out_shape=jax.ShapeDtypeStruct((t, n), _F32),
        compiler_params=_params("parallel"),
        name="qkv_projection",
    )(h, gain.reshape(1, d), w, cos2, sin2)


def _hgrn_tables():
    c = HGRN_CHUNK
    assert 1 << HGRN_LEVELS == c
    t = np.arange(c)
    col = t[None, :]
    row = t[:, None]
    sums = np.zeros((HGRN_LEVELS + 1, c, c), np.float32)
    sums[0] = col <= row
    sums[1] = col > row
    masks = np.zeros((HGRN_LEVELS + 1, c, c), np.float32)
    for level in range(HGRN_LEVELS):
        half = c >> (level + 1)
        block = t // (2 * half)
        mid = block * 2 * half + half
        is_query = t >= mid
        if level < HGRN_LEVELS - 1:
            q_rows = (col >= mid[:, None]) & (col <= row) & is_query[:, None]
            k_rows = (col > row) & (col < mid[:, None]) & (~is_query)[:, None]
            sums[2 + level] = q_rows | k_rows
        masks[level] = ((block[:, None] == block[None, :]) & is_query[:, None]
                        & (~is_query)[None, :])
    masks[HGRN_LEVELS] = np.eye(c)
    assert np.array_equal(masks.sum(0), np.tril(np.ones((c, c))))
    sums = sums.reshape((HGRN_LEVELS + 1) * c, c)
    return np.concatenate([sums, sums], axis=1), masks


def _hgrn_kernel(h_ref, ngain_ref, w_ref, lbl_ref, gain_ref, sums_ref, masks_ref, o_ref,
                 state_ref, proj_ref, a2_ref, b2_ref, v2_ref, decay_ref, s_ref, *, layer):
    c = HGRN_CHUNK
    width = HGRN_HEADS * HEAD_DIM

    @pl.when(pl.program_id(1) == 0)
    def _():
        state_ref[...] = jnp.zeros_like(state_ref)

    x = h_ref[...]
    u = (x * _rms_scale(x) * ngain_ref[...]).astype(_BF16)
    half_rows = x.shape[0] // 2
    for half in range(2):
        rows = slice(half * half_rows, (half + 1) * half_rows)
        for j in range(w_ref.shape[1] // COL_TILE):
            cols = slice(j * COL_TILE, (j + 1) * COL_TILE)
            proj_ref[rows, cols] = _dot(u[rows], w_ref[:, cols])

    logits = lbl_ref[...]
    e = jnp.exp(logits - jnp.max(logits, axis=0, keepdims=True))
    lb = jnp.sum(e[:layer + 1], axis=0, keepdims=True) / jnp.sum(e, axis=0, keepdims=True)
    out_gain = gain_ref[...]

    n_chunks = h_ref.shape[0] // c

    def chunk_rows(ci):
        return pl.ds(pl.multiple_of(ci * c, c), c)

    def prepare(ci, slot):
        a_ref, b_ref, v_ref = a2_ref.at[slot], b2_ref.at[slot], v2_ref.at[slot]
        rows = chunk_rows(ci)
        q = proj_ref[rows, 0:width]
        forget = lb + (1.0 - lb) * _sigmoid(proj_ref[rows, width:2 * width])
        glog = jnp.log(forget) * LOG2_E
        kk = 1.0 - forget
        qq = q * _sigmoid(q)
        g_hi = glog.astype(_BF16)
        g_lo = (glog - g_hi.astype(_F32)).astype(_BF16)
        factors = jnp.exp2(_dot(sums_ref[...], jnp.concatenate([g_hi, g_lo], axis=0)))

        qq = qq.astype(_BF16)
        kk = kk.astype(_BF16)
        from_start = factors[0:c]
        a_ref[0] = qq * from_start.astype(_BF16)
        decay_ref[slot] = from_start[c - 1:c, :]
        b_ref[0] = kk * factors[c:2 * c].astype(_BF16)
        for level in range(HGRN_LEVELS - 1):
            fac = factors[(2 + level) * c:(3 + level) * c].astype(_BF16)
            a_ref[1 + level] = qq * fac
            b_ref[1 + level] = kk * fac
        a_ref[HGRN_LEVELS] = qq * forget.astype(_BF16)
        a_ref[HGRN_LEVELS + 1] = qq
        b_ref[HGRN_LEVELS] = kk
        v_ref[...] = proj_ref[rows, 2 * width:3 * width].astype(_BF16)

    def lanes(h):
        return slice(h * HEAD_DIM, (h + 1) * HEAD_DIM)

    def consume(ci, slot):
        a_ref, b_ref, v_ref = a2_ref.at[slot], b2_ref.at[slot], v2_ref.at[slot]
        rows = chunk_rows(ci)
        chunk_decay = decay_ref[slot]
        pairs = [(h, h + 1) for h in range(0, HGRN_HEADS, 2)]
        for pair in pairs:
            scores = [jnp.zeros((c, c), _F32) for _ in pair]
            for level in range(HGRN_LEVELS - 1):
                for i, h in enumerate(pair):
                    scores[i] += masks_ref[level] * _dot_nt(a_ref[1 + level, :, lanes(h)],
                                                            b_ref[1 + level, :, lanes(h)])
            for i, h in enumerate(pair):
                both = a_ref[HGRN_LEVELS:HGRN_LEVELS + 2, :, lanes(h)].reshape(2 * c, HEAD_DIM)
                prod = _dot_nt(both, b_ref[HGRN_LEVELS, :, lanes(h)])
                scores[i] += (masks_ref[HGRN_LEVELS - 1] * prod[:c]
                              + masks_ref[HGRN_LEVELS] * prod[c:])
                s_ref[h] = scores[i].astype(_BF16)
        for pair in pairs:
            states = [state_ref[h] for h in pair]
            intra = [_dot(s_ref[h], v_ref[:, lanes(h)]) for h in pair]
            inter = [_dot(a_ref[0, :, lanes(h)], states[i].astype(_BF16))
                     for i, h in enumerate(pair)]
            update = [_dot_tn(b_ref[0, :, lanes(h)], v_ref[:, lanes(h)]) for h in pair]
            for i, h in enumerate(pair):
                decay_col = jnp.transpose(
                    jnp.broadcast_to(chunk_decay[:, lanes(h)], (8, HEAD_DIM)))[:, 0:1]
                state_ref[h] = states[i] * decay_col + update[i]
                o = intra[i] + inter[i]
                o = o * _rms_scale(o) * out_gain
                gate = proj_ref[rows, 3 * width + h * HEAD_DIM:3 * width + (h + 1) * HEAD_DIM]
                o_ref[rows, lanes(h)] = (o * (gate * _sigmoid(gate))).astype(o_ref.dtype)

    n_slots = a2_ref.shape[0]
    ahead = n_slots - 1
    for ci in range(min(ahead, n_chunks)):
        prepare(ci, ci % n_slots)
    for ci in range(n_chunks):
        if ci + ahead < n_chunks:
            prepare(ci + ahead, (ci + ahead) % n_slots)
        consume(ci, ci % n_slots)


def _hgrn_mixer(h, norm_gain, w_in, lb_logits, out_gain, *, batch, layer):
    t, d = h.shape
    width = HGRN_HEADS * HEAD_DIM
    tc = HGRN_STEP_TOKENS
    steps = t // batch // tc
    sums, masks = _hgrn_tables()
    c = HGRN_CHUNK
    return pl.pallas_call(
        functools.partial(_hgrn_kernel, layer=layer),
        grid=(batch, steps),
        in_specs=[
            pl.BlockSpec((tc, d), lambda b, s: (b * steps + s, 0)),
            _resident((1, d)),
            _resident(w_in.shape),
            _resident(lb_logits.shape),
            _resident((1, HEAD_DIM)),
            _resident(sums.shape),
            _resident(masks.shape),
        ],
        out_specs=pl.BlockSpec((tc, width), lambda b, s: (b * steps + s, 0)),
        out_shape=jax.ShapeDtypeStruct((t, width), _BF16),
        scratch_shapes=[
            pltpu.VMEM((HGRN_HEADS, HEAD_DIM, HEAD_DIM), _F32),
            pltpu.VMEM((tc, 4 * width), _F32),
            pltpu.VMEM((HGRN_SLOTS, HGRN_LEVELS + 2, c, width), _BF16),
            pltpu.VMEM((HGRN_SLOTS, HGRN_LEVELS + 1, c, width), _BF16),
            pltpu.VMEM((HGRN_SLOTS, c, width), _BF16),
            pltpu.VMEM((HGRN_SLOTS, 1, width), _F32),
            pltpu.VMEM((HGRN_HEADS, c, c), _BF16),
        ],
        compiler_params=_params("parallel", "arbitrary"),
        name="hgrn_mixer",
    )(h, norm_gain.reshape(1, d), w_in, lb_logits, out_gain.reshape(1, HEAD_DIM),
      jnp.asarray(sums, _BF16), jnp.asarray(masks, _F32))


def _tail_kernel(a_ref, wo_hbm, h_ref, gain_ref, wi_hbm, wd_hbm, fgain_ref, o_ref,
                 u_ref, wo_ref, wi_ref, wd_ref, sem_ref, *, layer, final_norm, ff_tile):
    n_slabs, _, kw = a_ref.shape
    d_ff = wd_ref.shape[0]
    n_tiles = d_ff // ff_tile

    def weight_copies():
        copies = [pltpu.make_async_copy(wo_hbm, wo_ref, sem_ref.at[0])]
        for j in range(n_tiles):
            for half in range(2):
                cols = pl.ds(half * d_ff + j * ff_tile, ff_tile)
                copies.append(pltpu.make_async_copy(wi_hbm.at[layer, :, cols], wi_ref.at[:, cols],
                                                    sem_ref.at[len(copies)]))
            rows = pl.ds(j * ff_tile, ff_tile)
            copies.append(pltpu.make_async_copy(wd_hbm.at[layer, rows, :], wd_ref.at[rows, :],
                                                sem_ref.at[len(copies)]))
        return copies

    def body(copies):
        if copies:
            for n, cp in enumerate(copies):
                cp.start(priority=n % 2)
            copies[0].wait()
        mixed = h_ref[...]
        for s in range(n_slabs):
            mixed += _dot(a_ref[s], wo_ref[s * kw:(s + 1) * kw, :])
        o_ref[...] = mixed
        u_ref[...] = (mixed * _rms_scale(mixed) * gain_ref[...]).astype(_BF16)
        for j in range(n_tiles):
            if copies:
                for cp in copies[1 + 3 * j:4 + 3 * j]:
                    cp.wait()
            u = u_ref[...]
            gate = _dot(u, wi_ref[:, j * ff_tile:(j + 1) * ff_tile])
            up = _dot(u, wi_ref[:, d_ff + j * ff_tile:d_ff + (j + 1) * ff_tile])
            act = (gate * _sigmoid(gate) * up).astype(_BF16)
            o_ref[...] += _dot(act, wd_ref[j * ff_tile:(j + 1) * ff_tile, :])
        if final_norm:
            y = o_ref[...]
            o_ref[...] = y * _rms_scale(y) * fgain_ref[...]

    first_step = pl.program_id(0) == 0

    @pl.when(first_step)
    def _():
        body(weight_copies())

    @pl.when(jnp.logical_not(first_step))
    def _():
        body(None)


def _block_tail(a, w_out, h, gain, w_in, w_down, final_gain, *, layer, final_norm, ff_tile=256):
    n_slabs, t, kw = a.shape
    d = h.shape[1]
    d_ff = w_down.shape[1]
    tm = ROW_TILE
    n_copies = 1 + 3 * (d_ff // ff_tile)
    in_hbm = pl.BlockSpec(memory_space=pl.ANY)
    return pl.pallas_call(
        functools.partial(_tail_kernel, layer=layer, final_norm=final_norm, ff_tile=ff_tile),
        grid=(t // tm,),
        in_specs=[
            pl.BlockSpec((n_slabs, tm, kw), lambda i: (0, i, 0)),
            in_hbm,
            pl.BlockSpec((tm, d), lambda i: (i, 0)),
            _resident((1, d)),
            in_hbm,
            in_hbm,
            _resident((1, d)),
        ],
        out_specs=pl.BlockSpec((tm, d), lambda i: (i, 0)),
        out_shape=jax.ShapeDtypeStruct((t, d), _F32),
        scratch_shapes=[
            pltpu.VMEM((tm, d), _BF16),
            pltpu.VMEM(w_out.shape, w_out.dtype),
            pltpu.VMEM(w_in.shape[1:], w_in.dtype),
            pltpu.VMEM(w_down.shape[1:], w_down.dtype),
            pltpu.SemaphoreType.DMA((n_copies,)),
        ],
        compiler_params=_params("arbitrary"),
        name="block_tail",
    )(a, w_out, h, gain.reshape(1, d), w_in, w_down, final_gain.reshape(1, d))


def _attn_kernel(*refs):
    ins = refs[:15]
    out_ref, o_scr, l_scr = refs[15:]
    first_block = pl.program_id(1) == 0
    row = lax.broadcasted_iota(jnp.int32, (SPAN, SPAN), 0)
    col = lax.broadcasted_iota(jnp.int32, (SPAN, SPAN), 1)
    bias_cur = jnp.where(col <= row, 0.0, NEG_BIG).astype(_F32)
    bias_prev = jnp.where(col >= row, 0.0, NEG_BIG).astype(_F32)
    bias_halo = bias_prev + jnp.where(first_block, NEG_BIG, 0.0).astype(_F32)
    ones = jnp.ones((SPAN, HEAD_DIM), _BF16)
    chain_len = 4
    body_units = 16

    for g, dil in enumerate(DILATIONS):
        q_ref, k_ref, v_ref, kh_ref, vh_ref = ins[5 * g:5 * g + 5]

        def load(ref, start, dil=dil):
            idx = pl.ds(start, SPAN) if dil == 1 else pl.ds(start, SPAN, stride=dil)
            return ref[idx, :].astype(_BF16)

        def load_kv(kref, vref, start, load=load):
            return load(kref, start), jnp.concatenate([load(vref, start), ones], axis=1)

        def chain(starts, prev, prev_bias, g=g, dil=dil, q_ref=q_ref, k_ref=k_ref, v_ref=v_ref,
                  load=load, load_kv=load_kv):
            for start in starts:
                kp, vp = prev
                kc, vc = cur = load_kv(k_ref, v_ref, start)
                s = _dot_nt(load(q_ref, start), jnp.concatenate([kp, kc], axis=0))
                s = s + jnp.concatenate([prev_bias, bias_cur], axis=1)
                m = jnp.max(jnp.maximum(s[:, :SPAN], s[:, SPAN:]), axis=-1, keepdims=True)
                p = jnp.exp2(s - m).astype(_BF16)
                r = _dot(p, jnp.concatenate([vp, vc], axis=0))
                denom = r[:, HEAD_DIM:]
                idx = pl.ds(start, SPAN) if dil == 1 else pl.ds(start, SPAN, stride=dil)
                o_scr[g, idx, :] = r[:, :HEAD_DIM] / denom
                l_scr[g, idx, :] = m + jnp.log2(denom)
                prev, prev_bias = cur, bias_prev

        block_rows = SPAN * dil
        if dil == DILATIONS[-1]:
            def body(i, carry, chain=chain, load_kv=load_kv, kh_ref=kh_ref, vh_ref=vh_ref):
                for u in range(body_units):
                    r = i * body_units + u
                    chain([r], load_kv(kh_ref, vh_ref, r), bias_halo)
                return carry
            lax.fori_loop(0, dil // body_units, body, 0)
        elif dil > 1:
            assert DILATIONS[-1] // dil == chain_len

            def body(i, carry, chain=chain, load_kv=load_kv, kh_ref=kh_ref, vh_ref=vh_ref,
                     block_rows=block_rows):
                for u in range(body_units // chain_len):
                    r = i * (body_units // chain_len) + u
                    chain([r + b * block_rows for b in range(chain_len)],
                          load_kv(kh_ref, vh_ref, r), bias_halo)
                return carry
            lax.fori_loop(0, dil * chain_len // body_units, body, 0)
        else:
            for first in range(0, DILATIONS[-1], chain_len):
                prev = (load_kv(kh_ref, vh_ref, 0) if first == 0
                        else load_kv(k_ref, v_ref, (first - 1) * block_rows))
                chain([(first + b) * block_rows for b in range(chain_len)], prev,
                      bias_halo if first == 0 else bias_prev)

    merge_rows = 256

    def merge(ci, carry):
        rr = pl.ds(pl.multiple_of(ci * merge_rows, merge_rows), merge_rows)
        l0, l1, l2 = l_scr[0, rr, :], l_scr[1, rr, :], l_scr[2, rr, :]
        m = jnp.maximum(jnp.maximum(l0, l1), l2)
        e0, e1, e2 = jnp.exp2(l0 - m), jnp.exp2(l1 - m), jnp.exp2(l2 - m)
        inv = 1.0 / (e0 + e1 + e2)
        out_ref[0, rr, :] = (o_scr[0, rr, :] * (e0 * inv)).astype(out_ref.dtype)
        out_ref[1, rr, :] = (o_scr[1, rr, :] * (e1 * inv)).astype(out_ref.dtype)
        out_ref[2, rr, :] = (o_scr[2, rr, :] * (e2 * inv)).astype(out_ref.dtype)
        return carry

    lax.fori_loop(0, ATTN_BLOCK // merge_rows, merge, 0)


def _attention(qkv, *, batch):
    t = qkv.shape[0]
    tb = ATTN_BLOCK
    steps = t // batch // tb
    in_specs, operands = [], []
    for g, dil in enumerate(DILATIONS):
        halo = SPAN * dil
        ratio = tb // halo

        def cur(which, g=g):
            return pl.BlockSpec(
                (tb, HEAD_DIM),
                lambda b, i, j: (b * steps + i, which * ATTN_HEADS + g * HEADS_PER_GROUP + j))

        def prev(which, g=g, ratio=ratio, halo=halo):
            return pl.BlockSpec(
                (halo, HEAD_DIM),
                lambda b, i, j: (jnp.maximum((b * steps + i) * ratio - 1, 0),
                                 which * ATTN_HEADS + g * HEADS_PER_GROUP + j))

        in_specs += [cur(0), cur(1), cur(2), prev(1), prev(2)]
        operands += [qkv] * 5
    n_groups = len(DILATIONS)
    return pl.pallas_call(
        _attn_kernel,
        grid=(batch, steps, HEADS_PER_GROUP),
        in_specs=in_specs,
        out_specs=pl.BlockSpec((n_groups, tb, HEAD_DIM), lambda b, i, j: (0, b * steps + i, j)),
        out_shape=jax.ShapeDtypeStruct((n_groups, t, HEADS_PER_GROUP * HEAD_DIM), _BF16),
        scratch_shapes=[pltpu.VMEM((3, tb, HEAD_DIM), _F32), pltpu.VMEM((3, tb, HEAD_DIM), _F32)],
        compiler_params=_params("parallel", "arbitrary", "arbitrary"),
        name="dilated_attention",
    )(*operands)


def _rope_tables(seq_len):
    inv_freq = 1.0 / (ROPE_THETA ** (np.arange(0, HEAD_DIM, 2, dtype=np.float64) / HEAD_DIM))
    ang = np.arange(seq_len, dtype=np.float64)[:, None] * inv_freq[None, :]
    cos, sin = np.cos(ang), np.sin(ang)
    return (jnp.asarray(np.concatenate([cos, cos], axis=-1), _F32),
            jnp.asarray(np.concatenate([-sin, sin], axis=-1), _F32))


def kernel(x, norm_mix, norm_ffn, hgrn_w_in, hgrn_lb_logits, hgrn_out_norm, hgrn_w_out,
           attn_w_qkv, attn_w_out, ffn_w_in, ffn_w_down, final_norm):
    batch, seq, d = x.shape
    cos2, sin2 = _rope_tables(seq)
    h = x.reshape(batch * seq, d)

    gated = _hgrn_mixer(h, norm_mix[0], hgrn_w_in[0], hgrn_lb_logits, hgrn_out_norm[0],
                        batch=batch, layer=0)
    ffn_in, ffn_down = ffn_w_in, ffn_w_down
    h = _block_tail(gated[None], hgrn_w_out[0], h, norm_ffn[0], ffn_in, ffn_down, final_norm,
                    layer=0, final_norm=False)

    qkv = _qkv_projection(h, norm_mix[1], attn_w_qkv[0], cos2, sin2,
                          scale=HEAD_DIM ** -0.5 * LOG2_E)
    attn = _attention(qkv, batch=batch)
    h = _block_tail(attn, attn_w_out[0], h, norm_ffn[1], ffn_in, ffn_down, final_norm,
                    layer=1, final_norm=True)
    return h.reshape(batch, seq, d)
```
